```python
import math
import jax
import jax.numpy as jnp
from jax import lax
import numpy as np

D_MODEL = 1024
BATCH = 8
SEQ = 2048
DEPTH = 4
DEC_BATCH = 128
DEC_SEQ = 4
PAST_LEN = 8192
PAGE_SIZE = 128

N_MIX_GROUPS = 4
D_MIX = D_MODEL
D_GROUP = D_MIX // N_MIX_GROUPS

SSM_CH = 16
SSM_GROUPS = D_GROUP // SSM_CH
SSM_STATE = 64
SSM_DT_MIN = 1e-3
SSM_DT_MAX = 1e-1

HG_HEADS = 4
HG_DIM = D_GROUP // HG_HEADS
HG_CHUNK = 64

GD_HEADS = 4
GD_DK = D_GROUP // GD_HEADS
GD_DV = D_GROUP // GD_HEADS
GD_CONV = 4
GD_CHUNK = 64

SW_HEADS = 4
SW_KV_HEADS = 2
SW_QPK = SW_HEADS // SW_KV_HEADS
SW_HD = D_GROUP // SW_HEADS
SW_WINDOW = 128

D_FF = 2816
NORM_EPS = 1e-6

IN_SIZES = (D_GROUP,
            D_GROUP, D_GROUP, D_GROUP, D_GROUP,
            GD_HEADS * GD_DK, GD_HEADS * GD_DK, GD_HEADS * GD_DV, GD_HEADS * GD_DV,
            GD_HEADS, GD_HEADS,
            SW_HEADS * SW_HD, SW_KV_HEADS * SW_HD, SW_KV_HEADS * SW_HD)
D_IN = sum(IN_SIZES)

STATE_NAMES = ("ssm_re", "ssm_im", "hgrn", "gdn", "gdn_conv", "swa_k", "swa_v")

kernel_name = "hymba_s5_hgrn2_gdn_swa_macaron_step"


def _f32(a):
    return a.astype(jnp.float32)


def rms_norm(x, w):
    xf = _f32(x)
    y = xf * lax.rsqrt(jnp.mean(xf * xf, axis=-1, keepdims=True) + NORM_EPS)
    return (y * _f32(w)).astype(x.dtype)


def l2_normalize(x):
    return x * lax.rsqrt(jnp.sum(x * x, axis=-1, keepdims=True) + NORM_EPS)


def swiglu(x, w_gate, w_up, w_down):
    return (jax.nn.silu(x @ w_gate) * (x @ w_up)) @ w_down


def _pad_time(t, n, c):
    pad = n * c - t.shape[1]
    if pad:
        t = jnp.pad(t, [(0, 0), (0, pad)] + [(0, 0)] * (t.ndim - 2))
    return t


def _to_chunks(t, n, c):
    b, _, h = t.shape[:3]
    t = t.reshape((b, n, c, h) + t.shape[3:])
    return t.transpose((1, 0, 3, 2) + tuple(range(4, t.ndim)))


def _from_chunks(t, length):
    n, b, h, c, d = t.shape
    return t.transpose(1, 0, 3, 2, 4).reshape(b, n * c, h, d)[:, :length]


def _complex_affine_combine(earlier, later):
    a1r, a1i, b1r, b1i = earlier
    a2r, a2i, b2r, b2i = later
    return (a2r * a1r - a2i * a1i,
            a2r * a1i + a2i * a1r,
            a2r * b1r - a2i * b1i + b2r,
            a2r * b1i + a2i * b1r + b2i)


def s5_mixer(u, lw, h_re, h_im):
    bsz, length, _ = u.shape
    lam_re, lam_im = _f32(lw["ssm_lambda_re"]), _f32(lw["ssm_lambda_im"])
    dt = jnp.exp(_f32(lw["ssm_log_dt"]))[:, None]
    mag = jnp.exp(lam_re * dt)
    ang = lam_im * dt
    a_re, a_im = mag * jnp.cos(ang), mag * jnp.sin(ang)
    den = lam_re * lam_re + lam_im * lam_im
    z_re = ((a_re - 1.0) * lam_re + a_im * lam_im) / den
    z_im = (a_im * lam_re - (a_re - 1.0) * lam_im) / den
    b_re, b_im = _f32(lw["ssm_b_re"]), _f32(lw["ssm_b_im"])
    bb_re = z_re[..., None] * b_re - z_im[..., None] * b_im
    bb_im = z_re[..., None] * b_im + z_im[..., None] * b_re
    ug = u.reshape(bsz, length, SSM_GROUPS, SSM_CH)
    bu_re = jnp.einsum("blgc,gpc->blgp", ug, bb_re)
    bu_im = jnp.einsum("blgc,gpc->blgp", ug, bb_im)
    bu_re = bu_re.at[:, 0].add(a_re * h_re - a_im * h_im)
    bu_im = bu_im.at[:, 0].add(a_re * h_im + a_im * h_re)
    a_re_t = jnp.broadcast_to(a_re, bu_re.shape)
    a_im_t = jnp.broadcast_to(a_im, bu_im.shape)
    _, _, s_re, s_im = lax.associative_scan(
        _complex_affine_combine, (a_re_t, a_im_t, bu_re, bu_im), axis=1)
    y = (jnp.einsum("blgp,gcp->blgc", s_re, _f32(lw["ssm_c_re"]))
         - jnp.einsum("blgp,gcp->blgc", s_im, _f32(lw["ssm_c_im"])))
    y = y.reshape(bsz, length, D_GROUP) + _f32(lw["ssm_d"]) * u
    y = jax.nn.gelu(y)
    y = (y @ _f32(lw["ssm_w_glu1"])) * jax.nn.sigmoid(y @ _f32(lw["ssm_w_glu2"]))
    return y, s_re[:, -1], s_im[:, -1]


def diag_decay_chunked(q, k, v, log_f, s0):
    length = q.shape[1]
    c = min(HG_CHUNK, length)
    n = -(-length // c)
    qc, kc, vc, gc = (_to_chunks(_pad_time(t, n, c), n, c) for t in (q, k, v, log_f))
    cum = jnp.cumsum(gc, axis=3)
    incl = jnp.tril(jnp.ones((c, c), bool))

    def step(s, xs):
        qi, ki, vi, ci = xs
        diff = ci[:, :, :, None, :] - ci[:, :, None, :, :]
        dec = jnp.exp(jnp.where(incl[:, :, None], diff, -jnp.inf))
        att = jnp.einsum("bhtk,bhtsk->bhts", qi, dec * ki[:, :, None, :, :])
        o = att @ vi + (qi * jnp.exp(ci)) @ s
        cl = ci[:, :, -1:, :]
        s = (jnp.exp(cl[:, :, 0, :, None]) * s
             + jnp.einsum("bhsk,bhsv->bhkv", ki * jnp.exp(cl - ci), vi))
        return s, o

    s, o = lax.scan(step, s0, (qc, kc, vc, cum))
    return _from_chunks(o, length), s


def hgrn2_mixer(q_raw, f_raw, i_raw, g_raw, lb, norm_w, s0):
    bsz, length, _ = q_raw.shape
    hs = (bsz, length, HG_HEADS, HG_DIM)
    f = lb + (1.0 - lb) * jax.nn.sigmoid(f_raw)
    k = (1.0 - lb) * jax.nn.sigmoid(-f_raw)
    q = jax.nn.silu(q_raw)
    o, s = diag_decay_chunked(q.reshape(hs), k.reshape(hs), i_raw.reshape(hs),
                              jnp.log(f).reshape(hs), s0)
    o = rms_norm(o, norm_w) * jax.nn.silu(g_raw.reshape(hs))
    return o.reshape(bsz, length, D_GROUP), s


def causal_short_conv(x, buf, w):
    length = x.shape[1]
    xp = jnp.concatenate([buf, x], axis=1)
    y = xp[:, 0:length] * w[0]
    for j in range(1, GD_CONV):
        y = y + xp[:, j:j + length] * w[j]
    return y, xp[:, length:]


def gated_delta_chunked(q, k, v, beta, g, s0):
    length, kd = q.shape[1], q.shape[3]
    c = min(GD_CHUNK, length)
    n = -(-length // c)
    q, k, v, beta, g = (_to_chunks(_pad_time(t, n, c), n, c) for t in (q, k, v, beta, g))
    gc = jnp.cumsum(g, axis=-1)
    incl = jnp.tril(jnp.ones((c, c), bool))
    strict = jnp.tril(jnp.ones((c, c), bool), k=-1)
    dec = jnp.exp(jnp.where(incl, gc[..., :, None] - gc[..., None, :], -jnp.inf))
    m_low = jnp.where(strict, dec, 0.0) * beta[..., :, None] * jnp.einsum("nbhtk,nbhsk->nbhts", k, k)
    a_mat = m_low + jnp.eye(c, dtype=jnp.float32)
    gam = jnp.exp(gc)[..., None]
    rhs = jnp.concatenate([beta[..., None] * gam * k, beta[..., None] * v], axis=-1)
    sol = lax.linalg.triangular_solve(a_mat, rhs, left_side=True, lower=True, unit_diagonal=True)
    w_mat, u0 = sol[..., :kd], sol[..., kd:]
    qk = jnp.einsum("nbhtk,nbhsk->nbhts", q, k) * dec
    q_g = q * gam
    k_t = k * jnp.exp(gc[..., -1:] - gc)[..., None]
    g_last = jnp.exp(gc[..., -1])[..., None, None]

    def step(s, xs):
        w_i, u0_i, qk_i, qg_i, kt_i, gl_i = xs
        u = u0_i - w_i @ s
        o = qg_i @ s + qk_i @ u
        s = gl_i * s + jnp.einsum("bhsk,bhsv->bhkv", kt_i, u)
        return s, o

    s, o = lax.scan(step, s0, (w_mat, u0, qk, q_g, k_t, g_last))
    return _from_chunks(o, length), s


def gdn_mixer(q_raw, k_raw, v_raw, z_raw, b_raw, a_raw, lw, s0, conv_buf):
    bsz, length, _ = q_raw.shape
    qkv = jnp.concatenate([q_raw, k_raw, v_raw], axis=-1)
    qkv, conv_new = causal_short_conv(qkv, conv_buf, _f32(lw["gdn_conv_w"]))
    qkv = jax.nn.silu(qkv)
    q, k, v = jnp.split(qkv, [GD_HEADS * GD_DK, 2 * GD_HEADS * GD_DK], axis=-1)
    q = l2_normalize(q.reshape(bsz, length, GD_HEADS, GD_DK)) * (GD_DK ** -0.5)
    k = l2_normalize(k.reshape(bsz, length, GD_HEADS, GD_DK))
    v = v.reshape(bsz, length, GD_HEADS, GD_DV)
    beta = jax.nn.sigmoid(b_raw)
    g = -jnp.exp(_f32(lw["gdn_a_log"])) * jax.nn.softplus(a_raw + _f32(lw["gdn_dt_bias"]))
    o, s = gated_delta_chunked(q, k, v, beta, g, s0)
    o = rms_norm(o, lw["gdn_norm_w"]) * jax.nn.silu(z_raw.reshape(bsz, length, GD_HEADS, GD_DV))
    return o.reshape(bsz, length, D_GROUP), s, conv_new


def alibi_slopes():
    m = 2.0 ** (-8.0 * np.arange(1, SW_HEADS + 1) / SW_HEADS)
    return jnp.asarray(m, jnp.float32).reshape(SW_KV_HEADS, SW_QPK)


def sink_softmax_attend(q, k, v, dist, valid, slopes, sinks):
    s = jnp.einsum("...qkgd,...skd->...kgqs", q, k) * (SW_HD ** -0.5)
    s = s - slopes[:, :, None, None] * dist
    s = jnp.where(valid, s, -jnp.inf)
    sink = jnp.broadcast_to(sinks[:, :, None, None], s.shape[:-1] + (1,))
    m = jnp.maximum(jnp.max(s, axis=-1, keepdims=True), sink)
    p = jnp.exp(s - m)
    denom = jnp.sum(p, axis=-1, keepdims=True) + jnp.exp(sink - m)
    return jnp.einsum("...kgqs,...skd->...qkgd", p / denom, v)


def swa_mixer(q_raw, k_raw, v_raw, sinks, buf_k, buf_v):
    bsz, length, _ = q_raw.shape
    win = SW_WINDOW
    q = q_raw.reshape(bsz, length, SW_KV_HEADS, SW_QPK, SW_HD)
    k = k_raw.reshape(bsz, length, SW_KV_HEADS, SW_HD)
    v = v_raw.reshape(bsz, length, SW_KV_HEADS, SW_HD)
    slopes = alibi_slopes()
    sinks = _f32(sinks).reshape(SW_KV_HEADS, SW_QPK)
    if buf_k is None:
        nb = length // win
        qb = q.reshape(bsz, nb, win, SW_KV_HEADS, SW_QPK, SW_HD)
        kb = k.reshape(bsz, nb, win, SW_KV_HEADS, SW_HD)
        vb = v.reshape(bsz, nb, win, SW_KV_HEADS, SW_HD)
        k2 = jnp.concatenate([jnp.concatenate([jnp.zeros_like(kb[:, :1]), kb[:, :-1]], 1), kb], 2)
        v2 = jnp.concatenate([jnp.concatenate([jnp.zeros_like(vb[:, :1]), vb[:, :-1]], 1), vb], 2)
        i = jnp.arange(win)[:, None]
        j = jnp.arange(2 * win)[None, :]
        dist = win + i - j
        valid = (dist >= 0) & (dist <= win) & ((jnp.arange(nb)[:, None, None] > 0) | (j >= win))
        o = sink_softmax_attend(qb, k2, v2, dist.astype(jnp.float32), valid[:, None, None], slopes, sinks)
        return o.reshape(bsz, length, D_GROUP), k[:, length - win:], v[:, length - win:]
    kk = jnp.concatenate([_f32(buf_k), k], axis=1)
    vv = jnp.concatenate([_f32(buf_v), v], axis=1)
    i = jnp.arange(length)[:, None]
    j = jnp.arange(win + length)[None, :]
    dist = win + i - j
    valid = (dist >= 0) & (dist <= win)
    o = sink_softmax_attend(q, kk, vv, dist.astype(jnp.float32), valid, slopes, sinks)
    return o.reshape(bsz, length, D_GROUP), kk[:, -win:], vv[:, -win:]


def token_mixing(h, lw, lb, st):
    proj = _f32(h @ lw["w_in"])
    points = np.cumsum(IN_SIZES)[:-1].tolist()
    (u_a, q_b, f_b, i_b, g_b, q_c, k_c, v_c, z_c, b_c, a_c, q_d, k_d, v_d) = jnp.split(proj, points, axis=-1)
    y_a, s_re, s_im = s5_mixer(u_a, lw, _f32(st["ssm_re"]), _f32(st["ssm_im"]))
    y_b, s_hg = hgrn2_mixer(q_b, f_b, i_b, g_b, lb, lw["hgrn_norm_w"], _f32(st["hgrn"]))
    y_c, s_gd, conv_new = gdn_mixer(q_c, k_c, v_c, z_c, b_c, a_c, lw, _f32(st["gdn"]), _f32(st["gdn_conv"]))
    y_d, ck, cv = swa_mixer(q_d, k_d, v_d, lw["swa_sinks"], st["swa_k"], st["swa_v"])
    mix = jnp.concatenate([y_a, y_b, y_c, y_d], axis=-1).astype(h.dtype) @ lw["w_out"]
    new = {"ssm_re": s_re, "ssm_im": s_im, "hgrn": s_hg, "gdn": s_gd,
           "gdn_conv": conv_new, "swa_k": ck, "swa_v": cv}
    return mix, new


def run_trunk(x, st, params):
    gam = jax.nn.softmax(_f32(params["hgrn_lb_logits"]), axis=0)
    lbs = jnp.cumsum(gam, axis=0) - gam[:1]
    outs = {n: [] for n in STATE_NAMES}
    for l in range(DEPTH):
        lw = {n: a[l] for n, a in params.items()}
        stl = {n: (None if st[n] is None else st[n][l]) for n in STATE_NAMES}
        gn = lw["norm_gains"]
        h = rms_norm(x, gn[0])
        x = x + 0.5 * rms_norm(swiglu(h, lw["ffn_w_gate"][0], lw["ffn_w_up"][0], lw["ffn_w_down"][0]), gn[1])
        h = rms_norm(x, gn[2])
        mix, new = token_mixing(h, lw, lbs[l], stl)
        x = x + rms_norm(mix, gn[3])
        h = rms_norm(x, gn[4])
        x = x + 0.5 * rms_norm(swiglu(h, lw["ffn_w_gate"][1], lw["ffn_w_up"][1], lw["ffn_w_down"][1]), gn[5])
        for n in STATE_NAMES:
            outs[n].append(new[n].astype(x.dtype))
    return x, tuple(jnp.stack(outs[n], axis=0) for n in STATE_NAMES)


def setup_inputs(seed: int = 0) -> dict:
    key = jax.random.key(seed)
    keys = iter(jax.random.split(key, 48))
    f32 = jnp.float32

    def nrm(shape, scale):
        return jax.random.normal(next(keys), shape, f32) * scale

    def unif(shape, lo, hi):
        return jax.random.uniform(next(keys), shape, f32, lo, hi)

    G, P, C = SSM_GROUPS, SSM_STATE, SSM_CH
    n_idx = jnp.arange(P, dtype=f32)
    dt_gd = jnp.exp(unif((DEPTH, GD_HEADS), math.log(1e-3), math.log(1e-1)))
    return {
        "x_prompt": nrm((BATCH, SEQ, D_MODEL), 1.0),
        "x_sample": nrm((DEC_BATCH, DEC_SEQ, D_MODEL), 1.0),
        "state_ssm_re": nrm((DEPTH, DEC_BATCH, G, P), 0.3),
        "state_ssm_im": nrm((DEPTH, DEC_BATCH, G, P), 0.3),
        "state_hgrn": nrm((DEPTH, DEC_BATCH, HG_HEADS, HG_DIM, HG_DIM), 0.3),
        "state_gdn": nrm((DEPTH, DEC_BATCH, GD_HEADS, GD_DK, GD_DV), 0.1),
        "state_gdn_conv": nrm((DEPTH, DEC_BATCH, GD_CONV - 1, 3 * D_GROUP), 1.0),
        "cache_swa_k": nrm((DEPTH, DEC_BATCH, SW_WINDOW, SW_KV_HEADS, SW_HD), 1.0),
        "cache_swa_v": nrm((DEPTH, DEC_BATCH, SW_WINDOW, SW_KV_HEADS, SW_HD), 1.0),
        "norm_gains": 1.0 + nrm((DEPTH, 6, D_MODEL), 0.05),
        "ffn_w_gate": nrm((DEPTH, 2, D_MODEL, D_FF), D_MODEL ** -0.5),
        "ffn_w_up": nrm((DEPTH, 2, D_MODEL, D_FF), D_MODEL ** -0.5),
        "ffn_w_down": nrm((DEPTH, 2, D_FF, D_MODEL), D_FF ** -0.5),
        "w_in": nrm((DEPTH, D_MODEL, D_IN), D_MODEL ** -0.5),
        "w_out": nrm((DEPTH, D_MIX, D_MODEL), D_MIX ** -0.5),
        "ssm_lambda_re": -0.5 + nrm((DEPTH, G, P), 0.01),
        "ssm_lambda_im": math.pi * n_idx + nrm((DEPTH, G, P), 0.01),
        "ssm_log_dt": unif((DEPTH, G), math.log(SSM_DT_MIN), math.log(SSM_DT_MAX)),
        "ssm_b_re": nrm((DEPTH, G, P, C), (2 * C) ** -0.5),
        "ssm_b_im": nrm((DEPTH, G, P, C), (2 * C) ** -0.5),
        "ssm_c_re": nrm((DEPTH, G, C, P), P ** -0.5),
        "ssm_c_im": nrm((DEPTH, G, C, P), P ** -0.5),
        "ssm_d": nrm((DEPTH, D_GROUP), 1.0),
        "ssm_w_glu1": nrm((DEPTH, D_GROUP, D_GROUP), D_GROUP ** -0.5),
        "ssm_w_glu2": nrm((DEPTH, D_GROUP, D_GROUP), D_GROUP ** -0.5),
        "hgrn_lb_logits": nrm((DEPTH, D_GROUP), 0.5),
        "hgrn_norm_w": 1.0 + nrm((DEPTH, HG_DIM), 0.05),
        "gdn_conv_w": nrm((DEPTH, GD_CONV, 3 * D_GROUP), GD_CONV ** -0.5),
        "gdn_a_log": jnp.log(unif((DEPTH, GD_HEADS), 1.0, 16.0)),
        "gdn_dt_bias": dt_gd + jnp.log(-jnp.expm1(-dt_gd)),
        "gdn_norm_w": 1.0 + nrm((DEPTH, GD_DV), 0.05),
        "swa_sinks": nrm((DEPTH, SW_HEADS), 0.5),
    }


def reference(x_prompt, x_sample, state_ssm_re, state_ssm_im, state_hgrn, state_gdn, state_gdn_conv,
              cache_swa_k, cache_swa_v, norm_gains, ffn_w_gate, ffn_w_up, ffn_w_down, w_in, w_out,
              ssm_lambda_re, ssm_lambda_im, ssm_log_dt, ssm_b_re, ssm_b_im, ssm_c_re, ssm_c_im, ssm_d,
              ssm_w_glu1, ssm_w_glu2, hgrn_lb_logits, hgrn_norm_w, gdn_conv_w, gdn_a_log, gdn_dt_bias,
              gdn_norm_w, swa_sinks):
    params = {
        "norm_gains": norm_gains, "ffn_w_gate": ffn_w_gate, "ffn_w_up": ffn_w_up, "ffn_w_down": ffn_w_down,
        "w_in": w_in, "w_out": w_out,
        "ssm_lambda_re": ssm_lambda_re, "ssm_lambda_im": ssm_lambda_im, "ssm_log_dt": ssm_log_dt,
        "ssm_b_re": ssm_b_re, "ssm_b_im": ssm_b_im, "ssm_c_re": ssm_c_re, "ssm_c_im": ssm_c_im,
        "ssm_d": ssm_d, "ssm_w_glu1": ssm_w_glu1, "ssm_w_glu2": ssm_w_glu2,
        "hgrn_lb_logits": hgrn_lb_logits, "hgrn_norm_w": hgrn_norm_w,
        "gdn_conv_w": gdn_conv_w, "gdn_a_log": gdn_a_log, "gdn_dt_bias": gdn_dt_bias, "gdn_norm_w": gdn_norm_w,
        "swa_sinks": swa_sinks,
    }
    bp = x_prompt.shape[0]
    f32 = jnp.float32
    prompt_state = {
        "ssm_re": jnp.zeros((DEPTH, bp, SSM_GROUPS, SSM_STATE), f32),
        "ssm_im": jnp.zeros((DEPTH, bp, SSM_GROUPS, SSM_STATE), f32),
        "hgrn": jnp.zeros((DEPTH, bp, HG_HEADS, HG_DIM, HG_DIM), f32),
        "gdn": jnp.zeros((DEPTH, bp, GD_HEADS, GD_DK, GD_DV), f32),
        "gdn_conv": jnp.zeros((DEPTH, bp, GD_CONV - 1, 3 * D_GROUP), f32),
        "swa_k": None,
        "swa_v": None,
    }
    sample_state = {
        "ssm_re": state_ssm_re, "ssm_im": state_ssm_im, "hgrn": state_hgrn, "gdn": state_gdn,
        "gdn_conv": state_gdn_conv, "swa_k": cache_swa_k, "swa_v": cache_swa_v,
    }
    y_prompt, (sre_p, sim_p, hg_p, gd_p, cv_p, sk_p, sv_p) = run_trunk(x_prompt, prompt_state, params)
    y_sample, (sre_s, sim_s, hg_s, gd_s, cv_s, sk_s, sv_s) = run_trunk(x_sample, sample_state, params)
    return (y_prompt, y_sample, sre_p, sre_s, sim_p, sim_s, hg_p, hg_s, gd_p, gd_s,
            cv_p, cv_s, sk_p, sk_s, sv_p, sv_s)
```

```python
import functools
import math

import numpy as np
import jax
import jax.numpy as jnp
from jax import lax
from jax.experimental import pallas as pl
from jax.experimental.pallas import tpu as pltpu

D_MODEL = 1024
DEPTH = 4
D_GROUP = 256
N_HEADS = 4
HEAD_DIM = 64
SSM_CH = 16
SSM_GROUPS = 16
SSM_STATE = 64
SSM_LANES = SSM_GROUPS * SSM_STATE
GD_CONV = 4
SW_WINDOW = 128
SW_KV_HEADS = 2
D_FF = 2816
NORM_EPS = 1e-6
PROJ_W = 12 * D_GROUP
SUBLANES = 8
VMEM_LIMIT = 56 * 1024 * 1024

F32 = jnp.float32
BF16 = jnp.bfloat16
EXACT = lax.Precision.HIGHEST


def _mm(a, b):
    return jnp.dot(a.astype(BF16), b.astype(BF16), preferred_element_type=F32)


def _mm_nt(a, b):
    return lax.dot_general(a.astype(BF16), b.astype(BF16), (((1,), (1,)), ((), ())),
                           preferred_element_type=F32)


def _mm_tn(a, b):
    return lax.dot_general(a.astype(BF16), b.astype(BF16), (((0,), (0,)), ((), ())),
                           preferred_element_type=F32)


def _mmx(a, b):
    return jnp.dot(a, b, precision=EXACT, preferred_element_type=F32)


def _mmx_nt(a, b):
    return lax.dot_general(a, b, (((1,), (1,)), ((), ())), precision=EXACT,
                           preferred_element_type=F32)


def _rms(x, gain):
    return x * lax.rsqrt(jnp.mean(x * x, axis=-1, keepdims=True) + NORM_EPS) * gain


def _silu(x):
    return x * jax.nn.sigmoid(x)


def _div2(x, d):
    assert d & (d - 1) == 0
    return x >> (d.bit_length() - 1)


def _mod2(x, d):
    assert d & (d - 1) == 0
    return x & (d - 1)


def _head_masks(width=D_GROUP, head_dim=HEAD_DIM):
    lane = lax.broadcasted_iota(jnp.int32, (1, width), 1)
    return [(_div2(lane, head_dim) == h).astype(F32) for h in range(width // head_dim)]


def _stack_heads(x, masks):
    return jnp.concatenate([x * m for m in masks], axis=0)


def _unstack_heads(x, n_heads):
    c = x.shape[0] // n_heads
    out = x[0:c]
    for h in range(1, n_heads):
        out = out + x[h * c:(h + 1) * c]
    return out


def _lane_col(x, lane):
    idx = lax.broadcasted_iota(jnp.int32, x.shape, 1)
    return jnp.sum(jnp.where(idx == lane, x, 0.0), axis=-1, keepdims=True)


def _const_spec(shape):
    return pl.BlockSpec(shape, lambda *_: (0,) * len(shape), pipeline_mode=pl.Buffered(1))


def _params(*sem):
    return pltpu.CompilerParams(dimension_semantics=sem, vmem_limit_bytes=VMEM_LIMIT)


def _ffn_kernel(x_ref, gpre_ref, gpost_ref, wg_ref, wu_ref, wd_ref, o_ref):
    x = x_ref[...]
    h = _rms(x, gpre_ref[...]).astype(BF16)
    g = jnp.dot(h, wg_ref[...], preferred_element_type=F32)
    u = jnp.dot(h, wu_ref[...], preferred_element_type=F32)
    a = (_silu(g) * u).astype(BF16)
    y = jnp.dot(a, wd_ref[...], preferred_element_type=F32)
    o_ref[...] = x + 0.5 * _rms(y, gpost_ref[...])


def _ffn(x, gpre, gpost, wg, wu, wd, tm):
    n = x.shape[0]
    row = pl.BlockSpec((tm, D_MODEL), lambda i: (i, 0))
    return pl.pallas_call(
        _ffn_kernel,
        grid=(n // tm,),
        in_specs=[row, _const_spec((1, D_MODEL)), _const_spec((1, D_MODEL)),
                  _const_spec((D_MODEL, D_FF)), _const_spec((D_MODEL, D_FF)),
                  _const_spec((D_FF, D_MODEL))],
        out_specs=row,
        out_shape=jax.ShapeDtypeStruct((n, D_MODEL), F32),
        compiler_params=_params("parallel"),
        name="ffn",
    )(x, gpre, gpost, wg, wu, wd)


def _inproj_kernel(x_ref, g_ref, w_ref, o_ref):
    h = _rms(x_ref[...], g_ref[...]).astype(BF16)
    o_ref[...] = jnp.dot(h, w_ref[...], preferred_element_type=F32)


def _inproj(x, gain, w, tm):
    n = x.shape[0]
    return pl.pallas_call(
        _inproj_kernel,
        grid=(n // tm,),
        in_specs=[pl.BlockSpec((tm, D_MODEL), lambda i: (i, 0)), _const_spec((1, D_MODEL)),
                  _const_spec((D_MODEL, PROJ_W))],
        out_specs=pl.BlockSpec((tm, PROJ_W), lambda i: (i, 0)),
        out_shape=jax.ShapeDtypeStruct((n, PROJ_W), F32),
        compiler_params=_params("parallel"),
        name="inproj",
    )(x, gain, w)


def _outproj_kernel(x_ref, ya_ref, yb_ref, yc_ref, yd_ref, w_ref, g_ref, o_ref):
    y = jnp.concatenate([ya_ref[...], yb_ref[...], yc_ref[...], yd_ref[...]], axis=-1)
    mix = jnp.dot(y.astype(BF16), w_ref[...], preferred_element_type=F32)
    o_ref[...] = x_ref[...] + _rms(mix, g_ref[...])


def _outproj(x, ys, w, gain, tm):
    n = x.shape[0]
    row = pl.BlockSpec((tm, D_MODEL), lambda i: (i, 0))
    grp = pl.BlockSpec((tm, D_GROUP), lambda i: (i, 0))
    return pl.pallas_call(
        _outproj_kernel,
        grid=(n // tm,),
        in_specs=[row, grp, grp, grp, grp, _const_spec((D_MODEL, D_MODEL)), _const_spec((1, D_MODEL))],
        out_specs=row,
        out_shape=jax.ShapeDtypeStruct((n, D_MODEL), F32),
        compiler_params=_params("parallel"),
        name="outproj",
    )(x, *ys, w, gain)


class _SeqPlan:
    def __init__(self, n_seq, length, l_pad, seq_blk, t_blk, chunk):
        assert l_pad % t_blk == 0 and t_blk % chunk == 0 and n_seq % seq_blk == 0
        assert seq_blk == 1 or t_blk == l_pad
        assert chunk % SUBLANES == 0 and 0 <= l_pad - length < SUBLANES
        assert length % chunk == 0 or l_pad == chunk
        self.n_seq, self.length, self.l_pad = n_seq, length, l_pad
        self.seq_blk, self.t_blk, self.chunk = seq_blk, t_blk, chunk
        self.rows = seq_blk * t_blk
        self.t_steps = l_pad // t_blk
        self.grid = (n_seq // seq_blk, self.t_steps)
        self.chunks_per_seq = t_blk // chunk
        self.n_chunks = self.rows // chunk

    def rows_spec(self, width, col_block=0):
        ts = self.t_steps
        return pl.BlockSpec((self.rows, width), lambda i, j: (i * ts + j, col_block))

    def seq_spec(self, *tail):
        zeros = (0,) * len(tail)
        return pl.BlockSpec((self.seq_blk,) + tail, lambda i, j: (i,) + zeros)

    def valid_rows(self, j, n, size):
        t0 = j * self.t_blk + (n % self.chunks_per_seq) * self.chunk
        t = t0 + lax.broadcasted_iota(jnp.int32, (size, 1), 0)
        return t < self.length


def _seq_call(kernel, plan, in_specs, out_specs, out_shape, scratch, name):
    return pl.pallas_call(
        kernel, grid=plan.grid, in_specs=in_specs, out_specs=out_specs, out_shape=out_shape,
        scratch_shapes=scratch, compiler_params=_params("parallel", "arbitrary"), name=name)


def _s5_kernel(u_ref, h0_ref, bmat_ref, cmat_ref, d_ref, w1_ref, w2_ref, hs_ref, cp_ref,
               y_ref, hn_ref, s_scr, carry_scr, fin_scr, *, plan):
    j = pl.program_id(1)
    P = SSM_LANES

    @pl.when(j == 0)
    def _():
        carry_scr[...] = h0_ref[...]

    u = u_ref[...]
    s_scr[...] = _mm(u, bmat_ref[...])
    groups_per_seq = plan.t_blk // SUBLANES
    fin_row = (plan.length - 1) % SUBLANES

    def group(gi, carry):
        s = gi // groups_per_seq
        r0 = pl.multiple_of(gi * SUBLANES, SUBLANES)
        x_re = s_scr[pl.ds(r0, SUBLANES), 0:P]
        x_im = s_scr[pl.ds(r0, SUBLANES), P:2 * P]
        for lvl, d in enumerate((1, 2, 4)):
            a_re, a_im = hs_ref[2 * lvl], hs_ref[2 * lvl + 1]
            r_re, r_im = pltpu.roll(x_re, d, 0), pltpu.roll(x_im, d, 0)
            x_re, x_im = (x_re + a_re * r_re - a_im * r_im,
                          x_im + a_re * r_im + a_im * r_re)
        c = carry_scr[s]
        c_re, c_im = c[:, 0:P], c[:, P:2 * P]
        p_re, p_im = cp_ref[0], cp_ref[1]
        x_re, x_im = (x_re + p_re * c_re - p_im * c_im,
                      x_im + p_re * c_im + p_im * c_re)
        s_scr[pl.ds(r0, SUBLANES), 0:P] = x_re
        s_scr[pl.ds(r0, SUBLANES), P:2 * P] = x_im
        last = SUBLANES - 1
        carry_scr[s] = jnp.concatenate([x_re[last:last + 1], x_im[last:last + 1]], axis=-1)
        fin_scr[s] = jnp.concatenate([x_re[fin_row:fin_row + 1], x_im[fin_row:fin_row + 1]], axis=-1)
        return carry

    lax.fori_loop(0, plan.rows // SUBLANES, group, 0)

    y = _mm(s_scr[...], cmat_ref[...]) + d_ref[...] * u
    y = jax.nn.gelu(y)
    y = _mm(y, w1_ref[...]) * jax.nn.sigmoid(_mm(y, w2_ref[...]))
    t = j * plan.t_blk + _mod2(lax.broadcasted_iota(jnp.int32, (plan.rows, 1), 0), plan.t_blk)
    y_ref[...] = jnp.where(t < plan.length, y, 0.0)

    @pl.when(j == plan.t_steps - 1)
    def _():
        hn_ref[...] = fin_scr[...]


def _s5(proj, h0, consts, plan):
    n_rows = plan.n_seq * plan.l_pad
    P2 = 2 * SSM_LANES
    in_specs = [plan.rows_spec(D_GROUP, 0), plan.seq_spec(1, P2),
                _const_spec((D_GROUP, P2)), _const_spec((P2, D_GROUP)), _const_spec((1, D_GROUP)),
                _const_spec((D_GROUP, D_GROUP)), _const_spec((D_GROUP, D_GROUP)),
                _const_spec((6, SUBLANES, SSM_LANES)), _const_spec((2, SUBLANES, SSM_LANES))]
    out_specs = [plan.rows_spec(D_GROUP), plan.seq_spec(1, P2)]
    out_shape = [jax.ShapeDtypeStruct((n_rows, D_GROUP), F32),
                 jax.ShapeDtypeStruct((plan.n_seq, 1, P2), F32)]
    scratch = [pltpu.VMEM((plan.rows, P2), F32), pltpu.VMEM((plan.seq_blk, 1, P2), F32),
               pltpu.VMEM((plan.seq_blk, 1, P2), F32)]
    return _seq_call(functools.partial(_s5_kernel, plan=plan), plan, in_specs, out_specs, out_shape,
                     scratch, "s5")(proj, h0, *consts)


def _load_state_bd(s0, tile_mat, bd_mask):
    return _mmx(s0, tile_mat) * bd_mask


def _store_state_bd(s_bd, tile_mat_t):
    return _mmx(s_bd, tile_mat_t)


def _hgrn_kernel(q_ref, f_ref, i_ref, g_ref, s0_ref, lb_ref, nw_ref, bd_ref, tile_ref, tilet_ref, tri_ref,
                 y_ref, sn_ref, st_scr, *, plan, sub):
    j = pl.program_id(1)
    C = plan.chunk
    masks = _head_masks()
    bd = bd_ref[...]
    lb = lb_ref[...]

    @pl.when(j == 0)
    def _():
        def init(s, c):
            st_scr[s] = _load_state_bd(s0_ref[s], tile_ref[...], bd)
            return c
        lax.fori_loop(0, plan.seq_blk, init, 0)

    def chunk(n, carry):
        s = n // plan.chunks_per_seq
        rows = pl.ds(pl.multiple_of(n * C, C), C)
        valid = plan.valid_rows(j, n, C)
        fr = f_ref[rows, :]
        f = lb + (1.0 - lb) * jax.nn.sigmoid(fr)
        k = jnp.where(valid, (1.0 - lb) * jax.nn.sigmoid(-fr), 0.0)
        logf = jnp.where(valid, jnp.log(f), 0.0)
        q = _silu(q_ref[rows, :])
        v = i_ref[rows, :]
        cum = _mmx(tri_ref[...], logf)
        cum_last = cum[C - 1:C]
        st = st_scr[s]
        o = _mm_nt(q * jnp.exp(cum), st)
        intra = []
        for blk in range(C // sub):
            r0, r1 = blk * sub, (blk + 1) * sub
            c0 = cum[r0 - 1:r0] if blk else jnp.zeros_like(cum_last)
            qs = _stack_heads(q[r0:r1] * jnp.exp(cum[r0:r1] - c0), masks)
            ks = k[0:r1] * jnp.exp(c0 - cum[0:r1])
            att = _mm_nt(qs, ks)
            t_idx = r0 + _mod2(lax.broadcasted_iota(jnp.int32, att.shape, 0), sub)
            s_idx = lax.broadcasted_iota(jnp.int32, att.shape, 1)
            att = jnp.where(s_idx <= t_idx, att, 0.0)
            intra.append(_unstack_heads(_mm(att, v[0:r1]) * jnp.concatenate(
                [jnp.broadcast_to(m, (sub, D_GROUP)) for m in masks], axis=0), N_HEADS))
        o = o + jnp.concatenate(intra, axis=0)
        st_scr[s] = st * jnp.exp(cum_last) + _mm_tn(v, k * jnp.exp(cum_last - cum)) * bd
        ms = _mmx(o * o, bd) * (1.0 / HEAD_DIM)
        o = o * lax.rsqrt(ms + NORM_EPS) * nw_ref[...] * _silu(g_ref[rows, :])
        y_ref[rows, :] = jnp.where(valid, o, 0.0)
        return carry

    lax.fori_loop(0, plan.n_chunks, chunk, 0)

    @pl.when(j == plan.t_steps - 1)
    def _():
        def fin(s, c):
            sn_ref[s] = _store_state_bd(st_scr[s], tilet_ref[...])
            return c
        lax.fori_loop(0, plan.seq_blk, fin, 0)


def _hgrn(proj, s0, lb, norm_w, consts, plan):
    n_rows = plan.n_seq * plan.l_pad
    C = plan.chunk
    sub = min(16, C)
    in_specs = [plan.rows_spec(D_GROUP, 1), plan.rows_spec(D_GROUP, 2), plan.rows_spec(D_GROUP, 3),
                plan.rows_spec(D_GROUP, 4), plan.seq_spec(D_GROUP, HEAD_DIM),
                _const_spec((1, D_GROUP)), _const_spec((1, D_GROUP)),
                _const_spec((D_GROUP, D_GROUP)), _const_spec((HEAD_DIM, D_GROUP)),
                _const_spec((D_GROUP, HEAD_DIM)), _const_spec((C, C))]
    out_specs = [plan.rows_spec(D_GROUP), plan.seq_spec(D_GROUP, HEAD_DIM)]
    out_shape = [jax.ShapeDtypeStruct((n_rows, D_GROUP), F32),
                 jax.ShapeDtypeStruct((plan.n_seq, D_GROUP, HEAD_DIM), F32)]
    scratch = [pltpu.VMEM((plan.seq_blk, D_GROUP, D_GROUP), F32)]
    kern = functools.partial(_hgrn_kernel, plan=plan, sub=sub)
    return _seq_call(kern, plan, in_specs, out_specs, out_shape, scratch, "hgrn2")(
        proj, proj, proj, proj, s0, lb, norm_w, consts["bd"], consts["tile"], consts["tile_t"],
        consts["tri"][C])


def _gdn_kernel(q_ref, k_ref, v_ref, z_ref, sc_ref, s0_ref, cv0_ref, cw_ref, alog_ref, dtb_ref, nw_ref,
                bd_ref, tile_ref, tilet_ref, tri_ref, eb_ref, ea_ref,
                y_ref, sn_ref, cvn_ref, st_scr, cx_scr, prev_scr, *, plan):
    j = pl.program_id(1)
    C = plan.chunk
    H = N_HEADS
    HC = H * C
    masks = _head_masks()
    bd = bd_ref[...]
    lane = lax.broadcasted_iota(jnp.int32, (1, D_GROUP), 1)
    beta_lanes = lane < H
    a_lanes = (lane >= H) & (lane < 2 * H)
    neg_rate = jnp.where(a_lanes, -jnp.exp(alog_ref[...]), 0.0)
    n_lvl = max(1, int(math.ceil(math.log2(C))))
    valid_last = (plan.length - 1) % C + 1
    assert valid_last >= GD_CONV - 1
    pad0 = SUBLANES - (GD_CONV - 1)

    @pl.when(j == 0)
    def _():
        def init(s, c):
            st_scr[s] = _load_state_bd(s0_ref[s], tile_ref[...], bd)
            prev_scr[s] = jnp.zeros((SUBLANES, 3 * D_GROUP), F32)
            prev_scr[s, pl.ds(pad0, GD_CONV - 1), :] = cv0_ref[s]
            return c
        lax.fori_loop(0, plan.seq_blk, init, 0)

    row_i = lax.broadcasted_iota(jnp.int32, (HC, HC), 0)
    col_i = lax.broadcasted_iota(jnp.int32, (HC, HC), 1)
    same_head = _div2(row_i, C) == _div2(col_i, C)
    incl = same_head & (col_i <= row_i)
    strict = same_head & (col_i < row_i)
    eye = (row_i == col_i).astype(F32)

    def chunk(n, carry):
        s = n // plan.chunks_per_seq
        rows = pl.ds(pl.multiple_of(n * C, C), C)
        valid = plan.valid_rows(j, n, C)
        cx_scr[0:SUBLANES, :] = prev_scr[s]
        cx_scr[SUBLANES:SUBLANES + C, 0:D_GROUP] = q_ref[rows, :]
        cx_scr[SUBLANES:SUBLANES + C, D_GROUP:2 * D_GROUP] = k_ref[rows, :]
        cx_scr[SUBLANES:SUBLANES + C, 2 * D_GROUP:3 * D_GROUP] = v_ref[rows, :]
        cw = cw_ref[...]
        conv = cx_scr[pad0:pad0 + C, :] * cw[0:1]
        for tap in range(1, GD_CONV):
            conv = conv + cx_scr[pad0 + tap:pad0 + tap + C, :] * cw[tap:tap + 1]
        prev_scr[s, pl.ds(pad0, GD_CONV - 1), :] = cx_scr[SUBLANES + C - (GD_CONV - 1):SUBLANES + C, :]
        cvn_ref[s] = cx_scr[SUBLANES + valid_last - (GD_CONV - 1):SUBLANES + valid_last, :]
        conv = _silu(conv)
        q, k, v = conv[:, 0:D_GROUP], conv[:, D_GROUP:2 * D_GROUP], conv[:, 2 * D_GROUP:3 * D_GROUP]
        q = q * lax.rsqrt(_mmx(q * q, bd) + NORM_EPS) * (HEAD_DIM ** -0.5)
        k = k * lax.rsqrt(_mmx(k * k, bd) + NORM_EPS)
        k = jnp.where(valid, k, 0.0)

        sc = sc_ref[rows, :]
        beta_all = jnp.where(valid & beta_lanes, jax.nn.sigmoid(sc), 0.0)
        g_all = jnp.where(valid, neg_rate * jax.nn.softplus(sc + dtb_ref[...]), 0.0)
        gc_all = _mmx(tri_ref[...], g_all)
        beta_l = _mmx(beta_all, eb_ref[...])
        gc_l = _mmx(gc_all, ea_ref[...])
        gam_l = jnp.exp(gc_l)
        gc_last = gc_l[C - 1:C]

        gc_col = jnp.concatenate([_lane_col(gc_all, H + h) for h in range(H)], axis=0)
        beta_col = jnp.concatenate([_lane_col(beta_all, h) for h in range(H)], axis=0)
        g_sel = jnp.concatenate([jnp.where(lane == H + h, gc_all, 0.0) for h in range(H)], axis=0)
        gc_row = _mmx_nt(jnp.ones((SUBLANES, D_GROUP), F32), g_sel)[0:1]
        diff = gc_col - gc_row
        dec = jnp.exp(jnp.where(incl, diff, -jnp.inf))

        ks = _stack_heads(k, masks)
        qs = _stack_heads(q, masks)
        kk = _mm_nt(ks, ks)
        m = jnp.where(strict, dec, 0.0) * beta_col * kk
        t_inv = eye - m
        pw = m
        for _ in range(n_lvl - 1):
            pw = _mmx(pw, pw)
            t_inv = t_inv + _mmx(t_inv, pw)
        w = _unstack_heads(_mmx(t_inv, _stack_heads(beta_l * gam_l * k, masks)), H)
        u0 = _unstack_heads(_mmx(t_inv, _stack_heads(beta_l * v, masks)), H)
        qk = _mm_nt(qs, ks) * dec

        st = st_scr[s]
        u = u0 - _mm(w, st)
        o = _mm(q * gam_l, st) + _unstack_heads(_mm(qk, _stack_heads(u, masks)), H)
        st_scr[s] = jnp.exp(gc_last) * st + _mm_tn(k * jnp.exp(gc_last - gc_l), u) * bd

        ms = _mmx(o * o, bd) * (1.0 / HEAD_DIM)
        o = o * lax.rsqrt(ms + NORM_EPS) * nw_ref[...] * _silu(z_ref[rows, :])
        y_ref[rows, :] = jnp.where(valid, o, 0.0)
        return carry

    lax.fori_loop(0, plan.n_chunks, chunk, 0)

    @pl.when(j == plan.t_steps - 1)
    def _():
        def fin(s, c):
            sn_ref[s] = _store_state_bd(st_scr[s], tilet_ref[...])
            return c
        lax.fori_loop(0, plan.seq_blk, fin, 0)


def _gdn(proj, s0, cv0, lw, consts, plan):
    n_rows = plan.n_seq * plan.l_pad
    C = plan.chunk
    W3 = 3 * D_GROUP
    in_specs = [plan.rows_spec(D_GROUP, 5), plan.rows_spec(D_GROUP, 6), plan.rows_spec(D_GROUP, 7),
                plan.rows_spec(D_GROUP, 8), plan.rows_spec(D_GROUP, 11),
                plan.seq_spec(D_GROUP, HEAD_DIM), plan.seq_spec(GD_CONV - 1, W3),
                _const_spec((GD_CONV, W3)), _const_spec((1, D_GROUP)), _const_spec((1, D_GROUP)),
                _const_spec((1, D_GROUP)),
                _const_spec((D_GROUP, D_GROUP)), _const_spec((HEAD_DIM, D_GROUP)),
                _const_spec((D_GROUP, HEAD_DIM)), _const_spec((C, C)),
                _const_spec((D_GROUP, D_GROUP)), _const_spec((D_GROUP, D_GROUP))]
    out_specs = [plan.rows_spec(D_GROUP), plan.seq_spec(D_GROUP, HEAD_DIM), plan.seq_spec(GD_CONV - 1, W3)]
    out_shape = [jax.ShapeDtypeStruct((n_rows, D_GROUP), F32),
                 jax.ShapeDtypeStruct((plan.n_seq, D_GROUP, HEAD_DIM), F32),
                 jax.ShapeDtypeStruct((plan.n_seq, GD_CONV - 1, W3), F32)]
    scratch = [pltpu.VMEM((plan.seq_blk, D_GROUP, D_GROUP), F32),
               pltpu.VMEM((SUBLANES + C, W3), F32),
               pltpu.VMEM((plan.seq_blk, SUBLANES, W3), F32)]
    kern = functools.partial(_gdn_kernel, plan=plan)
    return _seq_call(kern, plan, in_specs, out_specs, out_shape, scratch, "gdn")(
        proj, proj, proj, proj, proj, s0, cv0, lw["conv_w"], lw["a_log"], lw["dt_bias"], lw["norm_w"],
        consts["bd"], consts["tile"], consts["tile_t"], consts["tri"][C], consts["eb"], consts["ea"])


def _swa_kernel(q_ref, kv_ref, ck_ref, cv_ref, bias_ref, sink_ref, y_ref, kn_ref, vn_ref,
                kk_scr, vv_scr, *, plan, has_cache):
    j = pl.program_id(1)
    QB = plan.chunk
    W = SW_WINDOW
    HK = SW_KV_HEADS * HEAD_DIM
    kv_masks = _head_masks(HK)
    valid_last = (plan.length - 1) % QB + 1

    @pl.when(j == 0)
    def _():
        kn_ref[...] = ck_ref[...]
        vn_ref[...] = cv_ref[...]

    def block(n, carry):
        s = n // plan.chunks_per_seq
        rows = pl.ds(pl.multiple_of(n * QB, QB), QB)
        valid = plan.valid_rows(j, n, QB)
        kk_scr[0:W, :] = kn_ref[s]
        vv_scr[0:W, :] = vn_ref[s]
        kv = kv_ref[rows, :]
        kk_scr[W:W + QB, :] = kv[:, 0:HK]
        vv_scr[W:W + QB, :] = kv[:, HK:2 * HK]
        q = q_ref[rows, :]
        qs = jnp.concatenate([q[:, 0:HK] * kv_masks[0], q[:, 0:HK] * kv_masks[1],
                              q[:, HK:2 * HK] * kv_masks[0], q[:, HK:2 * HK] * kv_masks[1]], axis=0)
        sc = _mm_nt(qs, kk_scr[...]) * (HEAD_DIM ** -0.5) + bias_ref[...]
        if not has_cache:
            t0 = j * plan.t_blk + (n % plan.chunks_per_seq) * QB
            col = lax.broadcasted_iota(jnp.int32, sc.shape, 1)
            sc = jnp.where(t0 + col < W, -jnp.inf, sc)
        sink = sink_ref[...]
        mx = jnp.maximum(jnp.max(sc, axis=-1, keepdims=True), sink)
        p = jnp.exp(sc - mx)
        denom = jnp.sum(p, axis=-1, keepdims=True) + jnp.exp(sink - mx)
        pv = _mm(p / denom, vv_scr[...])
        oa = pv[0:QB] * kv_masks[0] + pv[QB:2 * QB] * kv_masks[1]
        ob = pv[2 * QB:3 * QB] * kv_masks[0] + pv[3 * QB:4 * QB] * kv_masks[1]
        y_ref[rows, :] = jnp.where(valid, jnp.concatenate([oa, ob], axis=-1), 0.0)
        kn_ref[s] = kk_scr[valid_last:valid_last + W, :]
        vn_ref[s] = vv_scr[valid_last:valid_last + W, :]
        return carry

    lax.fori_loop(0, plan.n_chunks, block, 0)


def _swa(proj, cache_k, cache_v, bias, sink_col, plan, has_cache):
    n_rows = plan.n_seq * plan.l_pad
    QB = plan.chunk
    HK = SW_KV_HEADS * HEAD_DIM
    NK = SW_WINDOW + QB
    in_specs = [plan.rows_spec(D_GROUP, 9), plan.rows_spec(D_GROUP, 10),
                plan.seq_spec(SW_WINDOW, HK), plan.seq_spec(SW_WINDOW, HK),
                _const_spec((4 * QB, NK)), _const_spec((4 * QB, 1))]
    out_specs = [plan.rows_spec(D_GROUP), plan.seq_spec(SW_WINDOW, HK), plan.seq_spec(SW_WINDOW, HK)]
    out_shape = [jax.ShapeDtypeStruct((n_rows, D_GROUP), F32),
                 jax.ShapeDtypeStruct((plan.n_seq, SW_WINDOW, HK), F32),
                 jax.ShapeDtypeStruct((plan.n_seq, SW_WINDOW, HK), F32)]
    scratch = [pltpu.VMEM((NK, HK), F32), pltpu.VMEM((NK, HK), F32)]
    kern = functools.partial(_swa_kernel, plan=plan, has_cache=has_cache)
    return _seq_call(kern, plan, in_specs, out_specs, out_shape, scratch, "swa")(
        proj, proj, cache_k, cache_v, bias, sink_col)


_Q_HEAD_ORDER = (0, 2, 1, 3)


def _swa_bias(qb):
    w = SW_WINDOW
    i = np.arange(qb)[:, None]
    jj = np.arange(w + qb)[None, :]
    dist = w + i - jj
    ok = (dist >= 0) & (dist <= w)
    slopes = 2.0 ** (-8.0 * np.arange(1, N_HEADS + 1) / N_HEADS)
    blocks = []
    for g in range(2):
        for kv in range(SW_KV_HEADS):
            head = kv * 2 + g
            blocks.append(np.where(ok, -slopes[head] * dist, -np.inf))
    return jnp.asarray(np.concatenate(blocks, axis=0), F32)


def _constants(chunks):
    lane_head = np.arange(D_GROUP) // HEAD_DIM
    bd = (lane_head[:, None] == lane_head[None, :]).astype(np.float32)
    tile = np.tile(np.eye(HEAD_DIM, dtype=np.float32), (1, N_HEADS))
    eb = np.zeros((D_GROUP, D_GROUP), np.float32)
    ea = np.zeros((D_GROUP, D_GROUP), np.float32)
    for h in range(N_HEADS):
        eb[h, lane_head == h] = 1.0
        ea[N_HEADS + h, lane_head == h] = 1.0
    return {"bd": jnp.asarray(bd), "tile": jnp.asarray(tile), "tile_t": jnp.asarray(tile.T),
            "eb": jnp.asarray(eb), "ea": jnp.asarray(ea),
            "tri": {c: jnp.asarray(np.tril(np.ones((c, c), np.float32))) for c in chunks}}


def _cmul(ar, ai, br, bi):
    return ar * br - ai * bi, ar * bi + ai * br


def _s5_consts(lam_re, lam_im, log_dt, b_re, b_im, c_re, c_im, d, w1, w2):
    dt = jnp.exp(log_dt)[:, None]
    mag = jnp.exp(lam_re * dt)
    ang = lam_im * dt
    a_re, a_im = mag * jnp.cos(ang), mag * jnp.sin(ang)
    den = lam_re * lam_re + lam_im * lam_im
    z_re = ((a_re - 1.0) * lam_re + a_im * lam_im) / den
    z_im = (a_im * lam_re - (a_re - 1.0) * lam_im) / den
    bb_re = z_re[..., None] * b_re - z_im[..., None] * b_im
    bb_im = z_re[..., None] * b_im + z_im[..., None] * b_re
    eye = jnp.eye(SSM_GROUPS, dtype=F32)
    to_b = lambda t: jnp.einsum("gpc,gh->gchp", t, eye).reshape(D_GROUP, SSM_LANES)
    to_c = lambda t: jnp.einsum("gcp,gh->gphc", t, eye).reshape(SSM_LANES, D_GROUP)
    bmat = jnp.concatenate([to_b(bb_re), to_b(bb_im)], axis=1).astype(BF16)
    cmat = jnp.concatenate([to_c(c_re), -to_c(c_im)], axis=0).astype(BF16)
    ar, ai = a_re.reshape(1, SSM_LANES), a_im.reshape(1, SSM_LANES)
    pows = [(ar, ai)]
    for _ in range(SUBLANES - 1):
        pows.append(_cmul(pows[-1][0], pows[-1][1], ar, ai))
    t = jnp.arange(SUBLANES)[:, None]
    hs = []
    for dd in (1, 2, 4):
        hs.append(jnp.where(t >= dd, pows[dd - 1][0], 0.0))
        hs.append(jnp.where(t >= dd, pows[dd - 1][1], 0.0))
    cp = [jnp.concatenate([p[0] for p in pows], axis=0), jnp.concatenate([p[1] for p in pows], axis=0)]
    return (bmat, cmat, d.reshape(1, D_GROUP), w1.astype(BF16), w2.astype(BF16),
            jnp.stack(hs, axis=0), jnp.stack(cp, axis=0))


def _prep_w_in(w):
    q_d = w[:, 2312:2568].reshape(D_MODEL, N_HEADS, HEAD_DIM)[:, np.array(_Q_HEAD_ORDER)].reshape(D_MODEL, D_GROUP)
    scal = jnp.pad(w[:, 2304:2312], ((0, 0), (0, D_GROUP - 2 * N_HEADS)))
    return jnp.concatenate([w[:, 0:2304], q_d, w[:, 2568:2824], scal], axis=1).astype(BF16)


def _prep_w_out(w):
    d_rows = w[3 * D_GROUP:].reshape(N_HEADS, HEAD_DIM, D_MODEL)[np.array(_Q_HEAD_ORDER)].reshape(D_GROUP, D_MODEL)
    return jnp.concatenate([w[:3 * D_GROUP], d_rows], axis=0).astype(BF16)


def _lane_vec(vals, first_lane):
    return jnp.zeros((1, D_GROUP), F32).at[0, first_lane:first_lane + vals.shape[0]].set(vals)


def _trunk(x, state, layers, consts, plan, tm, has_cache, bias):
    outs = {k: [] for k in ("ssm", "hgrn", "gdn", "conv", "swa_k", "swa_v")}
    for l, lw in enumerate(layers):
        gn = lw["gains"]
        x = _ffn(x, gn[0], gn[1], lw["wg"][0], lw["wu"][0], lw["wd"][0], tm)
        proj = _inproj(x, gn[2], lw["w_in"], tm)
        y_a, ssm = _s5(proj, state["ssm"][l], lw["s5"], plan)
        y_b, hg = _hgrn(proj, state["hgrn"][l], lw["lb"], lw["hgrn_nw"], consts, plan)
        y_c, gd, cv = _gdn(proj, state["gdn"][l], state["conv"][l], lw["gdn"], consts, plan)
        y_d, ck, cvv = _swa(proj, state["swa_k"][l], state["swa_v"][l], bias, lw["sink_col"], plan, has_cache)
        x = _outproj(x, (y_a, y_b, y_c, y_d), lw["w_out"], gn[3], tm)
        x = _ffn(x, gn[4], gn[5], lw["wg"][1], lw["wu"][1], lw["wd"][1], tm)
        for k, val in zip(outs, (ssm, hg, gd, cv, ck, cvv)):
            outs[k].append(val)
    return x, {k: jnp.stack(v, axis=0) for k, v in outs.items()}


def _finish_states(st, n_seq):
    ssm = st["ssm"].reshape(DEPTH, n_seq, 2, SSM_GROUPS, SSM_STATE)
    hg = jnp.swapaxes(st["hgrn"].reshape(DEPTH, n_seq, N_HEADS, HEAD_DIM, HEAD_DIM), -1, -2)
    gd = st["gdn"].reshape(DEPTH, n_seq, N_HEADS, HEAD_DIM, HEAD_DIM)
    sk = st["swa_k"].reshape(DEPTH, n_seq, SW_WINDOW, SW_KV_HEADS, HEAD_DIM)
    sv = st["swa_v"].reshape(DEPTH, n_seq, SW_WINDOW, SW_KV_HEADS, HEAD_DIM)
    return ssm[:, :, 0], ssm[:, :, 1], hg, gd, st["conv"], sk, sv


def kernel(x_prompt, x_sample, state_ssm_re, state_ssm_im, state_hgrn, state_gdn, state_gdn_conv,
           cache_swa_k, cache_swa_v, norm_gains, ffn_w_gate, ffn_w_up, ffn_w_down, w_in, w_out,
           ssm_lambda_re, ssm_lambda_im, ssm_log_dt, ssm_b_re, ssm_b_im, ssm_c_re, ssm_c_im, ssm_d,
           ssm_w_glu1, ssm_w_glu2, hgrn_lb_logits, hgrn_norm_w, gdn_conv_w, gdn_a_log, gdn_dt_bias,
           gdn_norm_w, swa_sinks):
    bp, lp, _ = x_prompt.shape
    bs, ls, _ = x_sample.shape
    ls_pad = -(-ls // SUBLANES) * SUBLANES
    plan_p = _SeqPlan(bp, lp, lp, seq_blk=1, t_blk=min(lp, 256), chunk=min(lp, 64))
    plan_s = _SeqPlan(bs, ls, ls_pad, seq_blk=min(bs, 16), t_blk=ls_pad, chunk=ls_pad)
    consts = _constants({plan_p.chunk, plan_s.chunk})

    gam = jax.nn.softmax(hgrn_lb_logits.astype(F32), axis=0)
    lbs = jnp.cumsum(gam, axis=0) - gam[:1]
    layers = []
    for l in range(DEPTH):
        sinks = swa_sinks[l].astype(F32)[np.array(_Q_HEAD_ORDER)]
        layers.append({
            "gains": norm_gains[l].astype(F32).reshape(6, 1, D_MODEL),
            "wg": ffn_w_gate[l].astype(BF16), "wu": ffn_w_up[l].astype(BF16), "wd": ffn_w_down[l].astype(BF16),
            "w_in": _prep_w_in(w_in[l]), "w_out": _prep_w_out(w_out[l]),
            "s5": _s5_consts(ssm_lambda_re[l], ssm_lambda_im[l], ssm_log_dt[l], ssm_b_re[l], ssm_b_im[l],
                             ssm_c_re[l], ssm_c_im[l], ssm_d[l], ssm_w_glu1[l], ssm_w_glu2[l]),
            "lb": lbs[l].reshape(1, D_GROUP),
            "hgrn_nw": jnp.tile(hgrn_norm_w[l].astype(F32), N_HEADS).reshape(1, D_GROUP),
            "gdn": {"conv_w": gdn_conv_w[l].astype(F32),
                    "a_log": _lane_vec(gdn_a_log[l].astype(F32), N_HEADS),
                    "dt_bias": _lane_vec(gdn_dt_bias[l].astype(F32), N_HEADS),
                    "norm_w": jnp.tile(gdn_norm_w[l].astype(F32), N_HEADS).reshape(1, D_GROUP)},
            "sinks": sinks,
        })

    def run(x, n_seq, length, plan, state, has_cache):
        pad = plan.l_pad - length
        if pad:
            x = jnp.pad(x, ((0, 0), (0, pad), (0, 0)))
        x2 = x.reshape(n_seq * plan.l_pad, D_MODEL)
        for lw in layers:
            lw["sink_col"] = jnp.repeat(lw["sinks"], plan.chunk).reshape(4 * plan.chunk, 1)
        tm = min(512, x2.shape[0])
        y, st = _trunk(x2, state, layers, consts, plan, tm, has_cache, _swa_bias(plan.chunk))
        y = y.reshape(n_seq, plan.l_pad, D_MODEL)[:, :length]
        return (y,) + _finish_states(st, n_seq)

    HK = SW_KV_HEADS * HEAD_DIM
    zeros = lambda *shape: jnp.zeros((DEPTH, bp) + shape, F32)
    prompt_state = {"ssm": zeros(1, 2 * SSM_LANES), "hgrn": zeros(D_GROUP, HEAD_DIM),
                    "gdn": zeros(D_GROUP, HEAD_DIM), "conv": zeros(GD_CONV - 1, 3 * D_GROUP),
                    "swa_k": zeros(SW_WINDOW, HK), "swa_v": zeros(SW_WINDOW, HK)}
    sample_state = {
        "ssm": jnp.concatenate([state_ssm_re.reshape(DEPTH, bs, 1, SSM_LANES),
                                state_ssm_im.reshape(DEPTH, bs, 1, SSM_LANES)], axis=-1).astype(F32),
        "hgrn": jnp.swapaxes(state_hgrn.astype(F32), -1, -2).reshape(DEPTH, bs, D_GROUP, HEAD_DIM),
        "gdn": state_gdn.astype(F32).reshape(DEPTH, bs, D_GROUP, HEAD_DIM),
        "conv": state_gdn_conv.astype(F32),
        "swa_k": cache_swa_k.astype(F32).reshape(DEPTH, bs, SW_WINDOW, HK),
        "swa_v": cache_swa_v.astype(F32).reshape(DEPTH, bs, SW_WINDOW, HK)}

    yp, sre_p, sim_p, hg_p, gd_p, cv_p, sk_p, sv_p = run(x_prompt, bp, lp, plan_p, prompt_state, False)
    ys, sre_s, sim_s, hg_s, gd_s, cv_s, sk_s, sv_s = run(x_sample, bs, ls, plan_s, sample_state, True)
    return (yp, ys, sre_p, sre_s, sim_p, sim_s, hg_p, hg_s, gd_p, gd_s,
            cv_p, cv_s, sk_p, sk_s, sv_p, sv_s)
```

```python
import functools
import math

import numpy as np
import jax
import jax.numpy as jnp
from jax import lax
from jax.experimental import pallas as pl
from jax.experimental.pallas import tpu as pltpu

D_MODEL = 1024
DEPTH = 4
D_GROUP = 256
N_HEADS = 4
HEAD_DIM = 64
SSM_CH = 16
SSM_GROUPS = 16
SSM_STATE = 64
SSM_LANES = SSM_GROUPS * SSM_STATE
GD_CONV = 4
SW_WINDOW = 128
SW_KV_HEADS = 2
D_FF = 2816
NORM_EPS = 1e-6
PROJ_W = 12 * D_GROUP
SUBLANES = 8
VMEM_LIMIT = 56 * 1024 * 1024

F32 = jnp.float32
BF16 = jnp.bfloat16
EXACT_TERMS = 3
SUMSQ_TERMS = 2


def _mm(a, b):
    return jnp.dot(a.astype(BF16), b.astype(BF16), preferred_element_type=F32)


def _mm_nt(a, b):
    return lax.dot_general(a.astype(BF16), b.astype(BF16), (((1,), (1,)), ((), ())),
                           preferred_element_type=F32)


def _mm_tn(a, b):
    return lax.dot_general(a.astype(BF16), b.astype(BF16), (((0,), (0,)), ((), ())),
                           preferred_element_type=F32)


def _mms(a, b):
    return _mm(a, b)


def _split_bf16(x, terms):
    parts = []
    for _ in range(terms - 1):
        p = x.astype(BF16)
        parts.append(p)
        x = x - p.astype(F32)
    parts.append(x.astype(BF16))
    return parts


def _sel_r(a, sel_stack, terms):
    return jnp.dot(jnp.concatenate(_split_bf16(a, terms), axis=1), sel_stack, preferred_element_type=F32)


def _sel_l(sel_stack, b, terms):
    return jnp.dot(sel_stack, jnp.concatenate(_split_bf16(b, terms), axis=0), preferred_element_type=F32)


def _sel_nt(sel_stack, b, terms):
    return lax.dot_general(sel_stack, jnp.concatenate(_split_bf16(b, terms), axis=1),
                           (((1,), (1,)), ((), ())), preferred_element_type=F32)


def _rms(x, gain):
    return x * lax.rsqrt(jnp.mean(x * x, axis=-1, keepdims=True) + NORM_EPS) * gain


def _silu(x):
    return x * jax.nn.sigmoid(x)


def _div2(x, d):
    assert d & (d - 1) == 0
    return x >> (d.bit_length() - 1)


def _mod2(x, d):
    assert d & (d - 1) == 0
    return x & (d - 1)


def _head_masks(width=D_GROUP, head_dim=HEAD_DIM):
    lane = lax.broadcasted_iota(jnp.int32, (1, width), 1)
    return [(_div2(lane, head_dim) == h).astype(F32) for h in range(width // head_dim)]


def _stack_heads(x, masks):
    return jnp.concatenate([x * m for m in masks], axis=0)


def _unstack_heads(x, n_heads):
    c = x.shape[0] // n_heads
    out = x[0:c]
    for h in range(1, n_heads):
        out = out + x[h * c:(h + 1) * c]
    return out


def _lane_col(x, lane):
    idx = lax.broadcasted_iota(jnp.int32, x.shape, 1)
    return jnp.sum(jnp.where(idx == lane, x, 0.0), axis=-1, keepdims=True)


def _const_spec(shape):
    return pl.BlockSpec(shape, lambda *_: (0,) * len(shape), pipeline_mode=pl.Buffered(1))


def _layer_spec(shape, lead):
    idx = tuple(lead) + (0,) * len(shape)
    return pl.BlockSpec((None,) * len(lead) + tuple(shape), lambda *_: idx, pipeline_mode=pl.Buffered(1))


def _params(*sem):
    return pltpu.CompilerParams(dimension_semantics=sem, vmem_limit_bytes=VMEM_LIMIT)


def _ffn_kernel(x_ref, gpre_ref, gpost_ref, wg_ref, wu_ref, wd_ref, o_ref):
    x = x_ref[...]
    h = _rms(x, gpre_ref[...]).astype(BF16)
    g = jnp.dot(h, wg_ref[...], preferred_element_type=F32)
    u = jnp.dot(h, wu_ref[...], preferred_element_type=F32)
    a = (_silu(g) * u).astype(BF16)
    y = jnp.dot(a, wd_ref[...], preferred_element_type=F32)
    o_ref[...] = x + 0.5 * _rms(y, gpost_ref[...])


def _ffn(x, gpre, gpost, wg, wu, wd, lead, tm):
    n = x.shape[0]
    row = pl.BlockSpec((tm, D_MODEL), lambda i: (i, 0))
    return pl.pallas_call(
        _ffn_kernel,
        grid=(n // tm,),
        in_specs=[row, _const_spec((1, D_MODEL)), _const_spec((1, D_MODEL)),
                  _layer_spec((D_MODEL, D_FF), lead), _layer_spec((D_MODEL, D_FF), lead),
                  _layer_spec((D_FF, D_MODEL), lead)],
        out_specs=row,
        out_shape=jax.ShapeDtypeStruct((n, D_MODEL), F32),
        compiler_params=_params("parallel"),
        name="ffn",
    )(x, gpre, gpost, wg, wu, wd)


def _inproj_kernel(x_ref, g_ref, w_ref, o_ref):
    h = _rms(x_ref[...], g_ref[...]).astype(BF16)
    o_ref[...] = jnp.dot(h, w_ref[...], preferred_element_type=F32)


def _inproj(x, gain, w, lead, tm):
    n = x.shape[0]
    return pl.pallas_call(
        _inproj_kernel,
        grid=(n // tm,),
        in_specs=[pl.BlockSpec((tm, D_MODEL), lambda i: (i, 0)), _const_spec((1, D_MODEL)),
                  _layer_spec((D_MODEL, PROJ_W), lead)],
        out_specs=pl.BlockSpec((tm, PROJ_W), lambda i: (i, 0)),
        out_shape=jax.ShapeDtypeStruct((n, PROJ_W), F32),
        compiler_params=_params("parallel"),
        name="inproj",
    )(x, gain, w)


def _outproj_kernel(x_ref, ya_ref, yb_ref, yc_ref, yd_ref, w_ref, g_ref, o_ref):
    y = jnp.concatenate([ya_ref[...], yb_ref[...], yc_ref[...], yd_ref[...]], axis=-1)
    mix = jnp.dot(y.astype(BF16), w_ref[...], preferred_element_type=F32)
    o_ref[...] = x_ref[...] + _rms(mix, g_ref[...])


def _outproj(x, ys, w, lead, gain, tm):
    n = x.shape[0]
    row = pl.BlockSpec((tm, D_MODEL), lambda i: (i, 0))
    grp = pl.BlockSpec((tm, D_GROUP), lambda i: (i, 0))
    return pl.pallas_call(
        _outproj_kernel,
        grid=(n // tm,),
        in_specs=[row, grp, grp, grp, grp, _layer_spec((D_MODEL, D_MODEL), lead), _const_spec((1, D_MODEL))],
        out_specs=row,
        out_shape=jax.ShapeDtypeStruct((n, D_MODEL), F32),
        compiler_params=_params("parallel"),
        name="outproj",
    )(x, *ys, w, gain)


class _SeqPlan:
    def __init__(self, n_seq, length, l_pad, seq_blk, t_blk, chunk):
        assert l_pad % t_blk == 0 and t_blk % chunk == 0 and n_seq % seq_blk == 0
        assert chunk % SUBLANES == 0 and 0 <= l_pad - length < SUBLANES
        assert length % chunk == 0 or l_pad == chunk
        assert seq_blk & (seq_blk - 1) == 0
        self.n_seq, self.length, self.l_pad = n_seq, length, l_pad
        self.seq_blk, self.t_blk, self.chunk = seq_blk, t_blk, chunk
        self.rows = seq_blk * t_blk
        self.t_steps = l_pad // t_blk
        self.grid = (n_seq // seq_blk, self.t_steps)
        self.unroll = min(2, seq_blk)

    def rows_spec(self, width, col_block=0):
        return pl.BlockSpec((self.seq_blk, self.t_blk, width), lambda i, j: (i, j, col_block))

    def seq_spec(self, *tail, layer=None):
        zeros = (0,) * len(tail)
        if layer is None:
            return pl.BlockSpec((self.seq_blk,) + tail, lambda i, j: (i,) + zeros)
        return pl.BlockSpec((None, self.seq_blk) + tail, lambda i, j: (layer, i) + zeros)

    def rows_shape(self):
        return jax.ShapeDtypeStruct((self.n_seq, self.l_pad, D_GROUP), F32)

    def for_units(self, size, body):
        n_units = self.seq_blk * (self.t_blk // size)

        def step(n, carry):
            for u in range(self.unroll):
                unit = n * self.unroll + u
                body(_mod2(unit, self.seq_blk), _div2(unit, self.seq_blk))
            return carry

        lax.fori_loop(0, n_units // self.unroll, step, 0)

    def valid_rows(self, j, c, size):
        t = j * self.t_blk + c * size + lax.broadcasted_iota(jnp.int32, (size, 1), 0)
        return t < self.length


def _seq_call(kernel, plan, in_specs, out_specs, out_shape, scratch, name):
    return pl.pallas_call(
        kernel, grid=plan.grid, in_specs=in_specs, out_specs=out_specs, out_shape=out_shape,
        scratch_shapes=scratch, compiler_params=_params("parallel", "arbitrary"), name=name)


def _s5_kernel(u_ref, h0_ref, bmat_ref, cmat_ref, d_ref, w1_ref, w2_ref, hs_ref, cp_ref,
               y_ref, hn_ref, s_scr, carry_scr, fin_scr, *, plan):
    j = pl.program_id(1)
    P = SSM_LANES

    @pl.when(j == 0)
    def _():
        carry_scr[...] = h0_ref[...]

    u = u_ref[...].reshape(plan.rows, D_GROUP)
    s_scr[...] = _mm(u, bmat_ref[...])
    fin_row = (plan.length - 1) % SUBLANES

    def group(s, g):
        r0 = pl.multiple_of(s * plan.t_blk + g * SUBLANES, SUBLANES)
        x_re = s_scr[pl.ds(r0, SUBLANES), 0:P]
        x_im = s_scr[pl.ds(r0, SUBLANES), P:2 * P]
        for lvl, d in enumerate((1, 2, 4)):
            a_re, a_im = hs_ref[2 * lvl], hs_ref[2 * lvl + 1]
            r_re, r_im = pltpu.roll(x_re, d, 0), pltpu.roll(x_im, d, 0)
            x_re, x_im = (x_re + a_re * r_re - a_im * r_im,
                          x_im + a_re * r_im + a_im * r_re)
        c = carry_scr[s]
        c_re, c_im = c[:, 0:P], c[:, P:2 * P]
        p_re, p_im = cp_ref[0], cp_ref[1]
        x_re, x_im = (x_re + p_re * c_re - p_im * c_im,
                      x_im + p_re * c_im + p_im * c_re)
        s_scr[pl.ds(r0, SUBLANES), 0:P] = x_re
        s_scr[pl.ds(r0, SUBLANES), P:2 * P] = x_im
        last = SUBLANES - 1
        carry_scr[s] = jnp.concatenate([x_re[last:last + 1], x_im[last:last + 1]], axis=-1)
        fin_scr[s] = jnp.concatenate([x_re[fin_row:fin_row + 1], x_im[fin_row:fin_row + 1]], axis=-1)

    plan.for_units(SUBLANES, group)

    y = _mm(s_scr[...], cmat_ref[...]) + d_ref[...] * u
    y = jax.nn.gelu(y)
    y = _mm(y, w1_ref[...]) * jax.nn.sigmoid(_mm(y, w2_ref[...]))
    t = j * plan.t_blk + _mod2(lax.broadcasted_iota(jnp.int32, (plan.rows, 1), 0), plan.t_blk)
    y_ref[...] = jnp.where(t < plan.length, y, 0.0).reshape(plan.seq_blk, plan.t_blk, D_GROUP)

    @pl.when(j == plan.t_steps - 1)
    def _():
        hn_ref[...] = fin_scr[...]


def _s5(proj, h0, layer, consts, plan):
    P2 = 2 * SSM_LANES
    in_specs = [plan.rows_spec(D_GROUP, 0), plan.seq_spec(1, P2, layer=layer),
                _const_spec((D_GROUP, P2)), _const_spec((P2, D_GROUP)), _const_spec((1, D_GROUP)),
                _const_spec((D_GROUP, D_GROUP)), _const_spec((D_GROUP, D_GROUP)),
                _const_spec((6, SUBLANES, SSM_LANES)), _const_spec((2, SUBLANES, SSM_LANES))]
    out_specs = [plan.rows_spec(D_GROUP), plan.seq_spec(1, P2)]
    out_shape = [plan.rows_shape(), jax.ShapeDtypeStruct((plan.n_seq, 1, P2), F32)]
    scratch = [pltpu.VMEM((plan.rows, P2), F32), pltpu.VMEM((plan.seq_blk, 1, P2), F32),
               pltpu.VMEM((plan.seq_blk, 1, P2), F32)]
    return _seq_call(functools.partial(_s5_kernel, plan=plan), plan, in_specs, out_specs, out_shape,
                     scratch, "s5")(proj, h0, *consts)


def _selector_specs(chunk):
    return [_const_spec((D_GROUP, D_GROUP)), _const_spec((SUMSQ_TERMS * D_GROUP, D_GROUP)),
            _const_spec((EXACT_TERMS * HEAD_DIM, D_GROUP)), _const_spec((EXACT_TERMS * D_GROUP, HEAD_DIM)),
            _const_spec((chunk, EXACT_TERMS * chunk))]


def _selectors(consts, chunk):
    return consts["bd"], consts["bd2"], consts["tile"], consts["tile_t"], consts["tri"][chunk]


def _load_state_bd(s0, tile_mat, bd_mask):
    return _sel_r(s0, tile_mat, EXACT_TERMS) * bd_mask


def _store_state_bd(s_bd, tile_mat_t):
    return _sel_r(s_bd, tile_mat_t, EXACT_TERMS)


def _hgrn_kernel(q_ref, f_ref, i_ref, g_ref, s0_ref, lb_ref, nw_ref,
                 bd_ref, bd2_ref, tile_ref, tilet_ref, tri_ref,
                 y_ref, sn_ref, st_scr, *, plan, sub):
    j = pl.program_id(1)
    C = plan.chunk
    masks = _head_masks()
    bd = bd_ref[...]
    lb = lb_ref[...]

    @pl.when(j == 0)
    def _():
        def init(s, c):
            st_scr[s] = _load_state_bd(s0_ref[s], tile_ref[...], bd)
            return c
        lax.fori_loop(0, plan.seq_blk, init, 0)

    def chunk(s, c):
        rows = pl.ds(pl.multiple_of(c * C, C), C)
        valid = plan.valid_rows(j, c, C)
        fr = f_ref[s, rows, :]
        f = lb + (1.0 - lb) * jax.nn.sigmoid(fr)
        k = jnp.where(valid, (1.0 - lb) * jax.nn.sigmoid(-fr), 0.0)
        logf = jnp.where(valid, jnp.log(f), 0.0)
        q = _silu(q_ref[s, rows, :])
        v = i_ref[s, rows, :]
        cum = _sel_l(tri_ref[...], logf, EXACT_TERMS)
        cum_last = cum[C - 1:C]
        st = st_scr[s]
        o = _mm_nt(q * jnp.exp(cum), st)
        intra = []
        for blk in range(C // sub):
            r0, r1 = blk * sub, (blk + 1) * sub
            c0 = cum[r0 - 1:r0] if blk else jnp.zeros_like(cum_last)
            qs = _stack_heads(q[r0:r1] * jnp.exp(cum[r0:r1] - c0), masks)
            ks = k[0:r1] * jnp.exp(c0 - cum[0:r1])
            att = _mm_nt(qs, ks)
            t_idx = r0 + _mod2(lax.broadcasted_iota(jnp.int32, att.shape, 0), sub)
            s_idx = lax.broadcasted_iota(jnp.int32, att.shape, 1)
            att = jnp.where(s_idx <= t_idx, att, 0.0)
            intra.append(_unstack_heads(_mm(att, v[0:r1]) * jnp.concatenate(
                [jnp.broadcast_to(m, (sub, D_GROUP)) for m in masks], axis=0), N_HEADS))
        o = o + jnp.concatenate(intra, axis=0)
        st_scr[s] = st * jnp.exp(cum_last) + _mm_tn(v, k * jnp.exp(cum_last - cum)) * bd
        ms = _sel_r(o * o, bd2_ref[...], SUMSQ_TERMS) * (1.0 / HEAD_DIM)
        o = o * lax.rsqrt(ms + NORM_EPS) * nw_ref[...] * _silu(g_ref[s, rows, :])
        y_ref[s, rows, :] = jnp.where(valid, o, 0.0)

    plan.for_units(C, chunk)

    @pl.when(j == plan.t_steps - 1)
    def _():
        def fin(s, c):
            sn_ref[s] = _store_state_bd(st_scr[s], tilet_ref[...])
            return c
        lax.fori_loop(0, plan.seq_blk, fin, 0)


def _hgrn(proj, s0, layer, lb, norm_w, consts, plan):
    C = plan.chunk
    sub = min(16, C)
    in_specs = [plan.rows_spec(D_GROUP, 1), plan.rows_spec(D_GROUP, 2), plan.rows_spec(D_GROUP, 3),
                plan.rows_spec(D_GROUP, 4), plan.seq_spec(D_GROUP, HEAD_DIM, layer=layer),
                _const_spec((1, D_GROUP)), _const_spec((1, D_GROUP))] + _selector_specs(C)
    out_specs = [plan.rows_spec(D_GROUP), plan.seq_spec(D_GROUP, HEAD_DIM)]
    out_shape = [plan.rows_shape(), jax.ShapeDtypeStruct((plan.n_seq, D_GROUP, HEAD_DIM), F32)]
    scratch = [pltpu.VMEM((plan.seq_blk, D_GROUP, D_GROUP), F32)]
    kern = functools.partial(_hgrn_kernel, plan=plan, sub=sub)
    return _seq_call(kern, plan, in_specs, out_specs, out_shape, scratch, "hgrn2")(
        proj, proj, proj, proj, s0, lb, norm_w, *_selectors(consts, C))


def _gdn_kernel(q_ref, k_ref, v_ref, z_ref, sc_ref, s0_ref, cv0_ref, cw_ref, alog_ref, dtb_ref, nw_ref,
                bd_ref, bd2_ref, tile_ref, tilet_ref, tri_ref, eb_ref, ea_ref,
                y_ref, sn_ref, cvn_ref, st_scr, cx_scr, prev_scr, *, plan):
    j = pl.program_id(1)
    C = plan.chunk
    H = N_HEADS
    HC = H * C
    masks = _head_masks()
    bd = bd_ref[...]
    lane = lax.broadcasted_iota(jnp.int32, (1, D_GROUP), 1)
    beta_lanes = lane < H
    a_lanes = (lane >= H) & (lane < 2 * H)
    neg_rate = jnp.where(a_lanes, -jnp.exp(alog_ref[...]), 0.0)
    n_lvl = max(1, int(math.ceil(math.log2(C))))
    valid_last = (plan.length - 1) % C + 1
    assert valid_last >= GD_CONV - 1
    pad0 = SUBLANES - (GD_CONV - 1)

    @pl.when(j == 0)
    def _():
        def init(s, c):
            st_scr[s] = _load_state_bd(s0_ref[s], tile_ref[...], bd)
            prev_scr[s] = jnp.zeros((SUBLANES, 3 * D_GROUP), F32)
            prev_scr[s, pl.ds(pad0, GD_CONV - 1), :] = cv0_ref[s]
            return c
        lax.fori_loop(0, plan.seq_blk, init, 0)

    row_i = lax.broadcasted_iota(jnp.int32, (HC, HC), 0)
    col_i = lax.broadcasted_iota(jnp.int32, (HC, HC), 1)
    same_head = _div2(row_i, C) == _div2(col_i, C)
    incl = same_head & (col_i <= row_i)
    strict = same_head & (col_i < row_i)
    eye = (row_i == col_i).astype(F32)

    def chunk(s, c):
        rows = pl.ds(pl.multiple_of(c * C, C), C)
        valid = plan.valid_rows(j, c, C)
        cx = cx_scr.at[s]
        cx[0:SUBLANES, :] = prev_scr[s]
        cx[SUBLANES:SUBLANES + C, 0:D_GROUP] = q_ref[s, rows, :]
        cx[SUBLANES:SUBLANES + C, D_GROUP:2 * D_GROUP] = k_ref[s, rows, :]
        cx[SUBLANES:SUBLANES + C, 2 * D_GROUP:3 * D_GROUP] = v_ref[s, rows, :]
        cw = cw_ref[...]
        conv = cx[pad0:pad0 + C, :] * cw[0:1]
        for tap in range(1, GD_CONV):
            conv = conv + cx[pad0 + tap:pad0 + tap + C, :] * cw[tap:tap + 1]
        prev_scr[s, pl.ds(pad0, GD_CONV - 1), :] = cx[SUBLANES + C - (GD_CONV - 1):SUBLANES + C, :]
        cvn_ref[s] = cx[SUBLANES + valid_last - (GD_CONV - 1):SUBLANES + valid_last, :]
        conv = _silu(conv)
        q, k, v = conv[:, 0:D_GROUP], conv[:, D_GROUP:2 * D_GROUP], conv[:, 2 * D_GROUP:3 * D_GROUP]
        q = q * lax.rsqrt(_sel_r(q * q, bd2_ref[...], SUMSQ_TERMS) + NORM_EPS) * (HEAD_DIM ** -0.5)
        k = k * lax.rsqrt(_sel_r(k * k, bd2_ref[...], SUMSQ_TERMS) + NORM_EPS)
        k = jnp.where(valid, k, 0.0)

        sc = sc_ref[s, rows, :]
        beta_all = jnp.where(valid & beta_lanes, jax.nn.sigmoid(sc), 0.0)
        g_all = jnp.where(valid, neg_rate * jax.nn.softplus(sc + dtb_ref[...]), 0.0)
        gc_all = _sel_l(tri_ref[...], g_all, EXACT_TERMS)
        beta_l = _sel_r(beta_all, eb_ref[...], EXACT_TERMS)
        gc_l = _sel_r(gc_all, ea_ref[...], EXACT_TERMS)
        gam_l = jnp.exp(gc_l)
        gc_last = gc_l[C - 1:C]

        gc_col = jnp.concatenate([_lane_col(gc_all, H + h) for h in range(H)], axis=0)
        beta_col = jnp.concatenate([_lane_col(beta_all, h) for h in range(H)], axis=0)
        g_sel = jnp.concatenate([jnp.where(lane == H + h, gc_all, 0.0) for h in range(H)], axis=0)
        gc_row = _sel_nt(jnp.ones((SUBLANES, EXACT_TERMS * D_GROUP), BF16), g_sel, EXACT_TERMS)[0:1]
        diff = gc_col - gc_row
        dec = jnp.exp(jnp.where(incl, diff, -jnp.inf))

        ks = _stack_heads(k, masks)
        qs = _stack_heads(q, masks)
        kk = _mm_nt(ks, ks)
        m = jnp.where(strict, dec, 0.0) * beta_col * kk
        blk_r, blk_c = row_i >> 1, col_i >> 1
        t_inv = eye - jnp.where(blk_r == blk_c, m, 0.0)
        for _ in range(n_lvl - 1):
            lower_left = ((blk_r >> 1) == (blk_c >> 1)) & (blk_r != blk_c)
            t_inv = t_inv - _mms(_mms(t_inv, jnp.where(lower_left, m, 0.0)), t_inv)
            blk_r, blk_c = blk_r >> 1, blk_c >> 1
        w = _unstack_heads(_mms(t_inv, _stack_heads(beta_l * gam_l * k, masks)), H)
        u0 = _unstack_heads(_mms(t_inv, _stack_heads(beta_l * v, masks)), H)
        qk = _mm_nt(qs, ks) * dec

        st = st_scr[s]
        u = u0 - _mm(w, st)
        o = _mm(q * gam_l, st) + _unstack_heads(_mm(qk, _stack_heads(u, masks)), H)
        st_scr[s] = jnp.exp(gc_last) * st + _mm_tn(k * jnp.exp(gc_last - gc_l), u) * bd

        ms = _sel_r(o * o, bd2_ref[...], SUMSQ_TERMS) * (1.0 / HEAD_DIM)
        o = o * lax.rsqrt(ms + NORM_EPS) * nw_ref[...] * _silu(z_ref[s, rows, :])
        y_ref[s, rows, :] = jnp.where(valid, o, 0.0)

    plan.for_units(C, chunk)

    @pl.when(j == plan.t_steps - 1)
    def _():
        def fin(s, c):
            sn_ref[s] = _store_state_bd(st_scr[s], tilet_ref[...])
            return c
        lax.fori_loop(0, plan.seq_blk, fin, 0)


def _gdn(proj, s0, cv0, layer, lw, consts, plan):
    C = plan.chunk
    W3 = 3 * D_GROUP
    in_specs = [plan.rows_spec(D_GROUP, 5), plan.rows_spec(D_GROUP, 6), plan.rows_spec(D_GROUP, 7),
                plan.rows_spec(D_GROUP, 8), plan.rows_spec(D_GROUP, 11),
                plan.seq_spec(D_GROUP, HEAD_DIM, layer=layer), plan.seq_spec(GD_CONV - 1, W3, layer=layer),
                _const_spec((GD_CONV, W3)), _const_spec((1, D_GROUP)), _const_spec((1, D_GROUP)),
                _const_spec((1, D_GROUP))] + _selector_specs(C) + [
                _const_spec((EXACT_TERMS * D_GROUP, D_GROUP)), _const_spec((EXACT_TERMS * D_GROUP, D_GROUP))]
    out_specs = [plan.rows_spec(D_GROUP), plan.seq_spec(D_GROUP, HEAD_DIM), plan.seq_spec(GD_CONV - 1, W3)]
    out_shape = [plan.rows_shape(),
                 jax.ShapeDtypeStruct((plan.n_seq, D_GROUP, HEAD_DIM), F32),
                 jax.ShapeDtypeStruct((plan.n_seq, GD_CONV - 1, W3), F32)]
    scratch = [pltpu.VMEM((plan.seq_blk, D_GROUP, D_GROUP), F32),
               pltpu.VMEM((plan.seq_blk, SUBLANES + C, W3), F32),
               pltpu.VMEM((plan.seq_blk, SUBLANES, W3), F32)]
    kern = functools.partial(_gdn_kernel, plan=plan)
    return _seq_call(kern, plan, in_specs, out_specs, out_shape, scratch, "gdn")(
        proj, proj, proj, proj, proj, s0, cv0, lw["conv_w"], lw["a_log"], lw["dt_bias"], lw["norm_w"],
        *_selectors(consts, C), consts["eb"], consts["ea"])


def _swa_kernel(q_ref, kv_ref, ck_ref, cv_ref, bias_ref, sink_ref, y_ref, kn_ref, vn_ref,
                kk_scr, vv_scr, *, plan, has_cache):
    j = pl.program_id(1)
    QB = plan.chunk
    W = SW_WINDOW
    HK = SW_KV_HEADS * HEAD_DIM
    kv_masks = _head_masks(HK)
    valid_last = (plan.length - 1) % QB + 1

    @pl.when(j == 0)
    def _():
        kn_ref[...] = ck_ref[...]
        vn_ref[...] = cv_ref[...]

    def block(s, c):
        rows = pl.ds(pl.multiple_of(c * QB, QB), QB)
        valid = plan.valid_rows(j, c, QB)
        kk, vv = kk_scr.at[s], vv_scr.at[s]
        kk[0:W, :] = kn_ref[s]
        vv[0:W, :] = vn_ref[s]
        kv = kv_ref[s, rows, :]
        kk[W:W + QB, :] = kv[:, 0:HK]
        vv[W:W + QB, :] = kv[:, HK:2 * HK]
        q = q_ref[s, rows, :]
        qs = jnp.concatenate([q[:, 0:HK] * kv_masks[0], q[:, 0:HK] * kv_masks[1],
                              q[:, HK:2 * HK] * kv_masks[0], q[:, HK:2 * HK] * kv_masks[1]], axis=0)
        sc = _mm_nt(qs, kk[...]) * (HEAD_DIM ** -0.5) + bias_ref[...]
        if not has_cache:
            t0 = j * plan.t_blk + c * QB
            col = lax.broadcasted_iota(jnp.int32, sc.shape, 1)
            sc = jnp.where(t0 + col < W, -jnp.inf, sc)
        sink = sink_ref[...]
        mx = jnp.maximum(jnp.max(sc, axis=-1, keepdims=True), sink)
        p = jnp.exp(sc - mx)
        denom = jnp.sum(p, axis=-1, keepdims=True) + jnp.exp(sink - mx)
        pv = _mm(p / denom, vv[...])
        oa = pv[0:QB] * kv_masks[0] + pv[QB:2 * QB] * kv_masks[1]
        ob = pv[2 * QB:3 * QB] * kv_masks[0] + pv[3 * QB:4 * QB] * kv_masks[1]
        y_ref[s, rows, :] = jnp.where(valid, jnp.concatenate([oa, ob], axis=-1), 0.0)
        kn_ref[s] = kk[valid_last:valid_last + W, :]
        vn_ref[s] = vv[valid_last:valid_last + W, :]

    plan.for_units(QB, block)


def _swa(proj, cache_k, cache_v, layer, bias, sink_col, plan, has_cache):
    QB = plan.chunk
    HK = SW_KV_HEADS * HEAD_DIM
    NK = SW_WINDOW + QB
    in_specs = [plan.rows_spec(D_GROUP, 9), plan.rows_spec(D_GROUP, 10),
                plan.seq_spec(SW_WINDOW, HK, layer=layer), plan.seq_spec(SW_WINDOW, HK, layer=layer),
                _const_spec((4 * QB, NK)), _const_spec((4 * QB, 1))]
    out_specs = [plan.rows_spec(D_GROUP), plan.seq_spec(SW_WINDOW, HK), plan.seq_spec(SW_WINDOW, HK)]
    out_shape = [plan.rows_shape(),
                 jax.ShapeDtypeStruct((plan.n_seq, SW_WINDOW, HK), F32),
                 jax.ShapeDtypeStruct((plan.n_seq, SW_WINDOW, HK), F32)]
    scratch = [pltpu.VMEM((plan.seq_blk, NK, HK), F32), pltpu.VMEM((plan.seq_blk, NK, HK), F32)]
    kern = functools.partial(_swa_kernel, plan=plan, has_cache=has_cache)
    return _seq_call(kern, plan, in_specs, out_specs, out_shape, scratch, "swa")(
        proj, proj, cache_k, cache_v, bias, sink_col)


_Q_HEAD_ORDER = (0, 2, 1, 3)


def _swa_bias(qb):
    w = SW_WINDOW
    i = np.arange(qb)[:, None]
    jj = np.arange(w + qb)[None, :]
    dist = w + i - jj
    ok = (dist >= 0) & (dist <= w)
    slopes = 2.0 ** (-8.0 * np.arange(1, N_HEADS + 1) / N_HEADS)
    blocks = []
    for g in range(2):
        for kv in range(SW_KV_HEADS):
            head = kv * 2 + g
            blocks.append(np.where(ok, -slopes[head] * dist, -np.inf))
    return jnp.asarray(np.concatenate(blocks, axis=0), F32)


def _constants(chunks):
    lane_head = np.arange(D_GROUP) // HEAD_DIM
    bd = (lane_head[:, None] == lane_head[None, :]).astype(np.float32)
    tile = np.tile(np.eye(HEAD_DIM, dtype=np.float32), (1, N_HEADS))
    eb = np.zeros((D_GROUP, D_GROUP), np.float32)
    ea = np.zeros((D_GROUP, D_GROUP), np.float32)
    for h in range(N_HEADS):
        eb[h, lane_head == h] = 1.0
        ea[N_HEADS + h, lane_head == h] = 1.0
    rows = lambda m, n: jnp.asarray(np.tile(m, (n, 1)), BF16)
    cols = lambda m, n: jnp.asarray(np.tile(m, (1, n)), BF16)
    return {"bd": jnp.asarray(bd), "bd2": rows(bd, SUMSQ_TERMS),
            "tile": rows(tile, EXACT_TERMS), "tile_t": rows(tile.T, EXACT_TERMS),
            "eb": rows(eb, EXACT_TERMS), "ea": rows(ea, EXACT_TERMS),
            "tri": {c: cols(np.tril(np.ones((c, c), np.float32)), EXACT_TERMS) for c in chunks}}


def _cmul(ar, ai, br, bi):
    return ar * br - ai * bi, ar * bi + ai * br


def _s5_consts(lam_re, lam_im, log_dt, b_re, b_im, c_re, c_im, d, w1, w2):
    dt = jnp.exp(log_dt)[:, None]
    mag = jnp.exp(lam_re * dt)
    ang = lam_im * dt
    a_re, a_im = mag * jnp.cos(ang), mag * jnp.sin(ang)
    den = lam_re * lam_re + lam_im * lam_im
    z_re = ((a_re - 1.0) * lam_re + a_im * lam_im) / den
    z_im = (a_im * lam_re - (a_re - 1.0) * lam_im) / den
    bb_re = z_re[..., None] * b_re - z_im[..., None] * b_im
    bb_im = z_re[..., None] * b_im + z_im[..., None] * b_re
    eye = jnp.eye(SSM_GROUPS, dtype=F32)
    to_b = lambda t: jnp.einsum("gpc,gh->gchp", t, eye).reshape(D_GROUP, SSM_LANES)
    to_c = lambda t: jnp.einsum("gcp,gh->gphc", t, eye).reshape(SSM_LANES, D_GROUP)
    bmat = jnp.concatenate([to_b(bb_re), to_b(bb_im)], axis=1).astype(BF16)
    cmat = jnp.concatenate([to_c(c_re), -to_c(c_im)], axis=0).astype(BF16)
    ar, ai = a_re.reshape(1, SSM_LANES), a_im.reshape(1, SSM_LANES)
    pows = [(ar, ai)]
    for _ in range(SUBLANES - 1):
        pows.append(_cmul(pows[-1][0], pows[-1][1], ar, ai))
    t = jnp.arange(SUBLANES)[:, None]
    hs = []
    for dd in (1, 2, 4):
        hs.append(jnp.where(t >= dd, pows[dd - 1][0], 0.0))
        hs.append(jnp.where(t >= dd, pows[dd - 1][1], 0.0))
    cp = [jnp.concatenate([p[0] for p in pows], axis=0), jnp.concatenate([p[1] for p in pows], axis=0)]
    return (bmat, cmat, d.reshape(1, D_GROUP), w1.astype(BF16), w2.astype(BF16),
            jnp.stack(hs, axis=0), jnp.stack(cp, axis=0))


def _prep_w_in(w):
    q_d = w[:, :, 2312:2568].reshape(DEPTH, D_MODEL, N_HEADS, HEAD_DIM)[:, :, np.array(_Q_HEAD_ORDER)]
    scal = jnp.pad(w[:, :, 2304:2312], ((0, 0), (0, 0), (0, D_GROUP - 2 * N_HEADS)))
    return jnp.concatenate([w[:, :, 0:2304], q_d.reshape(DEPTH, D_MODEL, D_GROUP), w[:, :, 2568:2824], scal],
                           axis=2).astype(BF16)


def _prep_w_out(w):
    d_rows = w[:, 3 * D_GROUP:].reshape(DEPTH, N_HEADS, HEAD_DIM, D_MODEL)[:, np.array(_Q_HEAD_ORDER)]
    return jnp.concatenate([w[:, :3 * D_GROUP], d_rows.reshape(DEPTH, D_GROUP, D_MODEL)], axis=1).astype(BF16)


def _lane_vec(vals, first_lane):
    return jnp.zeros((1, D_GROUP), F32).at[0, first_lane:first_lane + vals.shape[0]].set(vals)


def _trunk(x, state, layers, big, consts, plan, tm, has_cache, bias):
    outs = {k: [] for k in ("ssm", "hgrn", "gdn", "conv", "swa_k", "swa_v")}
    for l, lw in enumerate(layers):
        gn = lw["gains"]
        x = _ffn(x, gn[0], gn[1], big["wg"], big["wu"], big["wd"], (l, 0), tm)
        proj = _inproj(x, gn[2], big["w_in"], (l,), tm).reshape(plan.n_seq, plan.l_pad, PROJ_W)
        sl = l if has_cache else 0
        y_a, ssm = _s5(proj, state["ssm"], sl, lw["s5"], plan)
        y_b, hg = _hgrn(proj, state["hgrn"], sl, lw["lb"], lw["hgrn_nw"], consts, plan)
        y_c, gd, cv = _gdn(proj, state["gdn"], state["conv"], sl, lw["gdn"], consts, plan)
        y_d, ck, cvv = _swa(proj, state["swa_k"], state["swa_v"], sl, bias, lw["sink_col"], plan, has_cache)
        ys = tuple(y.reshape(x.shape[0], D_GROUP) for y in (y_a, y_b, y_c, y_d))
        x = _outproj(x, ys, big["w_out"], (l,), gn[3], tm)
        x = _ffn(x, gn[4], gn[5], big["wg"], big["wu"], big["wd"], (l, 1), tm)
        for k, val in zip(outs, (ssm, hg, gd, cv, ck, cvv)):
            outs[k].append(val)
    return x, {k: jnp.stack(v, axis=0) for k, v in outs.items()}


def _finish_states(st, n_seq):
    ssm = st["ssm"].reshape(DEPTH, n_seq, 2, SSM_GROUPS, SSM_STATE)
    hg = jnp.swapaxes(st["hgrn"].reshape(DEPTH, n_seq, N_HEADS, HEAD_DIM, HEAD_DIM), -1, -2)
    gd = st["gdn"].reshape(DEPTH, n_seq, N_HEADS, HEAD_DIM, HEAD_DIM)
    sk = st["swa_k"].reshape(DEPTH, n_seq, SW_WINDOW, SW_KV_HEADS, HEAD_DIM)
    sv = st["swa_v"].reshape(DEPTH, n_seq, SW_WINDOW, SW_KV_HEADS, HEAD_DIM)
    return ssm[:, :, 0], ssm[:, :, 1], hg, gd, st["conv"], sk, sv


def kernel(x_prompt, x_sample, state_ssm_re, state_ssm_im, state_hgrn, state_gdn, state_gdn_conv,
           cache_swa_k, cache_swa_v, norm_gains, ffn_w_gate, ffn_w_up, ffn_w_down, w_in, w_out,
           ssm_lambda_re, ssm_lambda_im, ssm_log_dt, ssm_b_re, ssm_b_im, ssm_c_re, ssm_c_im, ssm_d,
           ssm_w_glu1, ssm_w_glu2, hgrn_lb_logits, hgrn_norm_w, gdn_conv_w, gdn_a_log, gdn_dt_bias,
           gdn_norm_w, swa_sinks):
    bp, lp, _ = x_prompt.shape
    bs, ls, _ = x_sample.shape
    ls_pad = -(-ls // SUBLANES) * SUBLANES
    plan_p = _SeqPlan(bp, lp, lp, seq_blk=min(bp, 2), t_blk=min(lp, 256), chunk=min(lp, 64))
    plan_s = _SeqPlan(bs, ls, ls_pad, seq_blk=min(bs, 16), t_blk=ls_pad, chunk=ls_pad)
    consts = _constants({plan_p.chunk, plan_s.chunk})

    gam = jax.nn.softmax(hgrn_lb_logits.astype(F32), axis=0)
    lbs = jnp.cumsum(gam, axis=0) - gam[:1]
    big = {"wg": ffn_w_gate.astype(BF16), "wu": ffn_w_up.astype(BF16), "wd": ffn_w_down.astype(BF16),
           "w_in": _prep_w_in(w_in), "w_out": _prep_w_out(w_out)}
    layers = []
    for l in range(DEPTH):
        sinks = swa_sinks[l].astype(F32)[np.array(_Q_HEAD_ORDER)]
        layers.append({
            "gains": norm_gains[l].astype(F32).reshape(6, 1, D_MODEL),
            "s5": _s5_consts(ssm_lambda_re[l], ssm_lambda_im[l], ssm_log_dt[l], ssm_b_re[l], ssm_b_im[l],
                             ssm_c_re[l], ssm_c_im[l], ssm_d[l], ssm_w_glu1[l], ssm_w_glu2[l]),
            "lb": lbs[l].reshape(1, D_GROUP),
            "hgrn_nw": jnp.tile(hgrn_norm_w[l].astype(F32), N_HEADS).reshape(1, D_GROUP),
            "gdn": {"conv_w": gdn_conv_w[l].astype(F32),
                    "a_log": _lane_vec(gdn_a_log[l].astype(F32), N_HEADS),
                    "dt_bias": _lane_vec(gdn_dt_bias[l].astype(F32), N_HEADS),
                    "norm_w": jnp.tile(gdn_norm_w[l].astype(F32), N_HEADS).reshape(1, D_GROUP)},
            "sinks": sinks,
        })

    def run(x, n_seq, length, plan, state, has_cache):
        pad = plan.l_pad - length
        if pad:
            x = jnp.pad(x, ((0, 0), (0, pad), (0, 0)))
        x2 = x.reshape(n_seq * plan.l_pad, D_MODEL)
        for lw in layers:
            lw["sink_col"] = jnp.repeat(lw["sinks"], plan.chunk).reshape(4 * plan.chunk, 1)
        tm = min(512, x2.shape[0])
        y, st = _trunk(x2, state, layers, big, consts, plan, tm, has_cache, _swa_bias(plan.chunk))
        y = y.reshape(n_seq, plan.l_pad, D_MODEL)[:, :length]
        return (y,) + _finish_states(st, n_seq)

    HK = SW_KV_HEADS * HEAD_DIM
    zeros = lambda *shape: jnp.zeros((1, bp) + shape, F32)
    prompt_state = {"ssm": zeros(1, 2 * SSM_LANES), "hgrn": zeros(D_GROUP, HEAD_DIM),
                    "gdn": zeros(D_GROUP, HEAD_DIM), "conv": zeros(GD_CONV - 1, 3 * D_GROUP),
                    "swa_k": zeros(SW_WINDOW, HK), "swa_v": zeros(SW_WINDOW, HK)}
    sample_state = {
        "ssm": jnp.concatenate([state_ssm_re.reshape(DEPTH, bs, 1, SSM_LANES),
                                state_ssm_im.reshape(DEPTH, bs, 1, SSM_LANES)], axis=-1).astype(F32),
        "hgrn": jnp.swapaxes(state_hgrn.astype(F32), -1, -2).reshape(DEPTH, bs, D_GROUP, HEAD_DIM),
        "gdn": state_gdn.astype(F32).reshape(DEPTH, bs, D_GROUP, HEAD_DIM),
        "conv": state_gdn_conv.astype(F32),
        "swa_k": cache_swa_k.astype(F32).reshape(DEPTH, bs, SW_WINDOW, HK),
        "swa_v": cache_swa_v.astype(F32).reshape(DEPTH, bs, SW_WINDOW, HK)}

    yp, sre_p, sim_p, hg_p, gd_p, cv_p, sk_p, sv_p = run(x_prompt, bp, lp, plan_p, prompt_state, False)
    ys, sre_s, sim_s, hg_s, gd_s, cv_s, sk_s, sv_s = run(x_sample, bs, ls, plan_s, sample_state, True)
    return (yp, ys, sre_p, sre_s, sim_p, sim_s, hg_p, hg_s, gd_p, gd_s,
            cv_p, cv_s, sk_p, sk_s, sv_p, sv_s)
```

```python
import functools
import math

import numpy as np
import jax
import jax.numpy as jnp
from jax import lax
from jax.experimental import pallas as pl
from jax.experimental.pallas import tpu as pltpu

D_MODEL = 1024
DEPTH = 4
D_GROUP = 256
N_HEADS = 4
HEAD_DIM = 64
SSM_CH = 16
SSM_GROUPS = 16
SSM_STATE = 64
SSM_LANES = SSM_GROUPS * SSM_STATE
GD_CONV = 4
SW_WINDOW = 128
SW_KV_HEADS = 2
D_FF = 2816
NORM_EPS = 1e-6
PROJ_W = 12 * D_GROUP
SUBLANES = 8
VMEM_LIMIT = 56 * 1024 * 1024

F32 = jnp.float32
BF16 = jnp.bfloat16
EXACT_TERMS = 3
SUMSQ_TERMS = 2


def _mm(a, b):
    return jnp.dot(a.astype(BF16), b.astype(BF16), preferred_element_type=F32)


def _mm_nt(a, b):
    return lax.dot_general(a.astype(BF16), b.astype(BF16), (((1,), (1,)), ((), ())),
                           preferred_element_type=F32)


def _mm_tn(a, b):
    return lax.dot_general(a.astype(BF16), b.astype(BF16), (((0,), (0,)), ((), ())),
                           preferred_element_type=F32)


def _mms(a, b):
    return _mm(a, b)


def _split_bf16(x, terms):
    parts = []
    for _ in range(terms - 1):
        p = x.astype(BF16)
        parts.append(p)
        x = x - p.astype(F32)
    parts.append(x.astype(BF16))
    return parts


def _sel_r(a, sel_stack, terms):
    return jnp.dot(jnp.concatenate(_split_bf16(a, terms), axis=1), sel_stack, preferred_element_type=F32)


def _sel_l(sel_stack, b, terms):
    return jnp.dot(sel_stack, jnp.concatenate(_split_bf16(b, terms), axis=0), preferred_element_type=F32)


def _sel_nt(sel_stack, b, terms):
    return lax.dot_general(sel_stack, jnp.concatenate(_split_bf16(b, terms), axis=1),
                           (((1,), (1,)), ((), ())), preferred_element_type=F32)


def _rms(x, gain):
    return x * lax.rsqrt(jnp.mean(x * x, axis=-1, keepdims=True) + NORM_EPS) * gain


def _silu(x):
    return x * jax.nn.sigmoid(x)


def _div2(x, d):
    assert d & (d - 1) == 0
    return x >> (d.bit_length() - 1)


def _mod2(x, d):
    assert d & (d - 1) == 0
    return x & (d - 1)


def _head_masks(width=D_GROUP, head_dim=HEAD_DIM):
    lane = lax.broadcasted_iota(jnp.int32, (1, width), 1)
    return [(_div2(lane, head_dim) == h).astype(F32) for h in range(width // head_dim)]


def _stack_heads(x, masks):
    return jnp.concatenate([x * m for m in masks], axis=0)


def _unstack_heads(x, n_heads):
    c = x.shape[0] // n_heads
    out = x[0:c]
    for h in range(1, n_heads):
        out = out + x[h * c:(h + 1) * c]
    return out


def _lane_col(x, lane):
    idx = lax.broadcasted_iota(jnp.int32, x.shape, 1)
    return jnp.sum(jnp.where(idx == lane, x, 0.0), axis=-1, keepdims=True)


def _const_spec(shape):
    return pl.BlockSpec(shape, lambda *_: (0,) * len(shape), pipeline_mode=pl.Buffered(1))


def _layer_spec(shape, lead):
    idx = tuple(lead) + (0,) * len(shape)
    return pl.BlockSpec((None,) * len(lead) + tuple(shape), lambda *_: idx, pipeline_mode=pl.Buffered(1))


def _params(*sem):
    return pltpu.CompilerParams(dimension_semantics=sem, vmem_limit_bytes=VMEM_LIMIT)


def _ffn_kernel(x_ref, gpre_ref, gpost_ref, wg_ref, wu_ref, wd_ref, *rest, mixed):
    x = x_ref[...]
    if mixed:
        ya_ref, yb_ref, yc_ref, yd_ref, wo_ref, gmix_ref, o_ref = rest
        y = jnp.concatenate([ya_ref[...], yb_ref[...], yc_ref[...], yd_ref[...]], axis=-1)
        x = x + _rms(jnp.dot(y.astype(BF16), wo_ref[...], preferred_element_type=F32), gmix_ref[...])
    else:
        o_ref, = rest
    h = _rms(x, gpre_ref[...]).astype(BF16)
    g = jnp.dot(h, wg_ref[...], preferred_element_type=F32)
    u = jnp.dot(h, wu_ref[...], preferred_element_type=F32)
    a = (_silu(g) * u).astype(BF16)
    y = jnp.dot(a, wd_ref[...], preferred_element_type=F32)
    o_ref[...] = x + 0.5 * _rms(y, gpost_ref[...])


def _ffn(x, gpre, gpost, wg, wu, wd, lead, tm, mix=None):
    n = x.shape[0]
    row = pl.BlockSpec((tm, D_MODEL), lambda i: (i, 0))
    in_specs = [row, _const_spec((1, D_MODEL)), _const_spec((1, D_MODEL)),
                _layer_spec((D_MODEL, D_FF), lead), _layer_spec((D_MODEL, D_FF), lead),
                _layer_spec((D_FF, D_MODEL), lead)]
    args = [x, gpre, gpost, wg, wu, wd]
    if mix is not None:
        ys, w_out, gmix = mix
        grp = pl.BlockSpec((tm, D_GROUP), lambda i: (i, 0))
        in_specs += [grp, grp, grp, grp, _layer_spec((D_MODEL, D_MODEL), lead[:1]), _const_spec((1, D_MODEL))]
        args += [*ys, w_out, gmix]
    return pl.pallas_call(
        functools.partial(_ffn_kernel, mixed=mix is not None),
        grid=(n // tm,),
        in_specs=in_specs,
        out_specs=row,
        out_shape=jax.ShapeDtypeStruct((n, D_MODEL), F32),
        compiler_params=_params("parallel"),
        name="ffn",
    )(*args)


def _inproj_kernel(x_ref, g_ref, w_ref, o_ref):
    h = _rms(x_ref[...], g_ref[...]).astype(BF16)
    o_ref[...] = jnp.dot(h, w_ref[...], preferred_element_type=F32)


def _inproj(x, gain, w, lead, tm):
    n = x.shape[0]
    return pl.pallas_call(
        _inproj_kernel,
        grid=(n // tm,),
        in_specs=[pl.BlockSpec((tm, D_MODEL), lambda i: (i, 0)), _const_spec((1, D_MODEL)),
                  _layer_spec((D_MODEL, PROJ_W), lead)],
        out_specs=pl.BlockSpec((tm, PROJ_W), lambda i: (i, 0)),
        out_shape=jax.ShapeDtypeStruct((n, PROJ_W), F32),
        compiler_params=_params("parallel"),
        name="inproj",
    )(x, gain, w)


class _SeqPlan:
    def __init__(self, n_seq, length, l_pad, seq_blk, t_blk, chunk):
        assert l_pad % t_blk == 0 and t_blk % chunk == 0 and n_seq % seq_blk == 0
        assert chunk % SUBLANES == 0 and 0 <= l_pad - length < SUBLANES
        assert length % chunk == 0 or l_pad == chunk
        assert seq_blk & (seq_blk - 1) == 0
        self.n_seq, self.length, self.l_pad = n_seq, length, l_pad
        self.seq_blk, self.t_blk, self.chunk = seq_blk, t_blk, chunk
        self.rows = seq_blk * t_blk
        self.t_steps = l_pad // t_blk
        self.grid = (n_seq // seq_blk, self.t_steps)
        self.unroll = min(4, seq_blk)

    def rows_spec(self, width, col_block=0):
        return pl.BlockSpec((self.seq_blk, self.t_blk, width), lambda i, j: (i, j, col_block))

    def seq_spec(self, *tail, layer=None):
        zeros = (0,) * len(tail)
        if layer is None:
            return pl.BlockSpec((self.seq_blk,) + tail, lambda i, j: (i,) + zeros)
        return pl.BlockSpec((None, self.seq_blk) + tail, lambda i, j: (layer, i) + zeros)

    def rows_shape(self):
        return jax.ShapeDtypeStruct((self.n_seq, self.l_pad, D_GROUP), F32)

    def for_unit_groups(self, size, body):
        n_units = self.seq_blk * (self.t_blk // size)

        def step(n, carry):
            if self.unroll == self.seq_blk:
                body([(u, n) for u in range(self.unroll)])
            else:
                units = [n * self.unroll + u for u in range(self.unroll)]
                body([(_mod2(unit, self.seq_blk), _div2(unit, self.seq_blk)) for unit in units])
            return carry

        lax.fori_loop(0, n_units // self.unroll, step, 0)

    def for_units(self, size, body):
        def group(units):
            for s, c in units:
                body(s, c)
        self.for_unit_groups(size, group)

    def valid_rows(self, j, c, size):
        t = j * self.t_blk + c * size + lax.broadcasted_iota(jnp.int32, (size, 1), 0)
        return t < self.length


def _seq_call(kernel, plan, in_specs, out_specs, out_shape, scratch, name):
    return pl.pallas_call(
        kernel, grid=plan.grid, in_specs=in_specs, out_specs=out_specs, out_shape=out_shape,
        scratch_shapes=scratch, compiler_params=_params("parallel", "arbitrary"), name=name)


def _s5_kernel(u_ref, h0_ref, bmat_ref, cmat_ref, d_ref, w1_ref, w2_ref, hs_ref, cp_ref,
               y_ref, hn_ref, s_scr, carry_scr, fin_scr, *, plan):
    j = pl.program_id(1)
    P = SSM_LANES

    @pl.when(j == 0)
    def _():
        carry_scr[...] = h0_ref[...]

    u = u_ref[...].reshape(plan.rows, D_GROUP)
    s_scr[...] = _mm(u, bmat_ref[...])
    fin_row = (plan.length - 1) % SUBLANES

    def group(s, g):
        r0 = pl.multiple_of(s * plan.t_blk + g * SUBLANES, SUBLANES)
        x_re = s_scr[pl.ds(r0, SUBLANES), 0:P]
        x_im = s_scr[pl.ds(r0, SUBLANES), P:2 * P]
        for lvl, d in enumerate((1, 2, 4)):
            a_re, a_im = hs_ref[2 * lvl], hs_ref[2 * lvl + 1]
            r_re, r_im = pltpu.roll(x_re, d, 0), pltpu.roll(x_im, d, 0)
            x_re, x_im = (x_re + a_re * r_re - a_im * r_im,
                          x_im + a_re * r_im + a_im * r_re)
        c = carry_scr[s]
        c_re, c_im = c[:, 0:P], c[:, P:2 * P]
        p_re, p_im = cp_ref[0], cp_ref[1]
        x_re, x_im = (x_re + p_re * c_re - p_im * c_im,
                      x_im + p_re * c_im + p_im * c_re)
        s_scr[pl.ds(r0, SUBLANES), 0:P] = x_re
        s_scr[pl.ds(r0, SUBLANES), P:2 * P] = x_im
        last = SUBLANES - 1
        carry_scr[s] = jnp.concatenate([x_re[last:last + 1], x_im[last:last + 1]], axis=-1)
        fin_scr[s] = jnp.concatenate([x_re[fin_row:fin_row + 1], x_im[fin_row:fin_row + 1]], axis=-1)

    plan.for_units(SUBLANES, group)

    y = _mm(s_scr[...], cmat_ref[...]) + d_ref[...] * u
    y = jax.nn.gelu(y)
    y = _mm(y, w1_ref[...]) * jax.nn.sigmoid(_mm(y, w2_ref[...]))
    t = j * plan.t_blk + _mod2(lax.broadcasted_iota(jnp.int32, (plan.rows, 1), 0), plan.t_blk)
    y_ref[...] = jnp.where(t < plan.length, y, 0.0).reshape(plan.seq_blk, plan.t_blk, D_GROUP)

    @pl.when(j == plan.t_steps - 1)
    def _():
        hn_ref[...] = fin_scr[...]


def _s5(proj, h0, layer, consts, plan):
    P2 = 2 * SSM_LANES
    in_specs = [plan.rows_spec(D_GROUP, 0), plan.seq_spec(1, P2, layer=layer),
                _const_spec((D_GROUP, P2)), _const_spec((P2, D_GROUP)), _const_spec((1, D_GROUP)),
                _const_spec((D_GROUP, D_GROUP)), _const_spec((D_GROUP, D_GROUP)),
                _const_spec((6, SUBLANES, SSM_LANES)), _const_spec((2, SUBLANES, SSM_LANES))]
    out_specs = [plan.rows_spec(D_GROUP), plan.seq_spec(1, P2)]
    out_shape = [plan.rows_shape(), jax.ShapeDtypeStruct((plan.n_seq, 1, P2), F32)]
    scratch = [pltpu.VMEM((plan.rows, P2), F32), pltpu.VMEM((plan.seq_blk, 1, P2), F32),
               pltpu.VMEM((plan.seq_blk, 1, P2), F32)]
    return _seq_call(functools.partial(_s5_kernel, plan=plan), plan, in_specs, out_specs, out_shape,
                     scratch, "s5")(proj, h0, *consts)


def _selector_specs(chunk):
    return [_const_spec((D_GROUP, D_GROUP)), _const_spec((SUMSQ_TERMS * D_GROUP, D_GROUP)),
            _const_spec((EXACT_TERMS * HEAD_DIM, D_GROUP)), _const_spec((EXACT_TERMS * D_GROUP, HEAD_DIM)),
            _const_spec((chunk, EXACT_TERMS * chunk))]


def _selectors(consts, chunk):
    return consts["bd"], consts["bd2"], consts["tile"], consts["tile_t"], consts["tri"][chunk]


def _load_state_bd(s0, tile_mat, bd_mask):
    return _sel_r(s0, tile_mat, EXACT_TERMS) * bd_mask


def _store_state_bd(s_bd, tile_mat_t):
    return _sel_r(s_bd, tile_mat_t, EXACT_TERMS)


def _hgrn_kernel(q_ref, f_ref, i_ref, g_ref, s0_ref, lb_ref, nw_ref,
                 bd_ref, bd2_ref, tile_ref, tilet_ref, tri_ref,
                 y_ref, sn_ref, st_scr, *, plan, sub):
    j = pl.program_id(1)
    C = plan.chunk
    masks = _head_masks()
    bd = bd_ref[...]
    lb = lb_ref[...]

    @pl.when(j == 0)
    def _():
        def init(s, c):
            st_scr[s] = _load_state_bd(s0_ref[s], tile_ref[...], bd)
            return c
        lax.fori_loop(0, plan.seq_blk, init, 0)

    stacked_mask = jnp.concatenate([jnp.broadcast_to(m, (sub, D_GROUP)) for m in masks], axis=0)

    def gated_inputs(s, c, valid):
        rows = pl.ds(pl.multiple_of(c * C, C), C)
        fr = f_ref[s, rows, :]
        f = lb + (1.0 - lb) * jax.nn.sigmoid(fr)
        k = jnp.where(valid, (1.0 - lb) * jax.nn.sigmoid(-fr), 0.0)
        logf = jnp.where(valid, jnp.log(f), 0.0)
        return _silu(q_ref[s, rows, :]), k, i_ref[s, rows, :], logf

    def scores(q, k, cum, blk):
        r0, r1 = blk * sub, (blk + 1) * sub
        c0 = cum[r0 - 1:r0] if blk else jnp.zeros_like(cum[0:1])
        qs = _stack_heads(q[r0:r1] * jnp.exp(cum[r0:r1] - c0), masks)
        att = _mm_nt(qs, k[0:r1] * jnp.exp(c0 - cum[0:r1]))
        t_idx = r0 + _mod2(lax.broadcasted_iota(jnp.int32, att.shape, 0), sub)
        s_idx = lax.broadcasted_iota(jnp.int32, att.shape, 1)
        return jnp.where(s_idx <= t_idx, att, 0.0)

    def chunk(units):
        each = lambda fn, *xs: [fn(*a) for a in zip(*xs)]
        valid = [plan.valid_rows(j, c, C) for _, c in units]
        qkvl = [gated_inputs(s, c, ok) for (s, c), ok in zip(units, valid)]
        q, k, v, logf = ([t[i] for t in qkvl] for i in range(4))
        cum = each(lambda x: _sel_l(tri_ref[...], x, EXACT_TERMS), logf)
        st = [st_scr[s] for s, _ in units]
        o = each(lambda a, b, x: _mm_nt(a * jnp.exp(b), x), q, cum, st)
        intra = [[] for _ in units]
        for blk in range(C // sub):
            att = each(lambda a, b, d: scores(a, b, d, blk), q, k, cum)
            pv = each(lambda a, x: _mm(a, x[0:(blk + 1) * sub]), att, v)
            for lst, x in zip(intra, pv):
                lst.append(_unstack_heads(x * stacked_mask, N_HEADS))
        o = each(lambda a, lst: a + jnp.concatenate(lst, axis=0), o, intra)
        upd = each(lambda x, b, d: _mm_tn(x, b * jnp.exp(d[C - 1:C] - d)) * bd, v, k, cum)
        for (s, _), x, d, u in zip(units, st, cum, upd):
            st_scr[s] = x * jnp.exp(d[C - 1:C]) + u
        ms = each(lambda x: _sel_r(x * x, bd2_ref[...], SUMSQ_TERMS) * (1.0 / HEAD_DIM), o)
        for (s, c), x, ss, ok in zip(units, o, ms, valid):
            rows = pl.ds(pl.multiple_of(c * C, C), C)
            x = x * lax.rsqrt(ss + NORM_EPS) * nw_ref[...] * _silu(g_ref[s, rows, :])
            y_ref[s, rows, :] = jnp.where(ok, x, 0.0)

    plan.for_unit_groups(C, chunk)

    @pl.when(j == plan.t_steps - 1)
    def _():
        def fin(s, c):
            sn_ref[s] = _store_state_bd(st_scr[s], tilet_ref[...])
            return c
        lax.fori_loop(0, plan.seq_blk, fin, 0)


def _hgrn(proj, s0, layer, lb, norm_w, consts, plan):
    C = plan.chunk
    sub = min(16, C)
    in_specs = [plan.rows_spec(D_GROUP, 1), plan.rows_spec(D_GROUP, 2), plan.rows_spec(D_GROUP, 3),
                plan.rows_spec(D_GROUP, 4), plan.seq_spec(D_GROUP, HEAD_DIM, layer=layer),
                _const_spec((1, D_GROUP)), _const_spec((1, D_GROUP))] + _selector_specs(C)
    out_specs = [plan.rows_spec(D_GROUP), plan.seq_spec(D_GROUP, HEAD_DIM)]
    out_shape = [plan.rows_shape(), jax.ShapeDtypeStruct((plan.n_seq, D_GROUP, HEAD_DIM), F32)]
    scratch = [pltpu.VMEM((plan.seq_blk, D_GROUP, D_GROUP), F32)]
    kern = functools.partial(_hgrn_kernel, plan=plan, sub=sub)
    return _seq_call(kern, plan, in_specs, out_specs, out_shape, scratch, "hgrn2")(
        proj, proj, proj, proj, s0, lb, norm_w, *_selectors(consts, C))


def _gdn_kernel(q_ref, k_ref, v_ref, z_ref, sc_ref, s0_ref, cv0_ref, cw_ref, alog_ref, dtb_ref, nw_ref,
                bd_ref, bd2_ref, tile_ref, tilet_ref, tri_ref, eb_ref, ea_ref, lvl_ref,
                y_ref, sn_ref, cvn_ref, st_scr, cx_scr, prev_scr, *, plan):
    j = pl.program_id(1)
    C = plan.chunk
    H = N_HEADS
    HC = H * C
    masks = _head_masks()
    bd = bd_ref[...]
    lane = lax.broadcasted_iota(jnp.int32, (1, D_GROUP), 1)
    beta_lanes = lane < H
    a_lanes = (lane >= H) & (lane < 2 * H)
    neg_rate = jnp.where(a_lanes, -jnp.exp(alog_ref[...]), 0.0)
    n_lvl = max(1, int(math.ceil(math.log2(C))))
    valid_last = (plan.length - 1) % C + 1
    assert valid_last >= GD_CONV - 1
    pad0 = SUBLANES - (GD_CONV - 1)

    @pl.when(j == 0)
    def _():
        def init(s, c):
            st_scr[s] = _load_state_bd(s0_ref[s], tile_ref[...], bd)
            prev_scr[s] = jnp.zeros((SUBLANES, 3 * D_GROUP), F32)
            prev_scr[s, pl.ds(pad0, GD_CONV - 1), :] = cv0_ref[s]
            return c
        lax.fori_loop(0, plan.seq_blk, init, 0)

    row_i = lax.broadcasted_iota(jnp.int32, (HC, HC), 0)
    col_i = lax.broadcasted_iota(jnp.int32, (HC, HC), 1)
    same_head = _div2(row_i, C) == _div2(col_i, C)
    incl = same_head & (col_i <= row_i)
    strict = same_head & (col_i < row_i)
    eye = (row_i == col_i).astype(F32)

    ones_row = jnp.ones((SUBLANES, EXACT_TERMS * D_GROUP), BF16)

    def conv_qkv(s, c):
        rows = pl.ds(pl.multiple_of(c * C, C), C)
        cx = cx_scr.at[s]
        cx[0:SUBLANES, :] = prev_scr[s]
        cx[SUBLANES:SUBLANES + C, 0:D_GROUP] = q_ref[s, rows, :]
        cx[SUBLANES:SUBLANES + C, D_GROUP:2 * D_GROUP] = k_ref[s, rows, :]
        cx[SUBLANES:SUBLANES + C, 2 * D_GROUP:3 * D_GROUP] = v_ref[s, rows, :]
        cw = cw_ref[...]
        conv = cx[pad0:pad0 + C, :] * cw[0:1]
        for tap in range(1, GD_CONV):
            conv = conv + cx[pad0 + tap:pad0 + tap + C, :] * cw[tap:tap + 1]
        prev_scr[s, pl.ds(pad0, GD_CONV - 1), :] = cx[SUBLANES + C - (GD_CONV - 1):SUBLANES + C, :]
        cvn_ref[s] = cx[SUBLANES + valid_last - (GD_CONV - 1):SUBLANES + valid_last, :]
        conv = _silu(conv)
        return conv[:, 0:D_GROUP], conv[:, D_GROUP:2 * D_GROUP], conv[:, 2 * D_GROUP:3 * D_GROUP]

    def gates(s, c, valid):
        sc = sc_ref[s, pl.ds(pl.multiple_of(c * C, C), C), :]
        beta_all = jnp.where(valid & beta_lanes, jax.nn.sigmoid(sc), 0.0)
        g_all = jnp.where(valid, neg_rate * jax.nn.softplus(sc + dtb_ref[...]), 0.0)
        return beta_all, g_all

    def decay_matrix(gc_all, gc_row):
        gc_col = jnp.concatenate([_lane_col(gc_all, H + h) for h in range(H)], axis=0)
        return jnp.exp(jnp.where(incl, gc_col - gc_row, -jnp.inf))

    def chunk(units):
        each = lambda f, *xs: [f(*a) for a in zip(*xs)]
        valid = [plan.valid_rows(j, c, C) for _, c in units]
        qkv = [conv_qkv(s, c) for s, c in units]
        q, k, v = ([t[i] for t in qkv] for i in range(3))
        qss = each(lambda x: _sel_r(x * x, bd2_ref[...], SUMSQ_TERMS), q)
        kss = each(lambda x: _sel_r(x * x, bd2_ref[...], SUMSQ_TERMS), k)
        q = each(lambda x, ss: x * lax.rsqrt(ss + NORM_EPS) * (HEAD_DIM ** -0.5), q, qss)
        k = each(lambda x, ss, ok: jnp.where(ok, x * lax.rsqrt(ss + NORM_EPS), 0.0), k, kss, valid)

        bg = [gates(s, c, ok) for (s, c), ok in zip(units, valid)]
        beta_all, g_all = [t[0] for t in bg], [t[1] for t in bg]
        gc_all = each(lambda g: _sel_l(tri_ref[...], g, EXACT_TERMS), g_all)
        beta_l = each(lambda b: _sel_r(b, eb_ref[...], EXACT_TERMS), beta_all)
        gc_l = each(lambda g: _sel_r(g, ea_ref[...], EXACT_TERMS), gc_all)
        gam_l = each(jnp.exp, gc_l)
        g_sel = each(lambda g: jnp.concatenate([jnp.where(lane == H + h, g, 0.0) for h in range(H)], axis=0), gc_all)
        gc_row = each(lambda g: _sel_nt(ones_row, g, EXACT_TERMS)[0:1], g_sel)
        dec = each(decay_matrix, gc_all, gc_row)
        beta_col = each(lambda b: jnp.concatenate([_lane_col(b, h) for h in range(H)], axis=0), beta_all)

        ks = each(lambda x: _stack_heads(x, masks), k)
        qs = each(lambda x: _stack_heads(x, masks), q)
        kk = each(_mm_nt, ks, ks)
        m = each(lambda d, b, x: jnp.where(strict, d, 0.0) * b * x, dec, beta_col, kk)
        t_inv = each(lambda x: eye - x * lvl_ref[0], m)
        for lvl in range(1, n_lvl):
            half = each(lambda t, x: _mms(t, x * lvl_ref[lvl]), t_inv, m)
            t_inv = each(lambda t, hf: t - _mms(hf, t), t_inv, half)
        rhs_w = each(lambda b, g, x: _stack_heads(b * g * x, masks), beta_l, gam_l, k)
        rhs_u = each(lambda b, x: _stack_heads(b * x, masks), beta_l, v)
        w = each(lambda t, r: _unstack_heads(_mms(t, r), H), t_inv, rhs_w)
        u0 = each(lambda t, r: _unstack_heads(_mms(t, r), H), t_inv, rhs_u)
        qk = each(lambda a, b, d: _mm_nt(a, b) * d, qs, ks, dec)

        st = [st_scr[s] for s, _ in units]
        u = each(lambda a, b, x: a - _mm(b, x), u0, w, st)
        o_st = each(lambda x, g, y: _mm(x * g, y), q, gam_l, st)
        o_in = each(lambda a, b: _unstack_heads(_mm(a, _stack_heads(b, masks)), H), qk, u)
        upd = each(lambda x, g, b: _mm_tn(x * jnp.exp(g[C - 1:C] - g), b) * bd, k, gc_l, u)
        for (s, _), g, x, d in zip(units, gc_l, st, upd):
            st_scr[s] = jnp.exp(g[C - 1:C]) * x + d

        o = each(jnp.add, o_st, o_in)
        ms = each(lambda x: _sel_r(x * x, bd2_ref[...], SUMSQ_TERMS) * (1.0 / HEAD_DIM), o)
        for (s, c), x, ss, ok in zip(units, o, ms, valid):
            rows = pl.ds(pl.multiple_of(c * C, C), C)
            x = x * lax.rsqrt(ss + NORM_EPS) * nw_ref[...] * _silu(z_ref[s, rows, :])
            y_ref[s, rows, :] = jnp.where(ok, x, 0.0)

    plan.for_unit_groups(C, chunk)

    @pl.when(j == plan.t_steps - 1)
    def _():
        def fin(s, c):
            sn_ref[s] = _store_state_bd(st_scr[s], tilet_ref[...])
            return c
        lax.fori_loop(0, plan.seq_blk, fin, 0)


def _gdn(proj, s0, cv0, layer, lw, consts, plan):
    C = plan.chunk
    W3 = 3 * D_GROUP
    in_specs = [plan.rows_spec(D_GROUP, 5), plan.rows_spec(D_GROUP, 6), plan.rows_spec(D_GROUP, 7),
                plan.rows_spec(D_GROUP, 8), plan.rows_spec(D_GROUP, 11),
                plan.seq_spec(D_GROUP, HEAD_DIM, layer=layer), plan.seq_spec(GD_CONV - 1, W3, layer=layer),
                _const_spec((GD_CONV, W3)), _const_spec((1, D_GROUP)), _const_spec((1, D_GROUP)),
                _const_spec((1, D_GROUP))] + _selector_specs(C) + [
                _const_spec((EXACT_TERMS * D_GROUP, D_GROUP)), _const_spec((EXACT_TERMS * D_GROUP, D_GROUP)),
                _const_spec(consts["solve"][C].shape)]
    out_specs = [plan.rows_spec(D_GROUP), plan.seq_spec(D_GROUP, HEAD_DIM), plan.seq_spec(GD_CONV - 1, W3)]
    out_shape = [plan.rows_shape(),
                 jax.ShapeDtypeStruct((plan.n_seq, D_GROUP, HEAD_DIM), F32),
                 jax.ShapeDtypeStruct((plan.n_seq, GD_CONV - 1, W3), F32)]
    scratch = [pltpu.VMEM((plan.seq_blk, D_GROUP, D_GROUP), F32),
               pltpu.VMEM((plan.seq_blk, SUBLANES + C, W3), F32),
               pltpu.VMEM((plan.seq_blk, SUBLANES, W3), F32)]
    kern = functools.partial(_gdn_kernel, plan=plan)
    return _seq_call(kern, plan, in_specs, out_specs, out_shape, scratch, "gdn")(
        proj, proj, proj, proj, proj, s0, cv0, lw["conv_w"], lw["a_log"], lw["dt_bias"], lw["norm_w"],
        *_selectors(consts, C), consts["eb"], consts["ea"], consts["solve"][C])


def _swa_kernel(q_ref, kv_ref, ck_ref, cv_ref, bias_ref, sink_ref, y_ref, kn_ref, vn_ref,
                kk_scr, vv_scr, *, plan, has_cache):
    j = pl.program_id(1)
    QB = plan.chunk
    W = SW_WINDOW
    HK = SW_KV_HEADS * HEAD_DIM
    kv_masks = _head_masks(HK)
    valid_last = (plan.length - 1) % QB + 1

    @pl.when(j == 0)
    def _():
        kn_ref[...] = ck_ref[...]
        vn_ref[...] = cv_ref[...]

    def stacked_queries(s, c):
        rows = pl.ds(pl.multiple_of(c * QB, QB), QB)
        kk, vv = kk_scr.at[s], vv_scr.at[s]
        kk[0:W, :] = kn_ref[s]
        vv[0:W, :] = vn_ref[s]
        kv = kv_ref[s, rows, :]
        kk[W:W + QB, :] = kv[:, 0:HK]
        vv[W:W + QB, :] = kv[:, HK:2 * HK]
        q = q_ref[s, rows, :]
        return jnp.concatenate([q[:, 0:HK] * kv_masks[0], q[:, 0:HK] * kv_masks[1],
                                q[:, HK:2 * HK] * kv_masks[0], q[:, HK:2 * HK] * kv_masks[1]], axis=0)

    def probabilities(sc, c):
        sc = sc * (HEAD_DIM ** -0.5) + bias_ref[...]
        if not has_cache:
            t0 = j * plan.t_blk + c * QB
            col = lax.broadcasted_iota(jnp.int32, sc.shape, 1)
            sc = jnp.where(t0 + col < W, -jnp.inf, sc)
        sink = sink_ref[...]
        mx = jnp.maximum(jnp.max(sc, axis=-1, keepdims=True), sink)
        p = jnp.exp(sc - mx)
        denom = jnp.sum(p, axis=-1, keepdims=True) + jnp.exp(sink - mx)
        return p / denom

    def block(units):
        qs = [stacked_queries(s, c) for s, c in units]
        sc = [_mm_nt(x, kk_scr[s]) for x, (s, _) in zip(qs, units)]
        p = [probabilities(x, c) for x, (_, c) in zip(sc, units)]
        pv = [_mm(x, vv_scr[s]) for x, (s, _) in zip(p, units)]
        for x, (s, c) in zip(pv, units):
            oa = x[0:QB] * kv_masks[0] + x[QB:2 * QB] * kv_masks[1]
            ob = x[2 * QB:3 * QB] * kv_masks[0] + x[3 * QB:4 * QB] * kv_masks[1]
            rows = pl.ds(pl.multiple_of(c * QB, QB), QB)
            y_ref[s, rows, :] = jnp.where(plan.valid_rows(j, c, QB), jnp.concatenate([oa, ob], axis=-1), 0.0)
            kn_ref[s] = kk_scr[s, valid_last:valid_last + W, :]
            vn_ref[s] = vv_scr[s, valid_last:valid_last + W, :]

    plan.for_unit_groups(QB, block)


def _swa(proj, cache_k, cache_v, layer, bias, sink_col, plan, has_cache):
    QB = plan.chunk
    HK = SW_KV_HEADS * HEAD_DIM
    NK = SW_WINDOW + QB
    in_specs = [plan.rows_spec(D_GROUP, 9), plan.rows_spec(D_GROUP, 10),
                plan.seq_spec(SW_WINDOW, HK, layer=layer), plan.seq_spec(SW_WINDOW, HK, layer=layer),
                _const_spec((4 * QB, NK)), _const_spec((4 * QB, 1))]
    out_specs = [plan.rows_spec(D_GROUP), plan.seq_spec(SW_WINDOW, HK), plan.seq_spec(SW_WINDOW, HK)]
    out_shape = [plan.rows_shape(),
                 jax.ShapeDtypeStruct((plan.n_seq, SW_WINDOW, HK), F32),
                 jax.ShapeDtypeStruct((plan.n_seq, SW_WINDOW, HK), F32)]
    scratch = [pltpu.VMEM((plan.seq_blk, NK, HK), F32), pltpu.VMEM((plan.seq_blk, NK, HK), F32)]
    kern = functools.partial(_swa_kernel, plan=plan, has_cache=has_cache)
    return _seq_call(kern, plan, in_specs, out_specs, out_shape, scratch, "swa")(
        proj, proj, cache_k, cache_v, bias, sink_col)


_Q_HEAD_ORDER = (0, 2, 1, 3)


def _swa_bias(qb):
    w = SW_WINDOW
    i = np.arange(qb)[:, None]
    jj = np.arange(w + qb)[None, :]
    dist = w + i - jj
    ok = (dist >= 0) & (dist <= w)
    slopes = 2.0 ** (-8.0 * np.arange(1, N_HEADS + 1) / N_HEADS)
    blocks = []
    for g in range(2):
        for kv in range(SW_KV_HEADS):
            head = kv * 2 + g
            blocks.append(np.where(ok, -slopes[head] * dist, -np.inf))
    return jnp.asarray(np.concatenate(blocks, axis=0), F32)


def _constants(chunks):
    lane_head = np.arange(D_GROUP) // HEAD_DIM
    bd = (lane_head[:, None] == lane_head[None, :]).astype(np.float32)
    tile = np.tile(np.eye(HEAD_DIM, dtype=np.float32), (1, N_HEADS))
    eb = np.zeros((D_GROUP, D_GROUP), np.float32)
    ea = np.zeros((D_GROUP, D_GROUP), np.float32)
    for h in range(N_HEADS):
        eb[h, lane_head == h] = 1.0
        ea[N_HEADS + h, lane_head == h] = 1.0
    def solve_levels(c):
        blk = np.arange(N_HEADS * c) >> 1
        out = [blk[:, None] == blk[None, :]]
        while (1 << len(out)) < c:
            out.append(((blk >> 1)[:, None] == (blk >> 1)[None, :]) & (blk[:, None] != blk[None, :]))
            blk = blk >> 1
        return jnp.asarray(np.stack(out).astype(np.float32))

    rows = lambda m, n: jnp.asarray(np.tile(m, (n, 1)), BF16)
    cols = lambda m, n: jnp.asarray(np.tile(m, (1, n)), BF16)
    return {"bd": jnp.asarray(bd), "bd2": rows(bd, SUMSQ_TERMS),
            "tile": rows(tile, EXACT_TERMS), "tile_t": rows(tile.T, EXACT_TERMS),
            "eb": rows(eb, EXACT_TERMS), "ea": rows(ea, EXACT_TERMS),
            "tri": {c: cols(np.tril(np.ones((c, c), np.float32)), EXACT_TERMS) for c in chunks},
            "solve": {c: solve_levels(c) for c in chunks}}


def _cmul(ar, ai, br, bi):
    return ar * br - ai * bi, ar * bi + ai * br


def _s5_consts(lam_re, lam_im, log_dt, b_re, b_im, c_re, c_im, d, w1, w2):
    dt = jnp.exp(log_dt)[:, None]
    mag = jnp.exp(lam_re * dt)
    ang = lam_im * dt
    a_re, a_im = mag * jnp.cos(ang), mag * jnp.sin(ang)
    den = lam_re * lam_re + lam_im * lam_im
    z_re = ((a_re - 1.0) * lam_re + a_im * lam_im) / den
    z_im = (a_im * lam_re - (a_re - 1.0) * lam_im) / den
    bb_re = z_re[..., None] * b_re - z_im[..., None] * b_im
    bb_im = z_re[..., None] * b_im + z_im[..., None] * b_re
    eye = jnp.eye(SSM_GROUPS, dtype=F32)
    to_b = lambda t: jnp.einsum("gpc,gh->gchp", t, eye).reshape(D_GROUP, SSM_LANES)
    to_c = lambda t: jnp.einsum("gcp,gh->gphc", t, eye).reshape(SSM_LANES, D_GROUP)
    bmat = jnp.concatenate([to_b(bb_re), to_b(bb_im)], axis=1).astype(BF16)
    cmat = jnp.concatenate([to_c(c_re), -to_c(c_im)], axis=0).astype(BF16)
    ar, ai = a_re.reshape(1, SSM_LANES), a_im.reshape(1, SSM_LANES)
    pows = [(ar, ai)]
    for _ in range(SUBLANES - 1):
        pows.append(_cmul(pows[-1][0], pows[-1][1], ar, ai))
    t = jnp.arange(SUBLANES)[:, None]
    hs = []
    for dd in (1, 2, 4):
        hs.append(jnp.where(t >= dd, pows[dd - 1][0], 0.0))
        hs.append(jnp.where(t >= dd, pows[dd - 1][1], 0.0))
    cp = [jnp.concatenate([p[0] for p in pows], axis=0), jnp.concatenate([p[1] for p in pows], axis=0)]
    return (bmat, cmat, d.reshape(1, D_GROUP), w1.astype(BF16), w2.astype(BF16),
            jnp.stack(hs, axis=0), jnp.stack(cp, axis=0))


def _prep_w_in(w):
    q_d = w[:, :, 2312:2568].reshape(DEPTH, D_MODEL, N_HEADS, HEAD_DIM)[:, :, np.array(_Q_HEAD_ORDER)]
    scal = jnp.pad(w[:, :, 2304:2312], ((0, 0), (0, 0), (0, D_GROUP - 2 * N_HEADS)))
    return jnp.concatenate([w[:, :, 0:2304], q_d.reshape(DEPTH, D_MODEL, D_GROUP), w[:, :, 2568:2824], scal],
                           axis=2).astype(BF16)


def _prep_w_out(w):
    d_rows = w[:, 3 * D_GROUP:].reshape(DEPTH, N_HEADS, HEAD_DIM, D_MODEL)[:, np.array(_Q_HEAD_ORDER)]
    return jnp.concatenate([w[:, :3 * D_GROUP], d_rows.reshape(DEPTH, D_GROUP, D_MODEL)], axis=1).astype(BF16)


def _lane_vec(vals, first_lane):
    return jnp.zeros((1, D_GROUP), F32).at[0, first_lane:first_lane + vals.shape[0]].set(vals)


def _trunk(x, state, layers, big, consts, plan, tm, has_cache, bias):
    outs = {k: [] for k in ("ssm", "hgrn", "gdn", "conv", "swa_k", "swa_v")}
    for l, lw in enumerate(layers):
        gn = lw["gains"]
        x = _ffn(x, gn[0], gn[1], big["wg"], big["wu"], big["wd"], (l, 0), tm)
        proj = _inproj(x, gn[2], big["w_in"], (l,), tm).reshape(plan.n_seq, plan.l_pad, PROJ_W)
        sl = l if has_cache else 0
        y_a, ssm = _s5(proj, state["ssm"], sl, lw["s5"], plan)
        y_b, hg = _hgrn(proj, state["hgrn"], sl, lw["lb"], lw["hgrn_nw"], consts, plan)
        y_c, gd, cv = _gdn(proj, state["gdn"], state["conv"], sl, lw["gdn"], consts, plan)
        y_d, ck, cvv = _swa(proj, state["swa_k"], state["swa_v"], sl, bias, lw["sink_col"], plan, has_cache)
        ys = tuple(y.reshape(x.shape[0], D_GROUP) for y in (y_a, y_b, y_c, y_d))
        x = _ffn(x, gn[4], gn[5], big["wg"], big["wu"], big["wd"], (l, 1), tm, mix=(ys, big["w_out"], gn[3]))
        for k, val in zip(outs, (ssm, hg, gd, cv, ck, cvv)):
            outs[k].append(val)
    return x, {k: jnp.stack(v, axis=0) for k, v in outs.items()}


def _finish_states(st, n_seq):
    ssm = st["ssm"].reshape(DEPTH, n_seq, 2, SSM_GROUPS, SSM_STATE)
    hg = jnp.swapaxes(st["hgrn"].reshape(DEPTH, n_seq, N_HEADS, HEAD_DIM, HEAD_DIM), -1, -2)
    gd = st["gdn"].reshape(DEPTH, n_seq, N_HEADS, HEAD_DIM, HEAD_DIM)
    sk = st["swa_k"].reshape(DEPTH, n_seq, SW_WINDOW, SW_KV_HEADS, HEAD_DIM)
    sv = st["swa_v"].reshape(DEPTH, n_seq, SW_WINDOW, SW_KV_HEADS, HEAD_DIM)
    return ssm[:, :, 0], ssm[:, :, 1], hg, gd, st["conv"], sk, sv


def kernel(x_prompt, x_sample, state_ssm_re, state_ssm_im, state_hgrn, state_gdn, state_gdn_conv,
           cache_swa_k, cache_swa_v, norm_gains, ffn_w_gate, ffn_w_up, ffn_w_down, w_in, w_out,
           ssm_lambda_re, ssm_lambda_im, ssm_log_dt, ssm_b_re, ssm_b_im, ssm_c_re, ssm_c_im, ssm_d,
           ssm_w_glu1, ssm_w_glu2, hgrn_lb_logits, hgrn_norm_w, gdn_conv_w, gdn_a_log, gdn_dt_bias,
           gdn_norm_w, swa_sinks):
    bp, lp, _ = x_prompt.shape
    bs, ls, _ = x_sample.shape
    ls_pad = -(-ls // SUBLANES) * SUBLANES
    plan_p = _SeqPlan(bp, lp, lp, seq_blk=min(bp, 4), t_blk=min(lp, 256), chunk=min(lp, 64))
    plan_s = _SeqPlan(bs, ls, ls_pad, seq_blk=min(bs, 16), t_blk=ls_pad, chunk=ls_pad)
    consts = _constants({plan_p.chunk, plan_s.chunk})

    gam = jax.nn.softmax(hgrn_lb_logits.astype(F32), axis=0)
    lbs = jnp.cumsum(gam, axis=0) - gam[:1]
    big = {"wg": ffn_w_gate.astype(BF16), "wu": ffn_w_up.astype(BF16), "wd": ffn_w_down.astype(BF16),
           "w_in": _prep_w_in(w_in), "w_out": _prep_w_out(w_out)}
    layers = []
    for l in range(DEPTH):
        sinks = swa_sinks[l].astype(F32)[np.array(_Q_HEAD_ORDER)]
        layers.append({
            "gains": norm_gains[l].astype(F32).reshape(6, 1, D_MODEL),
            "s5": _s5_consts(ssm_lambda_re[l], ssm_lambda_im[l], ssm_log_dt[l], ssm_b_re[l], ssm_b_im[l],
                             ssm_c_re[l], ssm_c_im[l], ssm_d[l], ssm_w_glu1[l], ssm_w_glu2[l]),
            "lb": lbs[l].reshape(1, D_GROUP),
            "hgrn_nw": jnp.tile(hgrn_norm_w[l].astype(F32), N_HEADS).reshape(1, D_GROUP),
            "gdn": {"conv_w": gdn_conv_w[l].astype(F32),
                    "a_log": _lane_vec(gdn_a_log[l].astype(F32), N_HEADS),
                    "dt_bias": _lane_vec(gdn_dt_bias[l].astype(F32), N_HEADS),
                    "norm_w": jnp.tile(gdn_norm_w[l].astype(F32), N_HEADS).reshape(1, D_GROUP)},
            "sinks": sinks,
        })

    def run(x, n_seq, length, plan, state, has_cache):
        pad = plan.l_pad - length
        if pad:
            x = jnp.pad(x, ((0, 0), (0, pad), (0, 0)))
        x2 = x.reshape(n_seq * plan.l_pad, D_MODEL)
        for lw in layers:
            lw["sink_col"] = jnp.repeat(lw["sinks"], plan.chunk).reshape(4 * plan.chunk, 1)
        tm = min(512, x2.shape[0])
        y, st = _trunk(x2, state, layers, big, consts, plan, tm, has_cache, _swa_bias(plan.chunk))
        y = y.reshape(n_seq, plan.l_pad, D_MODEL)[:, :length]
        return (y,) + _finish_states(st, n_seq)

    HK = SW_KV_HEADS * HEAD_DIM
    zeros = lambda *shape: jnp.zeros((1, bp) + shape, F32)
    prompt_state = {"ssm": zeros(1, 2 * SSM_LANES), "hgrn": zeros(D_GROUP, HEAD_DIM),
                    "gdn": zeros(D_GROUP, HEAD_DIM), "conv": zeros(GD_CONV - 1, 3 * D_GROUP),
                    "swa_k": zeros(SW_WINDOW, HK), "swa_v": zeros(SW_WINDOW, HK)}
    sample_state = {
        "ssm": jnp.concatenate([state_ssm_re.reshape(DEPTH, bs, 1, SSM_LANES),
                                state_ssm_im.reshape(DEPTH, bs, 1, SSM_LANES)], axis=-1).astype(F32),
        "hgrn": jnp.swapaxes(state_hgrn.astype(F32), -1, -2).reshape(DEPTH, bs, D_GROUP, HEAD_DIM),
        "gdn": state_gdn.astype(F32).reshape(DEPTH, bs, D_GROUP, HEAD_DIM),
        "conv": state_gdn_conv.astype(F32),
        "swa_k": cache_swa_k.astype(F32).reshape(DEPTH, bs, SW_WINDOW, HK),
        "swa_v": cache_swa_v.astype(F32).reshape(DEPTH, bs, SW_WINDOW, HK)}

    yp, sre_p, sim_p, hg_p, gd_p, cv_p, sk_p, sv_p = run(x_prompt, bp, lp, plan_p, prompt_state, False)
    ys, sre_s, sim_s, hg_s, gd_s, cv_s, sk_s, sv_s = run(x_sample, bs, ls, plan_s, sample_state, True)
    return (yp, ys, sre_p, sre_s, sim_p, sim_s, hg_p, hg_s, gd_p, gd_s,
            cv_p, cv_s, sk_p, sk_s, sv_p, sv_s)
```

```python
import functools
import math

import numpy as np
import jax
import jax.numpy as jnp
from jax import lax
from jax.experimental import pallas as pl
from jax.experimental.pallas import tpu as pltpu

D_MODEL = 1024
DEPTH = 4
D_GROUP = 256
N_HEADS = 4
HEAD_DIM = 64
SSM_CH = 16
SSM_GROUPS = 16
SSM_STATE = 64
SSM_LANES = SSM_GROUPS * SSM_STATE
GD_CONV = 4
SW_WINDOW = 128
SW_KV_HEADS = 2
D_FF = 2816
NORM_EPS = 1e-6
PROJ_W = 12 * D_GROUP
SUBLANES = 8
LANES = 128
VMEM_LIMIT = 56 * 1024 * 1024

F32 = jnp.float32
BF16 = jnp.bfloat16
EXACT_TERMS = 3
SUMSQ_TERMS = 2


def _mm(a, b):
    return jnp.dot(a.astype(BF16), b.astype(BF16), preferred_element_type=F32)


def _mm_nt(a, b):
    return lax.dot_general(a.astype(BF16), b.astype(BF16), (((1,), (1,)), ((), ())),
                           preferred_element_type=F32)


def _mm_tn(a, b):
    return lax.dot_general(a.astype(BF16), b.astype(BF16), (((0,), (0,)), ((), ())),
                           preferred_element_type=F32)


def _mms(a, b):
    return _mm(a, b)


def _split_bf16(x, terms):
    parts = []
    for _ in range(terms - 1):
        p = x.astype(BF16)
        parts.append(p)
        x = x - p.astype(F32)
    parts.append(x.astype(BF16))
    return parts


def _sel_r(a, sel_stack, terms):
    return jnp.dot(jnp.concatenate(_split_bf16(a, terms), axis=1), sel_stack, preferred_element_type=F32)


def _sel_l(sel_stack, b, terms):
    return jnp.dot(sel_stack, jnp.concatenate(_split_bf16(b, terms), axis=0), preferred_element_type=F32)


def _sel_nt(sel_stack, b, terms):
    return lax.dot_general(sel_stack, jnp.concatenate(_split_bf16(b, terms), axis=1),
                           (((1,), (1,)), ((), ())), preferred_element_type=F32)


def _rms(x, gain):
    return x * lax.rsqrt(jnp.mean(x * x, axis=-1, keepdims=True) + NORM_EPS) * gain


def _silu(x):
    return x * jax.nn.sigmoid(x)


def _div2(x, d):
    assert d & (d - 1) == 0
    return x >> (d.bit_length() - 1)


def _mod2(x, d):
    assert d & (d - 1) == 0
    return x & (d - 1)


def _head_masks(width=D_GROUP, head_dim=HEAD_DIM):
    lane = lax.broadcasted_iota(jnp.int32, (1, width), 1)
    return [(_div2(lane, head_dim) == h).astype(F32) for h in range(width // head_dim)]


def _stack_heads(x, masks):
    return jnp.concatenate([x * m for m in masks], axis=0)


def _unstack_heads(x, n_heads):
    c = x.shape[0] // n_heads
    out = x[0:c]
    for h in range(1, n_heads):
        out = out + x[h * c:(h + 1) * c]
    return out


def _lane_col(x, lane):
    idx = lax.broadcasted_iota(jnp.int32, x.shape, 1)
    return jnp.sum(jnp.where(idx == lane, x, 0.0), axis=-1, keepdims=True)


def _const_spec(shape):
    return pl.BlockSpec(shape, lambda *_: (0,) * len(shape), pipeline_mode=pl.Buffered(1))


def _layer_spec(shape, lead):
    idx = tuple(lead) + (0,) * len(shape)
    return pl.BlockSpec((None,) * len(lead) + tuple(shape), lambda *_: idx, pipeline_mode=pl.Buffered(1))


def _params(*sem):
    return pltpu.CompilerParams(dimension_semantics=sem, vmem_limit_bytes=VMEM_LIMIT)


def _ffn_kernel(x_ref, gpre_ref, gpost_ref, wg_ref, wu_ref, wd_ref, *rest, mixed):
    x = x_ref[...]
    if mixed:
        ya_ref, yb_ref, yc_ref, yd_ref, wo_ref, gmix_ref, o_ref = rest
        y = jnp.concatenate([ya_ref[...], yb_ref[...], yc_ref[...], yd_ref[...]], axis=-1)
        x = x + _rms(jnp.dot(y.astype(BF16), wo_ref[...], preferred_element_type=F32), gmix_ref[...])
    else:
        o_ref, = rest
    h = _rms(x, gpre_ref[...]).astype(BF16)
    g = jnp.dot(h, wg_ref[...], preferred_element_type=F32)
    u = jnp.dot(h, wu_ref[...], preferred_element_type=F32)
    a = (_silu(g) * u).astype(BF16)
    y = jnp.dot(a, wd_ref[...], preferred_element_type=F32)
    o_ref[...] = x + 0.5 * _rms(y, gpost_ref[...])


def _ffn(x, gpre, gpost, wg, wu, wd, lead, tm, mix=None):
    n = x.shape[0]
    row = pl.BlockSpec((tm, D_MODEL), lambda i: (i, 0))
    in_specs = [row, _const_spec((1, D_MODEL)), _const_spec((1, D_MODEL)),
                _layer_spec((D_MODEL, D_FF), lead), _layer_spec((D_MODEL, D_FF), lead),
                _layer_spec((D_FF, D_MODEL), lead)]
    args = [x, gpre, gpost, wg, wu, wd]
    if mix is not None:
        ys, w_out, gmix = mix
        grp = pl.BlockSpec((tm, D_GROUP), lambda i: (i, 0))
        in_specs += [grp, grp, grp, grp, _layer_spec((D_MODEL, D_MODEL), lead[:1]), _const_spec((1, D_MODEL))]
        args += [*ys, w_out, gmix]
    return pl.pallas_call(
        functools.partial(_ffn_kernel, mixed=mix is not None),
        grid=(n // tm,),
        in_specs=in_specs,
        out_specs=row,
        out_shape=jax.ShapeDtypeStruct((n, D_MODEL), F32),
        compiler_params=_params("parallel"),
        name="ffn",
    )(*args)


def _inproj_kernel(x_ref, g_ref, w_ref, o_ref):
    h = _rms(x_ref[...], g_ref[...]).astype(BF16)
    o_ref[...] = jnp.dot(h, w_ref[...], preferred_element_type=F32)


def _inproj(x, gain, w, lead, tm):
    n = x.shape[0]
    return pl.pallas_call(
        _inproj_kernel,
        grid=(n // tm,),
        in_specs=[pl.BlockSpec((tm, D_MODEL), lambda i: (i, 0)), _const_spec((1, D_MODEL)),
                  _layer_spec((D_MODEL, PROJ_W), lead)],
        out_specs=pl.BlockSpec((tm, PROJ_W), lambda i: (i, 0)),
        out_shape=jax.ShapeDtypeStruct((n, PROJ_W), F32),
        compiler_params=_params("parallel"),
        name="inproj",
    )(x, gain, w)


class _SeqPlan:
    def __init__(self, n_seq, length, l_pad, seq_blk, t_blk, chunk):
        assert l_pad % t_blk == 0 and t_blk % chunk == 0 and n_seq % seq_blk == 0
        assert chunk % SUBLANES == 0 and 0 <= l_pad - length < SUBLANES
        assert length % chunk == 0 or l_pad == chunk
        assert seq_blk & (seq_blk - 1) == 0
        self.n_seq, self.length, self.l_pad = n_seq, length, l_pad
        self.seq_blk, self.t_blk, self.chunk = seq_blk, t_blk, chunk
        self.rows = seq_blk * t_blk
        self.t_steps = l_pad // t_blk
        self.grid = (n_seq // seq_blk, self.t_steps)
        self.unroll = min(4, seq_blk)

    def rows_spec(self, width, col_block=0):
        return pl.BlockSpec((self.seq_blk, self.t_blk, width), lambda i, j: (i, j, col_block))

    def seq_spec(self, *tail, layer=None):
        zeros = (0,) * len(tail)
        if layer is None:
            return pl.BlockSpec((self.seq_blk,) + tail, lambda i, j: (i,) + zeros)
        return pl.BlockSpec((None, self.seq_blk) + tail, lambda i, j: (layer, i) + zeros)

    def rows_shape(self):
        return jax.ShapeDtypeStruct((self.n_seq, self.l_pad, D_GROUP), F32)

    def for_unit_groups(self, size, body):
        n_units = self.seq_blk * (self.t_blk // size)

        def step(n, carry):
            if self.unroll == self.seq_blk:
                body([(u, n) for u in range(self.unroll)])
            else:
                units = [n * self.unroll + u for u in range(self.unroll)]
                body([(_mod2(unit, self.seq_blk), _div2(unit, self.seq_blk)) for unit in units])
            return carry

        lax.fori_loop(0, n_units // self.unroll, step, 0)

    def for_seq_groups(self, body):
        def step(n, carry):
            body([n * self.unroll + u for u in range(self.unroll)])
            return carry
        lax.fori_loop(0, self.seq_blk // self.unroll, step, 0)

    def for_units(self, size, body):
        def group(units):
            for s, c in units:
                body(s, c)
        self.for_unit_groups(size, group)

    def valid_rows(self, j, c, size):
        t = j * self.t_blk + c * size + lax.broadcasted_iota(jnp.int32, (size, 1), 0)
        return t < self.length


def _seq_call(kernel, plan, in_specs, out_specs, out_shape, scratch, name):
    return pl.pallas_call(
        kernel, grid=plan.grid, in_specs=in_specs, out_specs=out_specs, out_shape=out_shape,
        scratch_shapes=scratch, compiler_params=_params("parallel", "arbitrary"), name=name)


def _s5_kernel(u_ref, h0_ref, bmat_ref, cmat_ref, d_ref, w1_ref, w2_ref, a_ref,
               y_ref, hn_ref, s_scr, carry_scr, *, plan):
    j = pl.program_id(1)
    P = SSM_LANES
    S = plan.seq_blk
    slab = min(SUBLANES, S)

    @pl.when(j == 0)
    def _():
        carry_scr[...] = h0_ref[...]

    u = u_ref[...].reshape(plan.rows, D_GROUP)
    s_scr[...] = _mm(u, bmat_ref[...])
    a_re, a_im = a_ref[0:slab, 0:P], a_ref[0:slab, P:2 * P]

    def step(t, x):
        out = []
        for g in range(S // slab):
            rows = pl.ds(pl.multiple_of(t * S + g * slab, slab), slab)
            x_re, x_im = x[2 * g], x[2 * g + 1]
            n_re = a_re * x_re - a_im * x_im + s_scr[rows, 0:P]
            n_im = a_re * x_im + a_im * x_re + s_scr[rows, P:2 * P]
            s_scr[rows, 0:P] = n_re
            s_scr[rows, P:2 * P] = n_im
            out += [n_re, n_im]
        return tuple(out)

    x0 = []
    for g in range(S // slab):
        x0 += [carry_scr[g * slab:(g + 1) * slab, 0:P], carry_scr[g * slab:(g + 1) * slab, P:2 * P]]
    x = lax.fori_loop(0, plan.t_blk, step, tuple(x0))
    for g in range(S // slab):
        carry_scr[g * slab:(g + 1) * slab, 0:P] = x[2 * g]
        carry_scr[g * slab:(g + 1) * slab, P:2 * P] = x[2 * g + 1]

    y = _mm(s_scr[...], cmat_ref[...]) + d_ref[...] * u
    y = jax.nn.gelu(y)
    y = _mm(y, w1_ref[...]) * jax.nn.sigmoid(_mm(y, w2_ref[...]))
    t = j * plan.t_blk + _div2(lax.broadcasted_iota(jnp.int32, (plan.rows, 1), 0), S)
    y_ref[...] = jnp.where(t < plan.length, y, 0.0).reshape(plan.t_blk, S, D_GROUP)

    @pl.when(j == plan.t_steps - 1)
    def _():
        hn_ref[...] = s_scr[((plan.length - 1) % plan.t_blk) * S:((plan.length - 1) % plan.t_blk + 1) * S, :]


def _s5(u_tm, h0, layer, consts, plan):
    P2 = 2 * SSM_LANES
    tm_spec = pl.BlockSpec((plan.t_blk, plan.seq_blk, D_GROUP), lambda i, j: (j, i, 0))
    in_specs = [tm_spec, plan.seq_spec(P2, layer=layer),
                _const_spec((D_GROUP, P2)), _const_spec((P2, D_GROUP)), _const_spec((1, D_GROUP)),
                _const_spec((D_GROUP, D_GROUP)), _const_spec((D_GROUP, D_GROUP)),
                _const_spec((SUBLANES, P2))]
    out_specs = [tm_spec, plan.seq_spec(P2)]
    out_shape = [jax.ShapeDtypeStruct((plan.l_pad, plan.n_seq, D_GROUP), F32),
                 jax.ShapeDtypeStruct((plan.n_seq, P2), F32)]
    scratch = [pltpu.VMEM((plan.rows, P2), F32), pltpu.VMEM((plan.seq_blk, P2), F32)]
    return _seq_call(functools.partial(_s5_kernel, plan=plan), plan, in_specs, out_specs, out_shape,
                     scratch, "s5")(u_tm, h0, *consts)


def _selector_specs(chunk):
    return [_const_spec((D_GROUP, D_GROUP)), _const_spec((SUMSQ_TERMS * D_GROUP, D_GROUP)),
            _const_spec((EXACT_TERMS * HEAD_DIM, D_GROUP)), _const_spec((EXACT_TERMS * D_GROUP, HEAD_DIM)),
            _const_spec((chunk, EXACT_TERMS * chunk))]


def _selectors(consts, chunk):
    return consts["bd"], consts["bd2"], consts["tile"], consts["tile_t"], consts["tri"][chunk]


def _load_state_bd(s0, tile_mat, bd_mask):
    return _sel_r(s0, tile_mat, EXACT_TERMS) * bd_mask


def _store_state_bd(s_bd, tile_mat_t):
    return _sel_r(s_bd, tile_mat_t, EXACT_TERMS)


def _hgrn_kernel(q_ref, f_ref, i_ref, g_ref, s0_ref, lb_ref, nw_ref,
                 bd_ref, bd2_ref, tile_ref, tilet_ref, tri_ref,
                 y_ref, sn_ref, st_scr, *, plan, sub):
    j = pl.program_id(1)
    C = plan.chunk
    masks = _head_masks()
    bd = bd_ref[...]
    lb = lb_ref[...]

    @pl.when(j == 0)
    def _():
        def init(seqs):
            for s, st in zip(seqs, [_load_state_bd(s0_ref[s], tile_ref[...], bd) for s in seqs]):
                st_scr[s] = st
        plan.for_seq_groups(init)

    stacked_mask = jnp.concatenate([jnp.broadcast_to(m, (sub, D_GROUP)) for m in masks], axis=0)

    def gated_inputs(s, c, valid):
        rows = pl.ds(pl.multiple_of(c * C, C), C)
        fr = f_ref[s, rows, :]
        f = lb + (1.0 - lb) * jax.nn.sigmoid(fr)
        k = jnp.where(valid, (1.0 - lb) * jax.nn.sigmoid(-fr), 0.0)
        logf = jnp.where(valid, jnp.log(f), 0.0)
        return _silu(q_ref[s, rows, :]), k, i_ref[s, rows, :], logf

    def scores(q, k, cum, blk):
        r0, r1 = blk * sub, (blk + 1) * sub
        c0 = cum[r0 - 1:r0] if blk else jnp.zeros_like(cum[0:1])
        qs = _stack_heads(q[r0:r1] * jnp.exp(cum[r0:r1] - c0), masks)
        att = _mm_nt(qs, k[0:r1] * jnp.exp(c0 - cum[0:r1]))
        t_idx = r0 + _mod2(lax.broadcasted_iota(jnp.int32, att.shape, 0), sub)
        s_idx = lax.broadcasted_iota(jnp.int32, att.shape, 1)
        return jnp.where(s_idx <= t_idx, att, 0.0)

    def chunk(units):
        each = lambda fn, *xs: [fn(*a) for a in zip(*xs)]
        valid = [plan.valid_rows(j, c, C) for _, c in units]
        qkvl = [gated_inputs(s, c, ok) for (s, c), ok in zip(units, valid)]
        q, k, v, logf = ([t[i] for t in qkvl] for i in range(4))
        cum = each(lambda x: _sel_l(tri_ref[...], x, EXACT_TERMS), logf)
        st = [st_scr[s] for s, _ in units]
        o = each(lambda a, b, x: _mm_nt(a * jnp.exp(b), x), q, cum, st)
        intra = [[] for _ in units]
        for blk in range(C // sub):
            att = each(lambda a, b, d: scores(a, b, d, blk), q, k, cum)
            pv = each(lambda a, x: _mm(a, x[0:(blk + 1) * sub]), att, v)
            for lst, x in zip(intra, pv):
                lst.append(_unstack_heads(x * stacked_mask, N_HEADS))
        o = each(lambda a, lst: a + jnp.concatenate(lst, axis=0), o, intra)
        upd = each(lambda x, b, d: _mm_tn(x, b * jnp.exp(d[C - 1:C] - d)) * bd, v, k, cum)
        for (s, _), x, d, u in zip(units, st, cum, upd):
            st_scr[s] = x * jnp.exp(d[C - 1:C]) + u
        ms = each(lambda x: _sel_r(x * x, bd2_ref[...], SUMSQ_TERMS) * (1.0 / HEAD_DIM), o)
        for (s, c), x, ss, ok in zip(units, o, ms, valid):
            rows = pl.ds(pl.multiple_of(c * C, C), C)
            x = x * lax.rsqrt(ss + NORM_EPS) * nw_ref[...] * _silu(g_ref[s, rows, :])
            y_ref[s, rows, :] = jnp.where(ok, x, 0.0)

    plan.for_unit_groups(C, chunk)

    @pl.when(j == plan.t_steps - 1)
    def _():
        def fin(seqs):
            for s, st in zip(seqs, [_store_state_bd(st_scr[s], tilet_ref[...]) for s in seqs]):
                sn_ref[s] = st
        plan.for_seq_groups(fin)


def _hgrn(proj, s0, layer, lb, norm_w, consts, plan):
    C = plan.chunk
    sub = min(16, C)
    in_specs = [plan.rows_spec(D_GROUP, 1), plan.rows_spec(D_GROUP, 2), plan.rows_spec(D_GROUP, 3),
                plan.rows_spec(D_GROUP, 4), plan.seq_spec(D_GROUP, HEAD_DIM, layer=layer),
                _const_spec((1, D_GROUP)), _const_spec((1, D_GROUP))] + _selector_specs(C)
    out_specs = [plan.rows_spec(D_GROUP), plan.seq_spec(D_GROUP, HEAD_DIM)]
    out_shape = [plan.rows_shape(), jax.ShapeDtypeStruct((plan.n_seq, D_GROUP, HEAD_DIM), F32)]
    scratch = [pltpu.VMEM((plan.seq_blk, D_GROUP, D_GROUP), F32)]
    kern = functools.partial(_hgrn_kernel, plan=plan, sub=sub)
    return _seq_call(kern, plan, in_specs, out_specs, out_shape, scratch, "hgrn2")(
        proj, proj, proj, proj, s0, lb, norm_w, *_selectors(consts, C))


def _gdn_kernel(q_ref, k_ref, v_ref, z_ref, sc_ref, s0_ref, cv0_ref, cw_ref, alog_ref, dtb_ref, nw_ref,
                bd_ref, bd2_ref, tile_ref, tilet_ref, tri_ref, eb_ref, ea_ref, lvl_ref,
                y_ref, sn_ref, cvn_ref, st_scr, cx_scr, prev_scr, *, plan):
    j = pl.program_id(1)
    C = plan.chunk
    H = N_HEADS
    HC = H * C
    masks = _head_masks()
    bd = bd_ref[...]
    lane = lax.broadcasted_iota(jnp.int32, (1, D_GROUP), 1)
    beta_lanes = lane < H
    a_lanes = (lane >= H) & (lane < 2 * H)
    neg_rate = jnp.where(a_lanes, -jnp.exp(alog_ref[...]), 0.0)
    n_lvl = max(1, int(math.ceil(math.log2(C))))
    valid_last = (plan.length - 1) % C + 1
    assert valid_last >= GD_CONV - 1
    pad0 = SUBLANES - (GD_CONV - 1)

    @pl.when(j == 0)
    def _():
        def init(seqs):
            for s, st in zip(seqs, [_load_state_bd(s0_ref[s], tile_ref[...], bd) for s in seqs]):
                st_scr[s] = st
                prev_scr[s] = jnp.zeros((SUBLANES, 3 * D_GROUP), F32)
                prev_scr[s, pl.ds(pad0, GD_CONV - 1), :] = cv0_ref[s]
        plan.for_seq_groups(init)

    row_i = lax.broadcasted_iota(jnp.int32, (HC, HC), 0)
    col_i = lax.broadcasted_iota(jnp.int32, (HC, HC), 1)
    same_head = _div2(row_i, C) == _div2(col_i, C)
    incl = same_head & (col_i <= row_i)
    strict = same_head & (col_i < row_i)
    eye = (row_i == col_i).astype(F32)

    ones_row = jnp.ones((SUBLANES, EXACT_TERMS * D_GROUP), BF16)

    def conv_qkv(s, c):
        rows = pl.ds(pl.multiple_of(c * C, C), C)
        cx = cx_scr.at[s]
        cx[0:SUBLANES, :] = prev_scr[s]
        cx[SUBLANES:SUBLANES + C, 0:D_GROUP] = q_ref[s, rows, :]
        cx[SUBLANES:SUBLANES + C, D_GROUP:2 * D_GROUP] = k_ref[s, rows, :]
        cx[SUBLANES:SUBLANES + C, 2 * D_GROUP:3 * D_GROUP] = v_ref[s, rows, :]
        cw = cw_ref[...]
        conv = cx[pad0:pad0 + C, :] * cw[0:1]
        for tap in range(1, GD_CONV):
            conv = conv + cx[pad0 + tap:pad0 + tap + C, :] * cw[tap:tap + 1]
        prev_scr[s, pl.ds(pad0, GD_CONV - 1), :] = cx[SUBLANES + C - (GD_CONV - 1):SUBLANES + C, :]
        cvn_ref[s] = cx[SUBLANES + valid_last - (GD_CONV - 1):SUBLANES + valid_last, :]
        conv = _silu(conv)
        return conv[:, 0:D_GROUP], conv[:, D_GROUP:2 * D_GROUP], conv[:, 2 * D_GROUP:3 * D_GROUP]

    def gates(s, c, valid):
        sc = sc_ref[s, pl.ds(pl.multiple_of(c * C, C), C), :]
        beta_all = jnp.where(valid & beta_lanes, jax.nn.sigmoid(sc), 0.0)
        g_all = jnp.where(valid, neg_rate * jax.nn.softplus(sc + dtb_ref[...]), 0.0)
        return beta_all, g_all

    def decay_matrix(gc_all, gc_row):
        gc_col = jnp.concatenate([_lane_col(gc_all, H + h) for h in range(H)], axis=0)
        return jnp.exp(jnp.where(incl, gc_col - gc_row, -jnp.inf))

    def chunk(units):
        each = lambda f, *xs: [f(*a) for a in zip(*xs)]
        valid = [plan.valid_rows(j, c, C) for _, c in units]
        qkv = [conv_qkv(s, c) for s, c in units]
        q, k, v = ([t[i] for t in qkv] for i in range(3))
        qss = each(lambda x: _sel_r(x * x, bd2_ref[...], SUMSQ_TERMS), q)
        kss = each(lambda x: _sel_r(x * x, bd2_ref[...], SUMSQ_TERMS), k)
        q = each(lambda x, ss: x * lax.rsqrt(ss + NORM_EPS) * (HEAD_DIM ** -0.5), q, qss)
        k = each(lambda x, ss, ok: jnp.where(ok, x * lax.rsqrt(ss + NORM_EPS), 0.0), k, kss, valid)

        bg = [gates(s, c, ok) for (s, c), ok in zip(units, valid)]
        beta_all, g_all = [t[0] for t in bg], [t[1] for t in bg]
        gc_all = each(lambda g: _sel_l(tri_ref[...], g, EXACT_TERMS), g_all)
        beta_l = each(lambda b: _sel_r(b, eb_ref[...], EXACT_TERMS), beta_all)
        gc_l = each(lambda g: _sel_r(g, ea_ref[...], EXACT_TERMS), gc_all)
        gam_l = each(jnp.exp, gc_l)
        g_sel = each(lambda g: jnp.concatenate([jnp.where(lane == H + h, g, 0.0) for h in range(H)], axis=0), gc_all)
        gc_row = each(lambda g: _sel_nt(ones_row, g, EXACT_TERMS)[0:1], g_sel)
        dec = each(decay_matrix, gc_all, gc_row)
        beta_col = each(lambda b: jnp.concatenate([_lane_col(b, h) for h in range(H)], axis=0), beta_all)

        ks = each(lambda x: _stack_heads(x, masks), k)
        qs = each(lambda x: _stack_heads(x, masks), q)
        kk = each(_mm_nt, ks, ks)
        m = each(lambda d, b, x: jnp.where(strict, d, 0.0) * b * x, dec, beta_col, kk)
        t_inv = each(lambda x: eye - x * lvl_ref[0], m)
        for lvl in range(1, n_lvl):
            half = each(lambda t, x: _mms(t, x * lvl_ref[lvl]), t_inv, m)
            t_inv = each(lambda t, hf: t - _mms(hf, t), t_inv, half)
        rhs_w = each(lambda b, g, x: _stack_heads(b * g * x, masks), beta_l, gam_l, k)
        rhs_u = each(lambda b, x: _stack_heads(b * x, masks), beta_l, v)
        w = each(lambda t, r: _unstack_heads(_mms(t, r), H), t_inv, rhs_w)
        u0 = each(lambda t, r: _unstack_heads(_mms(t, r), H), t_inv, rhs_u)
        qk = each(lambda a, b, d: _mm_nt(a, b) * d, qs, ks, dec)

        st = [st_scr[s] for s, _ in units]
        u = each(lambda a, b, x: a - _mm(b, x), u0, w, st)
        o_st = each(lambda x, g, y: _mm(x * g, y), q, gam_l, st)
        o_in = each(lambda a, b: _unstack_heads(_mm(a, _stack_heads(b, masks)), H), qk, u)
        upd = each(lambda x, g, b: _mm_tn(x * jnp.exp(g[C - 1:C] - g), b) * bd, k, gc_l, u)
        for (s, _), g, x, d in zip(units, gc_l, st, upd):
            st_scr[s] = jnp.exp(g[C - 1:C]) * x + d

        o = each(jnp.add, o_st, o_in)
        ms = each(lambda x: _sel_r(x * x, bd2_ref[...], SUMSQ_TERMS) * (1.0 / HEAD_DIM), o)
        for (s, c), x, ss, ok in zip(units, o, ms, valid):
            rows = pl.ds(pl.multiple_of(c * C, C), C)
            x = x * lax.rsqrt(ss + NORM_EPS) * nw_ref[...] * _silu(z_ref[s, rows, :])
            y_ref[s, rows, :] = jnp.where(ok, x, 0.0)

    plan.for_unit_groups(C, chunk)

    @pl.when(j == plan.t_steps - 1)
    def _():
        def fin(seqs):
            for s, st in zip(seqs, [_store_state_bd(st_scr[s], tilet_ref[...]) for s in seqs]):
                sn_ref[s] = st
        plan.for_seq_groups(fin)


def _gdn(proj, s0, cv0, layer, lw, consts, plan):
    C = plan.chunk
    W3 = 3 * D_GROUP
    in_specs = [plan.rows_spec(D_GROUP, 5), plan.rows_spec(D_GROUP, 6), plan.rows_spec(D_GROUP, 7),
                plan.rows_spec(D_GROUP, 8), plan.rows_spec(D_GROUP, 11),
                plan.seq_spec(D_GROUP, HEAD_DIM, layer=layer), plan.seq_spec(GD_CONV - 1, W3, layer=layer),
                _const_spec((GD_CONV, W3)), _const_spec((1, D_GROUP)), _const_spec((1, D_GROUP)),
                _const_spec((1, D_GROUP))] + _selector_specs(C) + [
                _const_spec((EXACT_TERMS * D_GROUP, D_GROUP)), _const_spec((EXACT_TERMS * D_GROUP, D_GROUP)),
                _const_spec(consts["solve"][C].shape)]
    out_specs = [plan.rows_spec(D_GROUP), plan.seq_spec(D_GROUP, HEAD_DIM), plan.seq_spec(GD_CONV - 1, W3)]
    out_shape = [plan.rows_shape(),
                 jax.ShapeDtypeStruct((plan.n_seq, D_GROUP, HEAD_DIM), F32),
                 jax.ShapeDtypeStruct((plan.n_seq, GD_CONV - 1, W3), F32)]
    scratch = [pltpu.VMEM((plan.seq_blk, D_GROUP, D_GROUP), F32),
               pltpu.VMEM((plan.seq_blk, SUBLANES + C, W3), F32),
               pltpu.VMEM((plan.seq_blk, SUBLANES, W3), F32)]
    kern = functools.partial(_gdn_kernel, plan=plan)
    return _seq_call(kern, plan, in_specs, out_specs, out_shape, scratch, "gdn")(
        proj, proj, proj, proj, proj, s0, cv0, lw["conv_w"], lw["a_log"], lw["dt_bias"], lw["norm_w"],
        *_selectors(consts, C), consts["eb"], consts["ea"], consts["solve"][C])


def _swa_kernel(q_ref, kv_ref, ck_ref, cv_ref, bias_ref, sink_ref, y_ref, kn_ref, vn_ref,
                kk_scr, vv_scr, *, plan, has_cache):
    j = pl.program_id(1)
    QB = plan.chunk
    W = SW_WINDOW
    HK = SW_KV_HEADS * HEAD_DIM
    kv_masks = _head_masks(HK)
    valid_last = (plan.length - 1) % QB + 1

    @pl.when(j == 0)
    def _():
        kn_ref[...] = ck_ref[...]
        vn_ref[...] = cv_ref[...]

    def stacked_queries(s, c):
        rows = pl.ds(pl.multiple_of(c * QB, QB), QB)
        kk, vv = kk_scr.at[s], vv_scr.at[s]
        kk[0:W, :] = kn_ref[s]
        vv[0:W, :] = vn_ref[s]
        kv = kv_ref[s, rows, :]
        kk[W:W + QB, :] = kv[:, 0:HK]
        vv[W:W + QB, :] = kv[:, HK:2 * HK]
        q = q_ref[s, rows, :]
        return jnp.concatenate([q[:, 0:HK] * kv_masks[0], q[:, 0:HK] * kv_masks[1],
                                q[:, HK:2 * HK] * kv_masks[0], q[:, HK:2 * HK] * kv_masks[1]], axis=0)

    def probabilities(sc, c):
        sc = sc * (HEAD_DIM ** -0.5) + bias_ref[...]
        if not has_cache:
            t0 = j * plan.t_blk + c * QB
            col = lax.broadcasted_iota(jnp.int32, sc.shape, 1)
            sc = jnp.where(t0 + col < W, -jnp.inf, sc)
        sink = sink_ref[...]
        mx = jnp.maximum(jnp.max(sc, axis=-1, keepdims=True), sink)
        p = jnp.exp(sc - mx)
        denom = jnp.sum(p, axis=-1, keepdims=True) + jnp.exp(sink - mx)
        return p / denom

    def block(units):
        qs = [stacked_queries(s, c) for s, c in units]
        sc = [_mm_nt(x, kk_scr[s]) for x, (s, _) in zip(qs, units)]
        p = [probabilities(x, c) for x, (_, c) in zip(sc, units)]
        pv = [_mm(x, vv_scr[s]) for x, (s, _) in zip(p, units)]
        for x, (s, c) in zip(pv, units):
            oa = x[0:QB] * kv_masks[0] + x[QB:2 * QB] * kv_masks[1]
            ob = x[2 * QB:3 * QB] * kv_masks[0] + x[3 * QB:4 * QB] * kv_masks[1]
            rows = pl.ds(pl.multiple_of(c * QB, QB), QB)
            y_ref[s, rows, :] = jnp.where(plan.valid_rows(j, c, QB), jnp.concatenate([oa, ob], axis=-1), 0.0)
            kn_ref[s] = kk_scr[s, valid_last:valid_last + W, :]
            vn_ref[s] = vv_scr[s, valid_last:valid_last + W, :]

    plan.for_unit_groups(QB, block)


def _swa(proj, cache_k, cache_v, layer, bias, sink_col, plan, has_cache):
    QB = plan.chunk
    HK = SW_KV_HEADS * HEAD_DIM
    NK = SW_WINDOW + QB
    in_specs = [plan.rows_spec(D_GROUP, 9), plan.rows_spec(D_GROUP, 10),
                plan.seq_spec(SW_WINDOW, HK, layer=layer), plan.seq_spec(SW_WINDOW, HK, layer=layer),
                _const_spec((4 * QB, NK)), _const_spec((4 * QB, 1))]
    out_specs = [plan.rows_spec(D_GROUP), plan.seq_spec(SW_WINDOW, HK), plan.seq_spec(SW_WINDOW, HK)]
    out_shape = [plan.rows_shape(),
                 jax.ShapeDtypeStruct((plan.n_seq, SW_WINDOW, HK), F32),
                 jax.ShapeDtypeStruct((plan.n_seq, SW_WINDOW, HK), F32)]
    scratch = [pltpu.VMEM((plan.seq_blk, NK, HK), F32), pltpu.VMEM((plan.seq_blk, NK, HK), F32)]
    kern = functools.partial(_swa_kernel, plan=plan, has_cache=has_cache)
    return _seq_call(kern, plan, in_specs, out_specs, out_shape, scratch, "swa")(
        proj, proj, cache_k, cache_v, bias, sink_col)


_Q_HEAD_ORDER = (0, 2, 1, 3)


def _swa_bias(qb):
    w = SW_WINDOW
    i = np.arange(qb)[:, None]
    jj = np.arange(w + qb)[None, :]
    dist = w + i - jj
    ok = (dist >= 0) & (dist <= w)
    slopes = 2.0 ** (-8.0 * np.arange(1, N_HEADS + 1) / N_HEADS)
    blocks = []
    for g in range(2):
        for kv in range(SW_KV_HEADS):
            head = kv * 2 + g
            blocks.append(np.where(ok, -slopes[head] * dist, -np.inf))
    return jnp.asarray(np.concatenate(blocks, axis=0), F32)


def _constants(chunks):
    lane_head = np.arange(D_GROUP) // HEAD_DIM
    bd = (lane_head[:, None] == lane_head[None, :]).astype(np.float32)
    tile = np.tile(np.eye(HEAD_DIM, dtype=np.float32), (1, N_HEADS))
    eb = np.zeros((D_GROUP, D_GROUP), np.float32)
    ea = np.zeros((D_GROUP, D_GROUP), np.float32)
    for h in range(N_HEADS):
        eb[h, lane_head == h] = 1.0
        ea[N_HEADS + h, lane_head == h] = 1.0
    def solve_levels(c):
        blk = np.arange(N_HEADS * c) >> 1
        out = [blk[:, None] == blk[None, :]]
        while (1 << len(out)) < c:
            out.append(((blk >> 1)[:, None] == (blk >> 1)[None, :]) & (blk[:, None] != blk[None, :]))
            blk = blk >> 1
        return jnp.asarray(np.stack(out).astype(np.float32))

    rows = lambda m, n: jnp.asarray(np.tile(m, (n, 1)), BF16)
    cols = lambda m, n: jnp.asarray(np.tile(m, (1, n)), BF16)
    return {"bd": jnp.asarray(bd), "bd2": rows(bd, SUMSQ_TERMS),
            "tile": rows(tile, EXACT_TERMS), "tile_t": rows(tile.T, EXACT_TERMS),
            "eb": rows(eb, EXACT_TERMS), "ea": rows(ea, EXACT_TERMS),
            "tri": {c: cols(np.tril(np.ones((c, c), np.float32)), EXACT_TERMS) for c in chunks},
            "solve": {c: solve_levels(c) for c in chunks}}


def _s5_consts(lam_re, lam_im, log_dt, b_re, b_im, c_re, c_im, d, w1, w2):
    dt = jnp.exp(log_dt)[:, None]
    mag = jnp.exp(lam_re * dt)
    ang = lam_im * dt
    a_re, a_im = mag * jnp.cos(ang), mag * jnp.sin(ang)
    den = lam_re * lam_re + lam_im * lam_im
    z_re = ((a_re - 1.0) * lam_re + a_im * lam_im) / den
    z_im = (a_im * lam_re - (a_re - 1.0) * lam_im) / den
    bb_re = z_re[..., None] * b_re - z_im[..., None] * b_im
    bb_im = z_re[..., None] * b_im + z_im[..., None] * b_re
    eye = jnp.eye(SSM_GROUPS, dtype=F32)
    to_b = lambda t: jnp.einsum("gpc,gh->gchp", t, eye).reshape(D_GROUP, SSM_LANES)
    to_c = lambda t: jnp.einsum("gcp,gh->gphc", t, eye).reshape(SSM_LANES, D_GROUP)
    bmat = jnp.concatenate([to_b(bb_re), to_b(bb_im)], axis=1).astype(BF16)
    cmat = jnp.concatenate([to_c(c_re), -to_c(c_im)], axis=0).astype(BF16)
    a_row = jnp.concatenate([a_re.reshape(1, SSM_LANES), a_im.reshape(1, SSM_LANES)], axis=1)
    return (bmat, cmat, d.reshape(1, D_GROUP), w1.astype(BF16), w2.astype(BF16),
            jnp.broadcast_to(a_row, (SUBLANES, 2 * SSM_LANES)))


def _prep_w_in(w):
    q_d = w[:, :, 2312:2568].reshape(DEPTH, D_MODEL, N_HEADS, HEAD_DIM)[:, :, np.array(_Q_HEAD_ORDER)]
    scal = jnp.pad(w[:, :, 2304:2312], ((0, 0), (0, 0), (0, D_GROUP - 2 * N_HEADS)))
    return jnp.concatenate([w[:, :, 0:2304], q_d.reshape(DEPTH, D_MODEL, D_GROUP), w[:, :, 2568:2824], scal],
                           axis=2).astype(BF16)


def _prep_w_out(w):
    d_rows = w[:, 3 * D_GROUP:].reshape(DEPTH, N_HEADS, HEAD_DIM, D_MODEL)[:, np.array(_Q_HEAD_ORDER)]
    return jnp.concatenate([w[:, :3 * D_GROUP], d_rows.reshape(DEPTH, D_GROUP, D_MODEL)], axis=1).astype(BF16)


def _lane_vec(vals, first_lane):
    return jnp.zeros((1, D_GROUP), F32).at[0, first_lane:first_lane + vals.shape[0]].set(vals)


def _trunk(x, state, layers, big, consts, plans, tm, has_cache):
    plan, plan5, plan_w = plans["mix"], plans["s5"], plans["swa"]
    bias = _swa_bias(plan_w.chunk)
    outs = {k: [] for k in ("ssm", "hgrn", "gdn", "conv", "swa_k", "swa_v")}
    for l, lw in enumerate(layers):
        sink_col = jnp.repeat(lw["sinks"], plan_w.chunk).reshape(4 * plan_w.chunk, 1)
        gn = lw["gains"]
        x = _ffn(x, gn[0], gn[1], big["wg"], big["wu"], big["wd"], (l, 0), tm)
        proj = _inproj(x, gn[2], big["w_in"], (l,), tm).reshape(plan.n_seq, plan.l_pad, PROJ_W)
        sl = l if has_cache else 0
        y_tm, ssm = _s5(jnp.swapaxes(proj[:, :, 0:D_GROUP], 0, 1), state["ssm"], sl, lw["s5"], plan5)
        y_a = jnp.swapaxes(y_tm, 0, 1)
        y_b, hg = _hgrn(proj, state["hgrn"], sl, lw["lb"], lw["hgrn_nw"], consts, plan)
        y_c, gd, cv = _gdn(proj, state["gdn"], state["conv"], sl, lw["gdn"], consts, plan)
        y_d, ck, cvv = _swa(proj, state["swa_k"], state["swa_v"], sl, bias, sink_col, plan_w, has_cache)
        ys = tuple(y.reshape(x.shape[0], D_GROUP) for y in (y_a, y_b, y_c, y_d))
        x = _ffn(x, gn[4], gn[5], big["wg"], big["wu"], big["wd"], (l, 1), tm, mix=(ys, big["w_out"], gn[3]))
        for k, val in zip(outs, (ssm, hg, gd, cv, ck, cvv)):
            outs[k].append(val)
    return x, {k: jnp.stack(v, axis=0) for k, v in outs.items()}


def _finish_states(st, n_seq):
    ssm = st["ssm"].reshape(DEPTH, n_seq, 2, SSM_GROUPS, SSM_STATE)
    hg = jnp.swapaxes(st["hgrn"].reshape(DEPTH, n_seq, N_HEADS, HEAD_DIM, HEAD_DIM), -1, -2)
    gd = st["gdn"].reshape(DEPTH, n_seq, N_HEADS, HEAD_DIM, HEAD_DIM)
    sk = st["swa_k"].reshape(DEPTH, n_seq, SW_WINDOW, SW_KV_HEADS, HEAD_DIM)
    sv = st["swa_v"].reshape(DEPTH, n_seq, SW_WINDOW, SW_KV_HEADS, HEAD_DIM)
    return ssm[:, :, 0], ssm[:, :, 1], hg, gd, st["conv"], sk, sv


def kernel(x_prompt, x_sample, state_ssm_re, state_ssm_im, state_hgrn, state_gdn, state_gdn_conv,
           cache_swa_k, cache_swa_v, norm_gains, ffn_w_gate, ffn_w_up, ffn_w_down, w_in, w_out,
           ssm_lambda_re, ssm_lambda_im, ssm_log_dt, ssm_b_re, ssm_b_im, ssm_c_re, ssm_c_im, ssm_d,
           ssm_w_glu1, ssm_w_glu2, hgrn_lb_logits, hgrn_norm_w, gdn_conv_w, gdn_a_log, gdn_dt_bias,
           gdn_norm_w, swa_sinks):
    bp, lp, _ = x_prompt.shape
    bs, ls, _ = x_sample.shape
    ls_pad = -(-ls // SUBLANES) * SUBLANES
    plan_s = _SeqPlan(bs, ls, ls_pad, seq_blk=min(bs, 16), t_blk=ls_pad, chunk=ls_pad)
    plans_s = {"mix": plan_s, "s5": plan_s, "swa": plan_s}
    plans_p = {
        "mix": _SeqPlan(bp, lp, lp, seq_blk=min(bp, 4), t_blk=min(lp, 256), chunk=min(lp, 64)),
        "s5": _SeqPlan(bp, lp, lp, seq_blk=min(bp, SUBLANES), t_blk=min(lp, 128), chunk=min(lp, 64)),
        "swa": _SeqPlan(bp, lp, lp, seq_blk=min(bp, 4), t_blk=min(lp, 256), chunk=min(lp, SW_WINDOW))}
    consts = _constants({plans_p["mix"].chunk, plan_s.chunk})

    gam = jax.nn.softmax(hgrn_lb_logits.astype(F32), axis=0)
    lbs = jnp.cumsum(gam, axis=0) - gam[:1]
    big = {"wg": ffn_w_gate.astype(BF16), "wu": ffn_w_up.astype(BF16), "wd": ffn_w_down.astype(BF16),
           "w_in": _prep_w_in(w_in), "w_out": _prep_w_out(w_out)}
    layers = []
    for l in range(DEPTH):
        sinks = swa_sinks[l].astype(F32)[np.array(_Q_HEAD_ORDER)]
        layers.append({
            "gains": norm_gains[l].astype(F32).reshape(6, 1, D_MODEL),
            "s5": _s5_consts(ssm_lambda_re[l], ssm_lambda_im[l], ssm_log_dt[l], ssm_b_re[l], ssm_b_im[l],
                             ssm_c_re[l], ssm_c_im[l], ssm_d[l], ssm_w_glu1[l], ssm_w_glu2[l]),
            "lb": lbs[l].reshape(1, D_GROUP),
            "hgrn_nw": jnp.tile(hgrn_norm_w[l].astype(F32), N_HEADS).reshape(1, D_GROUP),
            "gdn": {"conv_w": gdn_conv_w[l].astype(F32),
                    "a_log": _lane_vec(gdn_a_log[l].astype(F32), N_HEADS),
                    "dt_bias": _lane_vec(gdn_dt_bias[l].astype(F32), N_HEADS),
                    "norm_w": jnp.tile(gdn_norm_w[l].astype(F32), N_HEADS).reshape(1, D_GROUP)},
            "sinks": sinks,
        })

    def run(x, n_seq, length, plans, state, has_cache):
        l_pad = plans["mix"].l_pad
        if l_pad > length:
            x = jnp.pad(x, ((0, 0), (0, l_pad - length), (0, 0)))
        x2 = x.reshape(n_seq * l_pad, D_MODEL)
        tm = min(512, x2.shape[0])
        y, st = _trunk(x2, state, layers, big, consts, plans, tm, has_cache)
        y = y.reshape(n_seq, l_pad, D_MODEL)[:, :length]
        return (y,) + _finish_states(st, n_seq)

    HK = SW_KV_HEADS * HEAD_DIM
    zeros = lambda *shape: jnp.zeros((1, bp) + shape, F32)
    prompt_state = {"ssm": zeros(2 * SSM_LANES), "hgrn": zeros(D_GROUP, HEAD_DIM),
                    "gdn": zeros(D_GROUP, HEAD_DIM), "conv": zeros(GD_CONV - 1, 3 * D_GROUP),
                    "swa_k": zeros(SW_WINDOW, HK), "swa_v": zeros(SW_WINDOW, HK)}
    sample_state = {
        "ssm": jnp.concatenate([state_ssm_re.reshape(DEPTH, bs, SSM_LANES),
                                state_ssm_im.reshape(DEPTH, bs, SSM_LANES)], axis=-1).astype(F32),
        "hgrn": jnp.swapaxes(state_hgrn.astype(F32), -1, -2).reshape(DEPTH, bs, D_GROUP, HEAD_DIM),
        "gdn": state_gdn.astype(F32).reshape(DEPTH, bs, D_GROUP, HEAD_DIM),
        "conv": state_gdn_conv.astype(F32),
        "swa_k": cache_swa_k.astype(F32).reshape(DEPTH, bs, SW_WINDOW, HK),
        "swa_v": cache_swa_v.astype(F32).reshape(DEPTH, bs, SW_WINDOW, HK)}

    yp, sre_p, sim_p, hg_p, gd_p, cv_p, sk_p, sv_p = run(x_prompt, bp, lp, plans_p, prompt_state, False)
    ys, sre_s, sim_s, hg_s, gd_s, cv_s, sk_s, sv_s = run(x_sample, bs, ls, plans_s, sample_state, True)
    return (yp, ys, sre_p, sre_s, sim_p, sim_s, hg_p, hg_s, gd_p, gd_s,
            cv_p, cv_s, sk_p, sk_s, sv_p, sv_s)
```

```python
import functools
import math

import numpy as np
import jax
import jax.numpy as jnp
from jax import lax
from jax.experimental import pallas as pl
from jax.experimental.pallas import tpu as pltpu

D_MODEL = 1024
DEPTH = 4
D_GROUP = 256
N_HEADS = 4
HEAD_DIM = 64
SSM_CH = 16
SSM_GROUPS = 16
SSM_STATE = 64
SSM_LANES = SSM_GROUPS * SSM_STATE
GD_CONV = 4
SW_WINDOW = 128
SW_KV_HEADS = 2
D_FF = 2816
NORM_EPS = 1e-6
PROJ_W = 12 * D_GROUP
SUBLANES = 8
LANES = 128
VMEM_LIMIT = 56 * 1024 * 1024

F32 = jnp.float32
BF16 = jnp.bfloat16
EXACT_TERMS = 3
SUMSQ_TERMS = 2


def _mm(a, b):
    return jnp.dot(a.astype(BF16), b.astype(BF16), preferred_element_type=F32)


def _mm_nt(a, b):
    return lax.dot_general(a.astype(BF16), b.astype(BF16), (((1,), (1,)), ((), ())),
                           preferred_element_type=F32)


def _mm_tn(a, b):
    return lax.dot_general(a.astype(BF16), b.astype(BF16), (((0,), (0,)), ((), ())),
                           preferred_element_type=F32)


def _mms(a, b):
    return _mm(a, b)


def _split_bf16(x, terms):
    parts = []
    for _ in range(terms - 1):
        p = x.astype(BF16)
        parts.append(p)
        x = x - p.astype(F32)
    parts.append(x.astype(BF16))
    return parts


def _sel_r(a, sel_stack, terms):
    return jnp.dot(jnp.concatenate(_split_bf16(a, terms), axis=1), sel_stack, preferred_element_type=F32)


def _sel_l(sel_stack, b, terms):
    return jnp.dot(sel_stack, jnp.concatenate(_split_bf16(b, terms), axis=0), preferred_element_type=F32)


def _sel_nt(sel_stack, b, terms):
    return lax.dot_general(sel_stack, jnp.concatenate(_split_bf16(b, terms), axis=1),
                           (((1,), (1,)), ((), ())), preferred_element_type=F32)


def _rms(x, gain):
    return x * lax.rsqrt(jnp.mean(x * x, axis=-1, keepdims=True) + NORM_EPS) * gain


def _silu(x):
    return x * jax.nn.sigmoid(x)


def _div2(x, d):
    assert d & (d - 1) == 0
    return x >> (d.bit_length() - 1)


def _mod2(x, d):
    assert d & (d - 1) == 0
    return x & (d - 1)


def _head_masks(width=D_GROUP, head_dim=HEAD_DIM):
    lane = lax.broadcasted_iota(jnp.int32, (1, width), 1)
    return [(_div2(lane, head_dim) == h).astype(F32) for h in range(width // head_dim)]


def _stack_heads(x, masks):
    return jnp.concatenate([x * m for m in masks], axis=0)


def _unstack_heads(x, n_heads):
    c = x.shape[0] // n_heads
    out = x[0:c]
    for h in range(1, n_heads):
        out = out + x[h * c:(h + 1) * c]
    return out


def _lane_col(x, lane):
    idx = lax.broadcasted_iota(jnp.int32, x.shape, 1)
    return jnp.sum(jnp.where(idx == lane, x, 0.0), axis=-1, keepdims=True)


def _const_spec(shape):
    return pl.BlockSpec(shape, lambda *_: (0,) * len(shape), pipeline_mode=pl.Buffered(1))


def _layer_spec(shape, lead):
    idx = tuple(lead) + (0,) * len(shape)
    return pl.BlockSpec((None,) * len(lead) + tuple(shape), lambda *_: idx, pipeline_mode=pl.Buffered(1))


def _params(*sem):
    return pltpu.CompilerParams(dimension_semantics=sem, vmem_limit_bytes=VMEM_LIMIT)


def _ffn_kernel(x_ref, gpre_ref, gpost_ref, wg_ref, wu_ref, wd_ref, *rest, mixed):
    x = x_ref[...]
    if mixed:
        ya_ref, yb_ref, yc_ref, yd_ref, wo_ref, gmix_ref, o_ref = rest
        y = jnp.concatenate([ya_ref[...], yb_ref[...], yc_ref[...], yd_ref[...]], axis=-1)
        x = x + _rms(jnp.dot(y.astype(BF16), wo_ref[...], preferred_element_type=F32), gmix_ref[...])
    else:
        o_ref, = rest
    h = _rms(x, gpre_ref[...]).astype(BF16)
    g = jnp.dot(h, wg_ref[...], preferred_element_type=F32)
    u = jnp.dot(h, wu_ref[...], preferred_element_type=F32)
    a = (_silu(g) * u).astype(BF16)
    y = jnp.dot(a, wd_ref[...], preferred_element_type=F32)
    o_ref[...] = x + 0.5 * _rms(y, gpost_ref[...])


def _ffn(x, gpre, gpost, wg, wu, wd, lead, tm, mix=None):
    n = x.shape[0]
    row = pl.BlockSpec((tm, D_MODEL), lambda i: (i, 0))
    in_specs = [row, _const_spec((1, D_MODEL)), _const_spec((1, D_MODEL)),
                _layer_spec((D_MODEL, D_FF), lead), _layer_spec((D_MODEL, D_FF), lead),
                _layer_spec((D_FF, D_MODEL), lead)]
    args = [x, gpre, gpost, wg, wu, wd]
    if mix is not None:
        ys, w_out, gmix = mix
        l_pad, n_seq, _ = ys[0].shape
        grp = pl.BlockSpec((tm, D_GROUP), lambda i: (i, 0))
        ya_spec = _time_major_spec(tm, l_pad)
        if ya_spec is None:
            ya, ya_spec = jnp.swapaxes(ys[0], 0, 1).reshape(n, D_GROUP), grp
        else:
            ya = ys[0].reshape(l_pad, n_seq * D_GROUP)
        in_specs += [ya_spec, grp, grp, grp, _layer_spec((D_MODEL, D_MODEL), lead[:1]), _const_spec((1, D_MODEL))]
        args += [ya, *ys[1:], w_out, gmix]
    return pl.pallas_call(
        functools.partial(_ffn_kernel, mixed=mix is not None),
        grid=(n // tm,),
        in_specs=in_specs,
        out_specs=row,
        out_shape=jax.ShapeDtypeStruct((n, D_MODEL), F32),
        compiler_params=_params("parallel"),
        name="ffn",
    )(*args)


def _time_major_spec(tm, l_pad):
    if l_pad % tm:
        return None
    per_seq = l_pad // tm
    return pl.BlockSpec((tm, D_GROUP), lambda i: (i % per_seq, i // per_seq))


def _inproj_kernel(x_ref, g_ref, w_ref, o_ref, *u_ref):
    h = _rms(x_ref[...], g_ref[...]).astype(BF16)
    proj = jnp.dot(h, w_ref[...], preferred_element_type=F32)
    o_ref[...] = proj
    if u_ref:
        u_ref[0][...] = proj[:, 0:D_GROUP]


def _inproj(x, gain, w, lead, tm, n_seq, l_pad):
    n = x.shape[0]
    out_specs = [pl.BlockSpec((tm, PROJ_W), lambda i: (i, 0))]
    out_shape = [jax.ShapeDtypeStruct((n, PROJ_W), F32)]
    u_spec = _time_major_spec(tm, l_pad)
    if u_spec is not None:
        out_specs.append(u_spec)
        out_shape.append(jax.ShapeDtypeStruct((l_pad, n_seq * D_GROUP), F32))
    outs = pl.pallas_call(
        _inproj_kernel,
        grid=(n // tm,),
        in_specs=[pl.BlockSpec((tm, D_MODEL), lambda i: (i, 0)), _const_spec((1, D_MODEL)),
                  _layer_spec((D_MODEL, PROJ_W), lead)],
        out_specs=out_specs,
        out_shape=out_shape,
        compiler_params=_params("parallel"),
        name="inproj",
    )(x, gain, w)
    proj = outs[0].reshape(n_seq, l_pad, PROJ_W)
    if u_spec is None:
        return proj, None
    return proj, outs[1].reshape(l_pad, n_seq, D_GROUP)


class _SeqPlan:
    def __init__(self, n_seq, length, l_pad, seq_blk, t_blk, chunk, unroll=4):
        assert l_pad % t_blk == 0 and t_blk % chunk == 0 and n_seq % seq_blk == 0
        assert chunk % SUBLANES == 0 and 0 <= l_pad - length < SUBLANES
        assert length % chunk == 0 or l_pad == chunk
        assert seq_blk & (seq_blk - 1) == 0
        self.n_seq, self.length, self.l_pad = n_seq, length, l_pad
        self.seq_blk, self.t_blk, self.chunk = seq_blk, t_blk, chunk
        self.rows = seq_blk * t_blk
        self.t_steps = l_pad // t_blk
        self.grid = (n_seq // seq_blk, self.t_steps)
        self.unroll = min(unroll, seq_blk)

    def rows_spec(self, width, col_block=0):
        return pl.BlockSpec((self.seq_blk, self.t_blk, width), lambda i, j: (i, j, col_block))

    def seq_spec(self, *tail, layer=None):
        zeros = (0,) * len(tail)
        if layer is None:
            return pl.BlockSpec((self.seq_blk,) + tail, lambda i, j: (i,) + zeros)
        return pl.BlockSpec((None, self.seq_blk) + tail, lambda i, j: (layer, i) + zeros)

    def rows_shape(self):
        return jax.ShapeDtypeStruct((self.n_seq, self.l_pad, D_GROUP), F32)

    def for_unit_groups(self, size, body):
        n_units = self.seq_blk * (self.t_blk // size)

        def step(n, carry):
            if self.unroll == self.seq_blk:
                body([(u, n) for u in range(self.unroll)])
            else:
                units = [n * self.unroll + u for u in range(self.unroll)]
                body([(_mod2(unit, self.seq_blk), _div2(unit, self.seq_blk)) for unit in units])
            return carry

        lax.fori_loop(0, n_units // self.unroll, step, 0)

    def for_seq_groups(self, body):
        def step(n, carry):
            body([n * self.unroll + u for u in range(self.unroll)])
            return carry
        lax.fori_loop(0, self.seq_blk // self.unroll, step, 0)

    def for_units(self, size, body):
        def group(units):
            for s, c in units:
                body(s, c)
        self.for_unit_groups(size, group)

    def valid_rows(self, j, c, size):
        t = j * self.t_blk + c * size + lax.broadcasted_iota(jnp.int32, (size, 1), 0)
        return t < self.length


def _seq_call(kernel, plan, in_specs, out_specs, out_shape, scratch, name):
    return pl.pallas_call(
        kernel, grid=plan.grid, in_specs=in_specs, out_specs=out_specs, out_shape=out_shape,
        scratch_shapes=scratch, compiler_params=_params("parallel", "arbitrary"), name=name)


def _s5_kernel(u_ref, h0_ref, bmat_ref, cmat_ref, d_ref, w1_ref, w2_ref, a_ref,
               y_ref, hn_ref, s_scr, carry_scr, *, plan):
    j = pl.program_id(1)
    P = SSM_LANES
    S = plan.seq_blk
    slab = min(SUBLANES, S)

    @pl.when(j == 0)
    def _():
        carry_scr[...] = h0_ref[...]

    u = u_ref[...].reshape(plan.rows, D_GROUP)
    s_scr[...] = _mm(u, bmat_ref[...])
    a_re, a_im = a_ref[0:slab, 0:P], a_ref[0:slab, P:2 * P]

    def step(t, x):
        out = []
        for g in range(S // slab):
            rows = pl.ds(pl.multiple_of(t * S + g * slab, slab), slab)
            x_re, x_im = x[2 * g], x[2 * g + 1]
            n_re = a_re * x_re - a_im * x_im + s_scr[rows, 0:P]
            n_im = a_re * x_im + a_im * x_re + s_scr[rows, P:2 * P]
            s_scr[rows, 0:P] = n_re
            s_scr[rows, P:2 * P] = n_im
            out += [n_re, n_im]
        return tuple(out)

    x0 = []
    for g in range(S // slab):
        x0 += [carry_scr[g * slab:(g + 1) * slab, 0:P], carry_scr[g * slab:(g + 1) * slab, P:2 * P]]
    x = lax.fori_loop(0, plan.t_blk, step, tuple(x0))
    for g in range(S // slab):
        carry_scr[g * slab:(g + 1) * slab, 0:P] = x[2 * g]
        carry_scr[g * slab:(g + 1) * slab, P:2 * P] = x[2 * g + 1]

    y = _mm(s_scr[...], cmat_ref[...]) + d_ref[...] * u
    y = jax.nn.gelu(y)
    y = _mm(y, w1_ref[...]) * jax.nn.sigmoid(_mm(y, w2_ref[...]))
    t = j * plan.t_blk + _div2(lax.broadcasted_iota(jnp.int32, (plan.rows, 1), 0), S)
    y_ref[...] = jnp.where(t < plan.length, y, 0.0).reshape(plan.t_blk, S, D_GROUP)

    @pl.when(j == plan.t_steps - 1)
    def _():
        hn_ref[...] = s_scr[((plan.length - 1) % plan.t_blk) * S:((plan.length - 1) % plan.t_blk + 1) * S, :]


def _s5(u_tm, h0, layer, consts, plan):
    P2 = 2 * SSM_LANES
    tm_spec = pl.BlockSpec((plan.t_blk, plan.seq_blk, D_GROUP), lambda i, j: (j, i, 0))
    in_specs = [tm_spec, plan.seq_spec(P2, layer=layer),
                _const_spec((D_GROUP, P2)), _const_spec((P2, D_GROUP)), _const_spec((1, D_GROUP)),
                _const_spec((D_GROUP, D_GROUP)), _const_spec((D_GROUP, D_GROUP)),
                _const_spec((SUBLANES, P2))]
    out_specs = [tm_spec, plan.seq_spec(P2)]
    out_shape = [jax.ShapeDtypeStruct((plan.l_pad, plan.n_seq, D_GROUP), F32),
                 jax.ShapeDtypeStruct((plan.n_seq, P2), F32)]
    scratch = [pltpu.VMEM((plan.rows, P2), F32), pltpu.VMEM((plan.seq_blk, P2), F32)]
    return _seq_call(functools.partial(_s5_kernel, plan=plan), plan, in_specs, out_specs, out_shape,
                     scratch, "s5")(u_tm, h0, *consts)


def _selectors(consts, chunk, transposed):
    arrays = (consts["bd"], consts["bd2"], consts["tile_tc" if transposed else "tile"], consts["tile_t"],
              consts["tri"][chunk])
    return arrays, [_const_spec(a.shape) for a in arrays]


def _load_state_bd(s0, tile_mat, bd_mask):
    return _sel_r(s0, tile_mat, EXACT_TERMS) * bd_mask


def _store_state_bd(s_bd, tile_mat_t):
    return _sel_r(s_bd, tile_mat_t, EXACT_TERMS)


def _load_state_bd_t(s0, tile_mat_tc, bd_mask):
    return _sel_nt(tile_mat_tc, s0, EXACT_TERMS) * bd_mask


def _store_state_bd_t(st_bd, tile_mat_t):
    parts = jnp.concatenate(_split_bf16(st_bd, EXACT_TERMS), axis=0)
    return lax.dot_general(parts, tile_mat_t, (((0,), (0,)), ((), ())), preferred_element_type=F32)


def _hgrn_kernel(q_ref, f_ref, i_ref, g_ref, s0_ref, lb_ref, nw_ref,
                 bd_ref, bd2_ref, tile_ref, tilet_ref, tri_ref,
                 y_ref, sn_ref, st_scr, *, plan, sub):
    j = pl.program_id(1)
    C = plan.chunk
    masks = _head_masks()
    bd = bd_ref[...]
    lb = lb_ref[...]

    @pl.when(j == 0)
    def _():
        def init(seqs):
            for s, st in zip(seqs, [_load_state_bd_t(s0_ref[s], tile_ref[...], bd) for s in seqs]):
                st_scr[s] = st
        plan.for_seq_groups(init)

    stacked_mask = jnp.concatenate([jnp.broadcast_to(m, (sub, D_GROUP)) for m in masks], axis=0)

    def gated_inputs(s, c, valid):
        rows = pl.ds(pl.multiple_of(c * C, C), C)
        fr = f_ref[s, rows, :]
        f = lb + (1.0 - lb) * jax.nn.sigmoid(fr)
        k = jnp.where(valid, (1.0 - lb) * jax.nn.sigmoid(-fr), 0.0)
        logf = jnp.where(valid, jnp.log(f), 0.0)
        return _silu(q_ref[s, rows, :]), k, i_ref[s, rows, :], logf

    def scores(q, k, cum, blk):
        r0, r1 = blk * sub, (blk + 1) * sub
        c0 = cum[r0 - 1:r0] if blk else jnp.zeros_like(cum[0:1])
        qs = _stack_heads(q[r0:r1] * jnp.exp(cum[r0:r1] - c0), masks)
        att = _mm_nt(qs, k[0:r1] * jnp.exp(c0 - cum[0:r1]))
        t_idx = r0 + _mod2(lax.broadcasted_iota(jnp.int32, att.shape, 0), sub)
        s_idx = lax.broadcasted_iota(jnp.int32, att.shape, 1)
        return jnp.where(s_idx <= t_idx, att, 0.0)

    def chunk(units):
        each = lambda fn, *xs: [fn(*a) for a in zip(*xs)]
        valid = [plan.valid_rows(j, c, C) for _, c in units]
        qkvl = [gated_inputs(s, c, ok) for (s, c), ok in zip(units, valid)]
        q, k, v, logf = ([t[i] for t in qkvl] for i in range(4))
        cum = each(lambda x: _sel_l(tri_ref[...], x, EXACT_TERMS), logf)
        st = [st_scr[s] for s, _ in units]
        o = each(lambda a, b, x: _mm_nt(a * jnp.exp(b), x), q, cum, st)
        intra = [[] for _ in units]
        for blk in range(C // sub):
            att = each(lambda a, b, d: scores(a, b, d, blk), q, k, cum)
            pv = each(lambda a, x: _mm(a, x[0:(blk + 1) * sub]), att, v)
            for lst, x in zip(intra, pv):
                lst.append(_unstack_heads(x * stacked_mask, N_HEADS))
        o = each(lambda a, lst: a + jnp.concatenate(lst, axis=0), o, intra)
        upd = each(lambda x, b, d: _mm_tn(x, b * jnp.exp(d[C - 1:C] - d)) * bd, v, k, cum)
        for (s, _), x, d, u in zip(units, st, cum, upd):
            st_scr[s] = x * jnp.exp(d[C - 1:C]) + u
        ms = each(lambda x: _sel_r(x * x, bd2_ref[...], SUMSQ_TERMS) * (1.0 / HEAD_DIM), o)
        for (s, c), x, ss, ok in zip(units, o, ms, valid):
            rows = pl.ds(pl.multiple_of(c * C, C), C)
            x = x * lax.rsqrt(ss + NORM_EPS) * nw_ref[...] * _silu(g_ref[s, rows, :])
            y_ref[s, rows, :] = jnp.where(ok, x, 0.0)

    plan.for_unit_groups(C, chunk)

    @pl.when(j == plan.t_steps - 1)
    def _():
        def fin(seqs):
            for s, st in zip(seqs, [_store_state_bd_t(st_scr[s], tilet_ref[...]) for s in seqs]):
                sn_ref[s] = st
        plan.for_seq_groups(fin)


def _hgrn(proj, s0, layer, lb, norm_w, consts, plan):
    C = plan.chunk
    sub = min(16, C)
    selectors, selector_specs = _selectors(consts, C, transposed=True)
    in_specs = [plan.rows_spec(D_GROUP, 1), plan.rows_spec(D_GROUP, 2), plan.rows_spec(D_GROUP, 3),
                plan.rows_spec(D_GROUP, 4), plan.seq_spec(D_GROUP, HEAD_DIM, layer=layer),
                _const_spec((1, D_GROUP)), _const_spec((1, D_GROUP))] + selector_specs
    out_specs = [plan.rows_spec(D_GROUP), plan.seq_spec(D_GROUP, HEAD_DIM)]
    out_shape = [plan.rows_shape(), jax.ShapeDtypeStruct((plan.n_seq, D_GROUP, HEAD_DIM), F32)]
    scratch = [pltpu.VMEM((plan.seq_blk, D_GROUP, D_GROUP), F32)]
    kern = functools.partial(_hgrn_kernel, plan=plan, sub=sub)
    return _seq_call(kern, plan, in_specs, out_specs, out_shape, scratch, "hgrn2")(
        proj, proj, proj, proj, s0, lb, norm_w, *selectors)


def _gdn_kernel(q_ref, k_ref, v_ref, z_ref, sc_ref, s0_ref, cv0_ref, cw_ref, alog_ref, dtb_ref, nw_ref,
                bd_ref, bd2_ref, tile_ref, tilet_ref, tri_ref, eb_ref, ea_ref, lvl_ref,
                y_ref, sn_ref, cvn_ref, st_scr, cx_scr, prev_scr, *, plan):
    j = pl.program_id(1)
    C = plan.chunk
    H = N_HEADS
    HC = H * C
    masks = _head_masks()
    bd = bd_ref[...]
    lane = lax.broadcasted_iota(jnp.int32, (1, D_GROUP), 1)
    beta_lanes = lane < H
    a_lanes = (lane >= H) & (lane < 2 * H)
    neg_rate = jnp.where(a_lanes, -jnp.exp(alog_ref[...]), 0.0)
    n_lvl = max(1, int(math.ceil(math.log2(C))))
    valid_last = (plan.length - 1) % C + 1
    assert valid_last >= GD_CONV - 1
    pad0 = SUBLANES - (GD_CONV - 1)

    @pl.when(j == 0)
    def _():
        def init(seqs):
            for s, st in zip(seqs, [_load_state_bd(s0_ref[s], tile_ref[...], bd) for s in seqs]):
                st_scr[s] = st
                prev_scr[s] = jnp.zeros((SUBLANES, 3 * D_GROUP), F32)
                prev_scr[s, pl.ds(pad0, GD_CONV - 1), :] = cv0_ref[s]
        plan.for_seq_groups(init)

    row_i = lax.broadcasted_iota(jnp.int32, (HC, HC), 0)
    col_i = lax.broadcasted_iota(jnp.int32, (HC, HC), 1)
    same_head = _div2(row_i, C) == _div2(col_i, C)
    incl = same_head & (col_i <= row_i)
    strict = same_head & (col_i < row_i)
    eye = (row_i == col_i).astype(F32)

    ones_row = jnp.ones((SUBLANES, EXACT_TERMS * D_GROUP), BF16)

    def conv_qkv(s, c):
        rows = pl.ds(pl.multiple_of(c * C, C), C)
        cx = cx_scr.at[s]
        cx[0:SUBLANES, :] = prev_scr[s]
        cx[SUBLANES:SUBLANES + C, 0:D_GROUP] = q_ref[s, rows, :]
        cx[SUBLANES:SUBLANES + C, D_GROUP:2 * D_GROUP] = k_ref[s, rows, :]
        cx[SUBLANES:SUBLANES + C, 2 * D_GROUP:3 * D_GROUP] = v_ref[s, rows, :]
        cw = cw_ref[...]
        conv = cx[pad0:pad0 + C, :] * cw[0:1]
        for tap in range(1, GD_CONV):
            conv = conv + cx[pad0 + tap:pad0 + tap + C, :] * cw[tap:tap + 1]
        prev_scr[s, pl.ds(pad0, GD_CONV - 1), :] = cx[SUBLANES + C - (GD_CONV - 1):SUBLANES + C, :]
        cvn_ref[s] = cx[SUBLANES + valid_last - (GD_CONV - 1):SUBLANES + valid_last, :]
        conv = _silu(conv)
        return conv[:, 0:D_GROUP], conv[:, D_GROUP:2 * D_GROUP], conv[:, 2 * D_GROUP:3 * D_GROUP]

    def gates(s, c, valid):
        sc = sc_ref[s, pl.ds(pl.multiple_of(c * C, C), C), :]
        beta_all = jnp.where(valid & beta_lanes, jax.nn.sigmoid(sc), 0.0)
        g_all = jnp.where(valid, neg_rate * jax.nn.softplus(sc + dtb_ref[...]), 0.0)
        return beta_all, g_all

    def decay_matrix(gc_all, gc_row):
        gc_col = jnp.concatenate([_lane_col(gc_all, H + h) for h in range(H)], axis=0)
        return jnp.exp(jnp.where(incl, gc_col - gc_row, -jnp.inf))

    def chunk(units):
        each = lambda f, *xs: [f(*a) for a in zip(*xs)]
        valid = [plan.valid_rows(j, c, C) for _, c in units]
        qkv = [conv_qkv(s, c) for s, c in units]
        q, k, v = ([t[i] for t in qkv] for i in range(3))
        qss = each(lambda x: _sel_r(x * x, bd2_ref[...], SUMSQ_TERMS), q)
        kss = each(lambda x: _sel_r(x * x, bd2_ref[...], SUMSQ_TERMS), k)
        q = each(lambda x, ss: x * lax.rsqrt(ss + NORM_EPS) * (HEAD_DIM ** -0.5), q, qss)
        k = each(lambda x, ss, ok: jnp.where(ok, x * lax.rsqrt(ss + NORM_EPS), 0.0), k, kss, valid)

        bg = [gates(s, c, ok) for (s, c), ok in zip(units, valid)]
        beta_all, g_all = [t[0] for t in bg], [t[1] for t in bg]
        gc_all = each(lambda g: _sel_l(tri_ref[...], g, EXACT_TERMS), g_all)
        beta_l = each(lambda b: _sel_r(b, eb_ref[...], EXACT_TERMS), beta_all)
        gc_l = each(lambda g: _sel_r(g, ea_ref[...], EXACT_TERMS), gc_all)
        gam_l = each(jnp.exp, gc_l)
        g_sel = each(lambda g: jnp.concatenate([jnp.where(lane == H + h, g, 0.0) for h in range(H)], axis=0), gc_all)
        gc_row = each(lambda g: _sel_nt(ones_row, g, EXACT_TERMS)[0:1], g_sel)
        dec = each(decay_matrix, gc_all, gc_row)
        beta_col = each(lambda b: jnp.concatenate([_lane_col(b, h) for h in range(H)], axis=0), beta_all)

        ks = each(lambda x: _stack_heads(x, masks), k)
        qs = each(lambda x: _stack_heads(x, masks), q)
        kk = each(_mm_nt, ks, ks)
        m = each(lambda d, b, x: jnp.where(strict, d, 0.0) * b * x, dec, beta_col, kk)
        t_inv = each(lambda x: eye - x * lvl_ref[0], m)
        for lvl in range(1, n_lvl):
            half = each(lambda t, x: _mms(t, x * lvl_ref[lvl]), t_inv, m)
            t_inv = each(lambda t, hf: t - _mms(hf, t), t_inv, half)
        rhs_w = each(lambda b, g, x: _stack_heads(b * g * x, masks), beta_l, gam_l, k)
        rhs_u = each(lambda b, x: _stack_heads(b * x, masks), beta_l, v)
        w = each(lambda t, r: _unstack_heads(_mms(t, r), H), t_inv, rhs_w)
        u0 = each(lambda t, r: _unstack_heads(_mms(t, r), H), t_inv, rhs_u)
        qk = each(lambda a, b, d: _mm_nt(a, b) * d, qs, ks, dec)

        st = [st_scr[s] for s, _ in units]
        u = each(lambda a, b, x: a - _mm(b, x), u0, w, st)
        o_st = each(lambda x, g, y: _mm(x * g, y), q, gam_l, st)
        o_in = each(lambda a, b: _unstack_heads(_mm(a, _stack_heads(b, masks)), H), qk, u)
        upd = each(lambda x, g, b: _mm_tn(x * jnp.exp(g[C - 1:C] - g), b) * bd, k, gc_l, u)
        for (s, _), g, x, d in zip(units, gc_l, st, upd):
            st_scr[s] = jnp.exp(g[C - 1:C]) * x + d

        o = each(jnp.add, o_st, o_in)
        ms = each(lambda x: _sel_r(x * x, bd2_ref[...], SUMSQ_TERMS) * (1.0 / HEAD_DIM), o)
        for (s, c), x, ss, ok in zip(units, o, ms, valid):
            rows = pl.ds(pl.multiple_of(c * C, C), C)
            x = x * lax.rsqrt(ss + NORM_EPS) * nw_ref[...] * _silu(z_ref[s, rows, :])
            y_ref[s, rows, :] = jnp.where(ok, x, 0.0)

    plan.for_unit_groups(C, chunk)

    @pl.when(j == plan.t_steps - 1)
    def _():
        def fin(seqs):
            for s, st in zip(seqs, [_store_state_bd(st_scr[s], tilet_ref[...]) for s in seqs]):
                sn_ref[s] = st
        plan.for_seq_groups(fin)


def _gdn(proj, s0, cv0, layer, lw, consts, plan):
    C = plan.chunk
    W3 = 3 * D_GROUP
    in_specs = [plan.rows_spec(D_GROUP, 5), plan.rows_spec(D_GROUP, 6), plan.rows_spec(D_GROUP, 7),
                plan.rows_spec(D_GROUP, 8), plan.rows_spec(D_GROUP, 11),
                plan.seq_spec(D_GROUP, HEAD_DIM, layer=layer), plan.seq_spec(GD_CONV - 1, W3, layer=layer),
                _const_spec((GD_CONV, W3)), _const_spec((1, D_GROUP)), _const_spec((1, D_GROUP)),
                _const_spec((1, D_GROUP))] + _selectors(consts, C, transposed=False)[1] + [
                _const_spec((EXACT_TERMS * D_GROUP, D_GROUP)), _const_spec((EXACT_TERMS * D_GROUP, D_GROUP)),
                _const_spec(consts["solve"][C].shape)]
    out_specs = [plan.rows_spec(D_GROUP), plan.seq_spec(D_GROUP, HEAD_DIM), plan.seq_spec(GD_CONV - 1, W3)]
    out_shape = [plan.rows_shape(),
                 jax.ShapeDtypeStruct((plan.n_seq, D_GROUP, HEAD_DIM), F32),
                 jax.ShapeDtypeStruct((plan.n_seq, GD_CONV - 1, W3), F32)]
    scratch = [pltpu.VMEM((plan.seq_blk, D_GROUP, D_GROUP), F32),
               pltpu.VMEM((plan.seq_blk, SUBLANES + C, W3), F32),
               pltpu.VMEM((plan.seq_blk, SUBLANES, W3), F32)]
    kern = functools.partial(_gdn_kernel, plan=plan)
    return _seq_call(kern, plan, in_specs, out_specs, out_shape, scratch, "gdn")(
        proj, proj, proj, proj, proj, s0, cv0, lw["conv_w"], lw["a_log"], lw["dt_bias"], lw["norm_w"],
        *_selectors(consts, C, transposed=False)[0], consts["eb"], consts["ea"], consts["solve"][C])


def _swa_kernel(q_ref, kv_ref, ck_ref, cv_ref, bias_ref, sink_ref, y_ref, kn_ref, vn_ref,
                kk_scr, vv_scr, *, plan, has_cache):
    j = pl.program_id(1)
    QB = plan.chunk
    W = SW_WINDOW
    HK = SW_KV_HEADS * HEAD_DIM
    kv_masks = _head_masks(HK)
    valid_last = (plan.length - 1) % QB + 1

    @pl.when(j == 0)
    def _():
        kn_ref[...] = ck_ref[...]
        vn_ref[...] = cv_ref[...]

    def stacked_queries(s, c):
        rows = pl.ds(pl.multiple_of(c * QB, QB), QB)
        kk, vv = kk_scr.at[s], vv_scr.at[s]
        kk[0:W, :] = kn_ref[s]
        vv[0:W, :] = vn_ref[s]
        kv = kv_ref[s, rows, :]
        kk[W:W + QB, :] = kv[:, 0:HK]
        vv[W:W + QB, :] = kv[:, HK:2 * HK]
        q = q_ref[s, rows, :]
        return jnp.concatenate([q[:, 0:HK] * kv_masks[0], q[:, 0:HK] * kv_masks[1],
                                q[:, HK:2 * HK] * kv_masks[0], q[:, HK:2 * HK] * kv_masks[1]], axis=0)

    def probabilities(sc, c):
        sc = sc * (HEAD_DIM ** -0.5) + bias_ref[...]
        if not has_cache:
            t0 = j * plan.t_blk + c * QB
            col = lax.broadcasted_iota(jnp.int32, sc.shape, 1)
            sc = jnp.where(t0 + col < W, -jnp.inf, sc)
        sink = sink_ref[...]
        mx = jnp.maximum(jnp.max(sc, axis=-1, keepdims=True), sink)
        p = jnp.exp(sc - mx)
        denom = jnp.sum(p, axis=-1, keepdims=True) + jnp.exp(sink - mx)
        return p / denom

    def block(units):
        qs = [stacked_queries(s, c) for s, c in units]
        sc = [_mm_nt(x, kk_scr[s]) for x, (s, _) in zip(qs, units)]
        p = [probabilities(x, c) for x, (_, c) in zip(sc, units)]
        pv = [_mm(x, vv_scr[s]) for x, (s, _) in zip(p, units)]
        for x, (s, c) in zip(pv, units):
            oa = x[0:QB] * kv_masks[0] + x[QB:2 * QB] * kv_masks[1]
            ob = x[2 * QB:3 * QB] * kv_masks[0] + x[3 * QB:4 * QB] * kv_masks[1]
            rows = pl.ds(pl.multiple_of(c * QB, QB), QB)
            y_ref[s, rows, :] = jnp.where(plan.valid_rows(j, c, QB), jnp.concatenate([oa, ob], axis=-1), 0.0)
            kn_ref[s] = kk_scr[s, valid_last:valid_last + W, :]
            vn_ref[s] = vv_scr[s, valid_last:valid_last + W, :]

    plan.for_unit_groups(QB, block)


def _swa(proj, cache_k, cache_v, layer, bias, sink_col, plan, has_cache):
    QB = plan.chunk
    HK = SW_KV_HEADS * HEAD_DIM
    NK = SW_WINDOW + QB
    in_specs = [plan.rows_spec(D_GROUP, 9), plan.rows_spec(D_GROUP, 10),
                plan.seq_spec(SW_WINDOW, HK, layer=layer), plan.seq_spec(SW_WINDOW, HK, layer=layer),
                _const_spec((4 * QB, NK)), _const_spec((4 * QB, 1))]
    out_specs = [plan.rows_spec(D_GROUP), plan.seq_spec(SW_WINDOW, HK), plan.seq_spec(SW_WINDOW, HK)]
    out_shape = [plan.rows_shape(),
                 jax.ShapeDtypeStruct((plan.n_seq, SW_WINDOW, HK), F32),
                 jax.ShapeDtypeStruct((plan.n_seq, SW_WINDOW, HK), F32)]
    scratch = [pltpu.VMEM((plan.seq_blk, NK, HK), F32), pltpu.VMEM((plan.seq_blk, NK, HK), F32)]
    kern = functools.partial(_swa_kernel, plan=plan, has_cache=has_cache)
    return _seq_call(kern, plan, in_specs, out_specs, out_shape, scratch, "swa")(
        proj, proj, cache_k, cache_v, bias, sink_col)


_Q_HEAD_ORDER = (0, 2, 1, 3)


def _swa_bias(qb):
    w = SW_WINDOW
    i = np.arange(qb)[:, None]
    jj = np.arange(w + qb)[None, :]
    dist = w + i - jj
    ok = (dist >= 0) & (dist <= w)
    slopes = 2.0 ** (-8.0 * np.arange(1, N_HEADS + 1) / N_HEADS)
    blocks = []
    for g in range(2):
        for kv in range(SW_KV_HEADS):
            head = kv * 2 + g
            blocks.append(np.where(ok, -slopes[head] * dist, -np.inf))
    return jnp.asarray(np.concatenate(blocks, axis=0), F32)


def _constants(chunks):
    lane_head = np.arange(D_GROUP) // HEAD_DIM
    bd = (lane_head[:, None] == lane_head[None, :]).astype(np.float32)
    tile = np.tile(np.eye(HEAD_DIM, dtype=np.float32), (1, N_HEADS))
    eb = np.zeros((D_GROUP, D_GROUP), np.float32)
    ea = np.zeros((D_GROUP, D_GROUP), np.float32)
    for h in range(N_HEADS):
        eb[h, lane_head == h] = 1.0
        ea[N_HEADS + h, lane_head == h] = 1.0
    def solve_levels(c):
        blk = np.arange(N_HEADS * c) >> 1
        out = [blk[:, None] == blk[None, :]]
        while (1 << len(out)) < c:
            out.append(((blk >> 1)[:, None] == (blk >> 1)[None, :]) & (blk[:, None] != blk[None, :]))
            blk = blk >> 1
        return jnp.asarray(np.stack(out).astype(np.float32))

    rows = lambda m, n: jnp.asarray(np.tile(m, (n, 1)), BF16)
    cols = lambda m, n: jnp.asarray(np.tile(m, (1, n)), BF16)
    return {"bd": jnp.asarray(bd), "bd2": rows(bd, SUMSQ_TERMS),
            "tile": rows(tile, EXACT_TERMS), "tile_t": rows(tile.T, EXACT_TERMS), "tile_tc": cols(tile.T, EXACT_TERMS),
            "eb": rows(eb, EXACT_TERMS), "ea": rows(ea, EXACT_TERMS),
            "tri": {c: cols(np.tril(np.ones((c, c), np.float32)), EXACT_TERMS) for c in chunks},
            "solve": {c: solve_levels(c) for c in chunks}}


def _s5_consts(lam_re, lam_im, log_dt, b_re, b_im, c_re, c_im, d, w1, w2):
    dt = jnp.exp(log_dt)[:, None]
    mag = jnp.exp(lam_re * dt)
    ang = lam_im * dt
    a_re, a_im = mag * jnp.cos(ang), mag * jnp.sin(ang)
    den = lam_re * lam_re + lam_im * lam_im
    z_re = ((a_re - 1.0) * lam_re + a_im * lam_im) / den
    z_im = (a_im * lam_re - (a_re - 1.0) * lam_im) / den
    bb_re = z_re[..., None] * b_re - z_im[..., None] * b_im
    bb_im = z_re[..., None] * b_im + z_im[..., None] * b_re
    eye = jnp.eye(SSM_GROUPS, dtype=F32)
    to_b = lambda t: jnp.einsum("gpc,gh->gchp", t, eye).reshape(D_GROUP, SSM_LANES)
    to_c = lambda t: jnp.einsum("gcp,gh->gphc", t, eye).reshape(SSM_LANES, D_GROUP)
    bmat = jnp.concatenate([to_b(bb_re), to_b(bb_im)], axis=1).astype(BF16)
    cmat = jnp.concatenate([to_c(c_re), -to_c(c_im)], axis=0).astype(BF16)
    a_row = jnp.concatenate([a_re.reshape(1, SSM_LANES), a_im.reshape(1, SSM_LANES)], axis=1)
    return (bmat, cmat, d.reshape(1, D_GROUP), w1.astype(BF16), w2.astype(BF16),
            jnp.broadcast_to(a_row, (SUBLANES, 2 * SSM_LANES)))


def _prep_w_in(w):
    q_d = w[:, :, 2312:2568].reshape(DEPTH, D_MODEL, N_HEADS, HEAD_DIM)[:, :, np.array(_Q_HEAD_ORDER)]
    scal = jnp.pad(w[:, :, 2304:2312], ((0, 0), (0, 0), (0, D_GROUP - 2 * N_HEADS)))
    return jnp.concatenate([w[:, :, 0:2304], q_d.reshape(DEPTH, D_MODEL, D_GROUP), w[:, :, 2568:2824], scal],
                           axis=2).astype(BF16)


def _prep_w_out(w):
    d_rows = w[:, 3 * D_GROUP:].reshape(DEPTH, N_HEADS, HEAD_DIM, D_MODEL)[:, np.array(_Q_HEAD_ORDER)]
    return jnp.concatenate([w[:, :3 * D_GROUP], d_rows.reshape(DEPTH, D_GROUP, D_MODEL)], axis=1).astype(BF16)


def _lane_vec(vals, first_lane):
    return jnp.zeros((1, D_GROUP), F32).at[0, first_lane:first_lane + vals.shape[0]].set(vals)


def _trunk(x, state, layers, big, consts, plans, tm, has_cache):
    plan, plan5, plan_w = plans["mix"], plans["s5"], plans["swa"]
    bias = _swa_bias(plan_w.chunk)
    outs = {k: [] for k in ("ssm", "hgrn", "gdn", "conv", "swa_k", "swa_v")}
    for l, lw in enumerate(layers):
        sink_col = jnp.repeat(lw["sinks"], plan_w.chunk).reshape(4 * plan_w.chunk, 1)
        gn = lw["gains"]
        x = _ffn(x, gn[0], gn[1], big["wg"], big["wu"], big["wd"], (l, 0), tm)
        proj, u_tm = _inproj(x, gn[2], big["w_in"], (l,), tm, plan.n_seq, plan.l_pad)
        if u_tm is None:
            u_tm = jnp.swapaxes(proj[:, :, 0:D_GROUP], 0, 1)
        sl = l if has_cache else 0
        y_tm, ssm = _s5(u_tm, state["ssm"], sl, lw["s5"], plan5)
        y_b, hg = _hgrn(proj, state["hgrn"], sl, lw["lb"], lw["hgrn_nw"], consts, plan)
        y_c, gd, cv = _gdn(proj, state["gdn"], state["conv"], sl, lw["gdn"], consts, plan)
        y_d, ck, cvv = _swa(proj, state["swa_k"], state["swa_v"], sl, bias, sink_col, plan_w, has_cache)
        ys = (y_tm,) + tuple(y.reshape(x.shape[0], D_GROUP) for y in (y_b, y_c, y_d))
        x = _ffn(x, gn[4], gn[5], big["wg"], big["wu"], big["wd"], (l, 1), tm, mix=(ys, big["w_out"], gn[3]))
        for k, val in zip(outs, (ssm, hg, gd, cv, ck, cvv)):
            outs[k].append(val)
    return x, {k: jnp.stack(v, axis=0) for k, v in outs.items()}


def _finish_states(st, n_seq):
    ssm = st["ssm"].reshape(DEPTH, n_seq, 2, SSM_GROUPS, SSM_STATE)
    hg = st["hgrn"].reshape(DEPTH, n_seq, N_HEADS, HEAD_DIM, HEAD_DIM)
    gd = st["gdn"].reshape(DEPTH, n_seq, N_HEADS, HEAD_DIM, HEAD_DIM)
    sk = st["swa_k"].reshape(DEPTH, n_seq, SW_WINDOW, SW_KV_HEADS, HEAD_DIM)
    sv = st["swa_v"].reshape(DEPTH, n_seq, SW_WINDOW, SW_KV_HEADS, HEAD_DIM)
    return ssm[:, :, 0], ssm[:, :, 1], hg, gd, st["conv"], sk, sv


def kernel(x_prompt, x_sample, state_ssm_re, state_ssm_im, state_hgrn, state_gdn, state_gdn_conv,
           cache_swa_k, cache_swa_v, norm_gains, ffn_w_gate, ffn_w_up, ffn_w_down, w_in, w_out,
           ssm_lambda_re, ssm_lambda_im, ssm_log_dt, ssm_b_re, ssm_b_im, ssm_c_re, ssm_c_im, ssm_d,
           ssm_w_glu1, ssm_w_glu2, hgrn_lb_logits, hgrn_norm_w, gdn_conv_w, gdn_a_log, gdn_dt_bias,
           gdn_norm_w, swa_sinks):
    bp, lp, _ = x_prompt.shape
    bs, ls, _ = x_sample.shape
    ls_pad = -(-ls // SUBLANES) * SUBLANES
    plan_s = _SeqPlan(bs, ls, ls_pad, seq_blk=min(bs, 16), t_blk=ls_pad, chunk=ls_pad)
    plans_s = {"mix": plan_s, "s5": plan_s, "swa": plan_s}
    plans_p = {
        "mix": _SeqPlan(bp, lp, lp, seq_blk=min(bp, 8), t_blk=min(lp, 256), chunk=min(lp, 64), unroll=8),
        "s5": _SeqPlan(bp, lp, lp, seq_blk=min(bp, SUBLANES), t_blk=min(lp, 128), chunk=min(lp, 64)),
        "swa": _SeqPlan(bp, lp, lp, seq_blk=min(bp, 4), t_blk=min(lp, 256), chunk=min(lp, SW_WINDOW))}
    consts = _constants({plans_p["mix"].chunk, plan_s.chunk})

    gam = jax.nn.softmax(hgrn_lb_logits.astype(F32), axis=0)
    lbs = jnp.cumsum(gam, axis=0) - gam[:1]
    big = {"wg": ffn_w_gate.astype(BF16), "wu": ffn_w_up.astype(BF16), "wd": ffn_w_down.astype(BF16),
           "w_in": _prep_w_in(w_in), "w_out": _prep_w_out(w_out)}
    layers = []
    for l in range(DEPTH):
        sinks = swa_sinks[l].astype(F32)[np.array(_Q_HEAD_ORDER)]
        layers.append({
            "gains": norm_gains[l].astype(F32).reshape(6, 1, D_MODEL),
            "s5": _s5_consts(ssm_lambda_re[l], ssm_lambda_im[l], ssm_log_dt[l], ssm_b_re[l], ssm_b_im[l],
                             ssm_c_re[l], ssm_c_im[l], ssm_d[l], ssm_w_glu1[l], ssm_w_glu2[l]),
            "lb": lbs[l].reshape(1, D_GROUP),
            "hgrn_nw": jnp.tile(hgrn_norm_w[l].astype(F32), N_HEADS).reshape(1, D_GROUP),
            "gdn": {"conv_w": gdn_conv_w[l].astype(F32),
                    "a_log": _lane_vec(gdn_a_log[l].astype(F32), N_HEADS),
                    "dt_bias": _lane_vec(gdn_dt_bias[l].astype(F32), N_HEADS),
                    "norm_w": jnp.tile(gdn_norm_w[l].astype(F32), N_HEADS).reshape(1, D_GROUP)},
            "sinks": sinks,
        })

    def run(x, n_seq, length, plans, state, has_cache):
        l_pad = plans["mix"].l_pad
        if l_pad > length:
            x = jnp.pad(x, ((0, 0), (0, l_pad - length), (0, 0)))
        x2 = x.reshape(n_seq * l_pad, D_MODEL)
        tm = min(512, x2.shape[0])
        y, st = _trunk(x2, state, layers, big, consts, plans, tm, has_cache)
        y = y.reshape(n_seq, l_pad, D_MODEL)[:, :length]
        return (y,) + _finish_states(st, n_seq)

    HK = SW_KV_HEADS * HEAD_DIM
    zeros = lambda *shape: jnp.zeros((1, bp) + shape, F32)
    prompt_state = {"ssm": zeros(2 * SSM_LANES), "hgrn": zeros(D_GROUP, HEAD_DIM),
                    "gdn": zeros(D_GROUP, HEAD_DIM), "conv": zeros(GD_CONV - 1, 3 * D_GROUP),
                    "swa_k": zeros(SW_WINDOW, HK), "swa_v": zeros(SW_WINDOW, HK)}
    sample_state = {
        "ssm": jnp.concatenate([state_ssm_re.reshape(DEPTH, bs, SSM_LANES),
                                state_ssm_im.reshape(DEPTH, bs, SSM_LANES)], axis=-1).astype(F32),
        "hgrn": state_hgrn.astype(F32).reshape(DEPTH, bs, D_GROUP, HEAD_DIM),
        "gdn": state_gdn.astype(F32).reshape(DEPTH, bs, D_GROUP, HEAD_DIM),
        "conv": state_gdn_conv.astype(F32),
        "swa_k": cache_swa_k.astype(F32).reshape(DEPTH, bs, SW_WINDOW, HK),
        "swa_v": cache_swa_v.astype(F32).reshape(DEPTH, bs, SW_WINDOW, HK)}

    yp, sre_p, sim_p, hg_p, gd_p, cv_p, sk_p, sv_p = run(x_prompt, bp, lp, plans_p, prompt_state, False)
    ys, sre_s, sim_s, hg_s, gd_s, cv_s, sk_s, sv_s = run(x_sample, bs, ls, plans_s, sample_state, True)
    return (yp, ys, sre_p, sre_s, sim_p, sim_s, hg_p, hg_s, gd_p, gd_s,
            cv_p, cv_s, sk_p, sk_s, sv_p, sv_s)
```

```python
import functools
import math

import numpy as np
import jax
import jax.numpy as jnp
from jax import lax
from jax.experimental import pallas as pl
from jax.experimental.pallas import tpu as pltpu

D_MODEL = 1024
DEPTH = 4
D_GROUP = 256
N_HEADS = 4
HEAD_DIM = 64
SSM_CH = 16
SSM_GROUPS = 16
SSM_STATE = 64
SSM_LANES = SSM_GROUPS * SSM_STATE
GD_CONV = 4
SW_WINDOW = 128
SW_KV_HEADS = 2
D_FF = 2816
NORM_EPS = 1e-6
PROJ_W = 12 * D_GROUP
SUBLANES = 8
LANES = 128
VMEM_LIMIT = 56 * 1024 * 1024

F32 = jnp.float32
BF16 = jnp.bfloat16
EXACT_TERMS = 3
SUMSQ_TERMS = 2


def _mm(a, b):
    return jnp.dot(a.astype(BF16), b.astype(BF16), preferred_element_type=F32)


def _mm_nt(a, b):
    return lax.dot_general(a.astype(BF16), b.astype(BF16), (((1,), (1,)), ((), ())),
                           preferred_element_type=F32)


def _mm_tn(a, b):
    return lax.dot_general(a.astype(BF16), b.astype(BF16), (((0,), (0,)), ((), ())),
                           preferred_element_type=F32)


def _mms(a, b):
    return _mm(a, b)


def _split_bf16(x, terms):
    parts = []
    for _ in range(terms - 1):
        p = x.astype(BF16)
        parts.append(p)
        x = x - p.astype(F32)
    parts.append(x.astype(BF16))
    return parts


def _sel_r(a, sel_stack, terms):
    return jnp.dot(jnp.concatenate(_split_bf16(a, terms), axis=1), sel_stack, preferred_element_type=F32)


def _sel_l(sel_stack, b, terms):
    return jnp.dot(sel_stack, jnp.concatenate(_split_bf16(b, terms), axis=0), preferred_element_type=F32)


def _sel_nt(sel_stack, b, terms):
    return lax.dot_general(sel_stack, jnp.concatenate(_split_bf16(b, terms), axis=1),
                           (((1,), (1,)), ((), ())), preferred_element_type=F32)


def _rms(x, gain):
    return x * lax.rsqrt(jnp.mean(x * x, axis=-1, keepdims=True) + NORM_EPS) * gain


def _silu(x):
    return x * jax.nn.sigmoid(x)


def _div2(x, d):
    assert d & (d - 1) == 0
    return x >> (d.bit_length() - 1)


def _mod2(x, d):
    assert d & (d - 1) == 0
    return x & (d - 1)


def _head_masks(width=D_GROUP, head_dim=HEAD_DIM):
    lane = lax.broadcasted_iota(jnp.int32, (1, width), 1)
    return [(_div2(lane, head_dim) == h).astype(F32) for h in range(width // head_dim)]


def _stack_heads(x, masks):
    return jnp.concatenate([x * m for m in masks], axis=0)


def _unstack_heads(x, n_heads):
    c = x.shape[0] // n_heads
    out = x[0:c]
    for h in range(1, n_heads):
        out = out + x[h * c:(h + 1) * c]
    return out


def _lane_col(x, lane):
    idx = lax.broadcasted_iota(jnp.int32, x.shape, 1)
    return jnp.sum(jnp.where(idx == lane, x, 0.0), axis=-1, keepdims=True)


def _const_spec(shape):
    return pl.BlockSpec(shape, lambda *_: (0,) * len(shape), pipeline_mode=pl.Buffered(1))


def _layer_spec(shape, lead):
    idx = tuple(lead) + (0,) * len(shape)
    return pl.BlockSpec((None,) * len(lead) + tuple(shape), lambda *_: idx, pipeline_mode=pl.Buffered(1))


def _params(*sem):
    return pltpu.CompilerParams(dimension_semantics=sem, vmem_limit_bytes=VMEM_LIMIT)


def _ffn_kernel(x_ref, gpre_ref, gpost_ref, wg_ref, wu_ref, wd_ref, *rest, mixed):
    x = x_ref[...]
    if mixed:
        ya_ref, yb_ref, yc_ref, yd_ref, wo_ref, gmix_ref, o_ref = rest
        y = jnp.concatenate([ya_ref[...], yb_ref[...], yc_ref[...], yd_ref[...]], axis=-1)
        x = x + _rms(jnp.dot(y.astype(BF16), wo_ref[...], preferred_element_type=F32), gmix_ref[...])
    else:
        o_ref, = rest
    h = _rms(x, gpre_ref[...]).astype(BF16)
    g = jnp.dot(h, wg_ref[...], preferred_element_type=F32)
    u = jnp.dot(h, wu_ref[...], preferred_element_type=F32)
    a = (_silu(g) * u).astype(BF16)
    y = jnp.dot(a, wd_ref[...], preferred_element_type=F32)
    o_ref[...] = x + 0.5 * _rms(y, gpost_ref[...])


def _ffn(x, gpre, gpost, wg, wu, wd, lead, tm, mix=None):
    n = x.shape[0]
    row = pl.BlockSpec((tm, D_MODEL), lambda i: (i, 0))
    in_specs = [row, _const_spec((1, D_MODEL)), _const_spec((1, D_MODEL)),
                _layer_spec((D_MODEL, D_FF), lead), _layer_spec((D_MODEL, D_FF), lead),
                _layer_spec((D_FF, D_MODEL), lead)]
    args = [x, gpre, gpost, wg, wu, wd]
    if mix is not None:
        ys, w_out, gmix = mix
        l_pad, n_seq, _ = ys[0].shape
        grp = pl.BlockSpec((tm, D_GROUP), lambda i: (i, 0))
        ya_spec = _time_major_spec(tm, l_pad)
        if ya_spec is None:
            ya, ya_spec = jnp.swapaxes(ys[0], 0, 1).reshape(n, D_GROUP), grp
        else:
            ya = ys[0].reshape(l_pad, n_seq * D_GROUP)
        in_specs += [ya_spec, grp, grp, grp, _layer_spec((D_MODEL, D_MODEL), lead[:1]), _const_spec((1, D_MODEL))]
        args += [ya, *ys[1:], w_out, gmix]
    return pl.pallas_call(
        functools.partial(_ffn_kernel, mixed=mix is not None),
        grid=(n // tm,),
        in_specs=in_specs,
        out_specs=row,
        out_shape=jax.ShapeDtypeStruct((n, D_MODEL), F32),
        compiler_params=_params("parallel"),
        name="ffn",
    )(*args)


def _time_major_spec(tm, l_pad):
    if l_pad % tm:
        return None
    per_seq = l_pad // tm
    return pl.BlockSpec((tm, D_GROUP), lambda i: (i % per_seq, i // per_seq))


def _inproj_kernel(x_ref, g_ref, w_ref, o_ref, *u_ref):
    h = _rms(x_ref[...], g_ref[...]).astype(BF16)
    proj = jnp.dot(h, w_ref[...], preferred_element_type=F32)
    o_ref[...] = proj
    if u_ref:
        u_ref[0][...] = proj[:, 0:D_GROUP]


def _inproj(x, gain, w, lead, tm, n_seq, l_pad):
    n = x.shape[0]
    out_specs = [pl.BlockSpec((tm, PROJ_W), lambda i: (i, 0))]
    out_shape = [jax.ShapeDtypeStruct((n, PROJ_W), F32)]
    u_spec = _time_major_spec(tm, l_pad)
    if u_spec is not None:
        out_specs.append(u_spec)
        out_shape.append(jax.ShapeDtypeStruct((l_pad, n_seq * D_GROUP), F32))
    outs = pl.pallas_call(
        _inproj_kernel,
        grid=(n // tm,),
        in_specs=[pl.BlockSpec((tm, D_MODEL), lambda i: (i, 0)), _const_spec((1, D_MODEL)),
                  _layer_spec((D_MODEL, PROJ_W), lead)],
        out_specs=out_specs,
        out_shape=out_shape,
        compiler_params=_params("parallel"),
        name="inproj",
    )(x, gain, w)
    proj = outs[0].reshape(n_seq, l_pad, PROJ_W)
    if u_spec is None:
        return proj, None
    return proj, outs[1].reshape(l_pad, n_seq, D_GROUP)


class _SeqPlan:
    def __init__(self, n_seq, length, l_pad, seq_blk, t_blk, chunk, unroll=4):
        assert l_pad % t_blk == 0 and t_blk % chunk == 0 and n_seq % seq_blk == 0
        assert chunk % SUBLANES == 0 and 0 <= l_pad - length < SUBLANES
        assert length % chunk == 0 or l_pad == chunk
        assert seq_blk & (seq_blk - 1) == 0
        self.n_seq, self.length, self.l_pad = n_seq, length, l_pad
        self.seq_blk, self.t_blk, self.chunk = seq_blk, t_blk, chunk
        self.rows = seq_blk * t_blk
        self.t_steps = l_pad // t_blk
        self.grid = (n_seq // seq_blk, self.t_steps)
        self.unroll = min(unroll, seq_blk)

    def rows_spec(self, width, col_block=0):
        return pl.BlockSpec((self.seq_blk, self.t_blk, width), lambda i, j: (i, j, col_block))

    def seq_spec(self, *tail, layer=None):
        zeros = (0,) * len(tail)
        if layer is None:
            return pl.BlockSpec((self.seq_blk,) + tail, lambda i, j: (i,) + zeros)
        return pl.BlockSpec((None, self.seq_blk) + tail, lambda i, j: (layer, i) + zeros)

    def rows_shape(self):
        return jax.ShapeDtypeStruct((self.n_seq, self.l_pad, D_GROUP), F32)

    def for_unit_groups(self, size, body):
        n_units = self.seq_blk * (self.t_blk // size)

        def step(n, carry):
            if self.unroll == self.seq_blk:
                body([(u, n) for u in range(self.unroll)])
            else:
                units = [n * self.unroll + u for u in range(self.unroll)]
                body([(_mod2(unit, self.seq_blk), _div2(unit, self.seq_blk)) for unit in units])
            return carry

        lax.fori_loop(0, n_units // self.unroll, step, 0)

    def for_seq_groups(self, body):
        def step(n, carry):
            body([n * self.unroll + u for u in range(self.unroll)])
            return carry
        lax.fori_loop(0, self.seq_blk // self.unroll, step, 0)

    def for_units(self, size, body):
        def group(units):
            for s, c in units:
                body(s, c)
        self.for_unit_groups(size, group)

    def valid_rows(self, j, c, size):
        t = j * self.t_blk + c * size + lax.broadcasted_iota(jnp.int32, (size, 1), 0)
        return t < self.length


def _seq_call(kernel, plan, in_specs, out_specs, out_shape, scratch, name):
    return pl.pallas_call(
        kernel, grid=plan.grid, in_specs=in_specs, out_specs=out_specs, out_shape=out_shape,
        scratch_shapes=scratch, compiler_params=_params("parallel", "arbitrary"), name=name)


def _s5_kernel(u_ref, h0_ref, bmat_ref, cmat_ref, d_ref, w1_ref, w2_ref, a_ref,
               y_ref, hn_ref, s_scr, carry_scr, *, plan):
    j = pl.program_id(1)
    P = SSM_LANES
    S = plan.seq_blk
    slab = min(SUBLANES, S)

    @pl.when(j == 0)
    def _():
        carry_scr[...] = h0_ref[...]

    u = u_ref[...].reshape(plan.rows, D_GROUP)
    s_scr[...] = _mm(u, bmat_ref[...])
    a_re, a_im = a_ref[0:slab, 0:P], a_ref[0:slab, P:2 * P]

    def step(t, x):
        out = []
        for g in range(S // slab):
            rows = pl.ds(pl.multiple_of(t * S + g * slab, slab), slab)
            x_re, x_im = x[2 * g], x[2 * g + 1]
            n_re = a_re * x_re - a_im * x_im + s_scr[rows, 0:P]
            n_im = a_re * x_im + a_im * x_re + s_scr[rows, P:2 * P]
            s_scr[rows, 0:P] = n_re
            s_scr[rows, P:2 * P] = n_im
            out += [n_re, n_im]
        return tuple(out)

    x0 = []
    for g in range(S // slab):
        x0 += [carry_scr[g * slab:(g + 1) * slab, 0:P], carry_scr[g * slab:(g + 1) * slab, P:2 * P]]
    x = lax.fori_loop(0, plan.t_blk, step, tuple(x0))
    for g in range(S // slab):
        carry_scr[g * slab:(g + 1) * slab, 0:P] = x[2 * g]
        carry_scr[g * slab:(g + 1) * slab, P:2 * P] = x[2 * g + 1]

    y = _mm(s_scr[...], cmat_ref[...]) + d_ref[...] * u
    y = jax.nn.gelu(y)
    y = _mm(y, w1_ref[...]) * jax.nn.sigmoid(_mm(y, w2_ref[...]))
    t = j * plan.t_blk + _div2(lax.broadcasted_iota(jnp.int32, (plan.rows, 1), 0), S)
    y_ref[...] = jnp.where(t < plan.length, y, 0.0).reshape(plan.t_blk, S, D_GROUP)

    @pl.when(j == plan.t_steps - 1)
    def _():
        hn_ref[...] = s_scr[((plan.length - 1) % plan.t_blk) * S:((plan.length - 1) % plan.t_blk + 1) * S, :]


def _s5(u_tm, h0, layer, consts, plan):
    P2 = 2 * SSM_LANES
    tm_spec = pl.BlockSpec((plan.t_blk, plan.seq_blk, D_GROUP), lambda i, j: (j, i, 0))
    in_specs = [tm_spec, plan.seq_spec(P2, layer=layer),
                _const_spec((D_GROUP, P2)), _const_spec((P2, D_GROUP)), _const_spec((1, D_GROUP)),
                _const_spec((D_GROUP, D_GROUP)), _const_spec((D_GROUP, D_GROUP)),
                _const_spec((SUBLANES, P2))]
    out_specs = [tm_spec, plan.seq_spec(P2)]
    out_shape = [jax.ShapeDtypeStruct((plan.l_pad, plan.n_seq, D_GROUP), F32),
                 jax.ShapeDtypeStruct((plan.n_seq, P2), F32)]
    scratch = [pltpu.VMEM((plan.rows, P2), F32), pltpu.VMEM((plan.seq_blk, P2), F32)]
    return _seq_call(functools.partial(_s5_kernel, plan=plan), plan, in_specs, out_specs, out_shape,
                     scratch, "s5")(u_tm, h0, *consts)


def _selectors(consts, chunk, transposed):
    arrays = (consts["bd"], consts["bd2"], consts["tile_tc" if transposed else "tile"], consts["tile_t"],
              consts["tri"][chunk])
    return arrays, [_const_spec(a.shape) for a in arrays]


def _load_state_bd(s0, tile_mat, bd_mask):
    return _sel_r(s0, tile_mat, EXACT_TERMS) * bd_mask


def _store_state_bd(s_bd, tile_mat_t):
    return _sel_r(s_bd, tile_mat_t, EXACT_TERMS)


def _load_state_bd_t(s0, tile_mat_tc, bd_mask):
    return _sel_nt(tile_mat_tc, s0, EXACT_TERMS) * bd_mask


def _store_state_bd_t(st_bd, tile_mat_t):
    parts = jnp.concatenate(_split_bf16(st_bd, EXACT_TERMS), axis=0)
    return lax.dot_general(parts, tile_mat_t, (((0,), (0,)), ((), ())), preferred_element_type=F32)


def _hgrn_kernel(q_ref, f_ref, i_ref, g_ref, s0_ref, lb_ref, nw_ref,
                 bd_ref, bd2_ref, tile_ref, tilet_ref, tri_ref,
                 y_ref, sn_ref, st_scr, *, plan, sub):
    j = pl.program_id(1)
    C = plan.chunk
    masks = _head_masks()
    bd = bd_ref[...]
    lb = lb_ref[...]

    @pl.when(j == 0)
    def _():
        def init(seqs):
            for s, st in zip(seqs, [_load_state_bd_t(s0_ref[s], tile_ref[...], bd) for s in seqs]):
                st_scr[s] = st
        plan.for_seq_groups(init)

    stacked_mask = jnp.concatenate([jnp.broadcast_to(m, (sub, D_GROUP)) for m in masks], axis=0)

    def gated_inputs(s, c, valid):
        rows = pl.ds(pl.multiple_of(c * C, C), C)
        fr = f_ref[s, rows, :]
        f = lb + (1.0 - lb) * jax.nn.sigmoid(fr)
        k = jnp.where(valid, (1.0 - lb) * jax.nn.sigmoid(-fr), 0.0)
        logf = jnp.where(valid, jnp.log(f), 0.0)
        return _silu(q_ref[s, rows, :]), k, i_ref[s, rows, :], logf

    def scores(q, k, cum, blk):
        r0, r1 = blk * sub, (blk + 1) * sub
        c0 = cum[r0 - 1:r0] if blk else jnp.zeros_like(cum[0:1])
        qs = _stack_heads(q[r0:r1] * jnp.exp(cum[r0:r1] - c0), masks)
        att = _mm_nt(qs, k[0:r1] * jnp.exp(c0 - cum[0:r1]))
        t_idx = r0 + _mod2(lax.broadcasted_iota(jnp.int32, att.shape, 0), sub)
        s_idx = lax.broadcasted_iota(jnp.int32, att.shape, 1)
        return jnp.where(s_idx <= t_idx, att, 0.0)

    def chunk(units):
        each = lambda fn, *xs: [fn(*a) for a in zip(*xs)]
        valid = [plan.valid_rows(j, c, C) for _, c in units]
        qkvl = [gated_inputs(s, c, ok) for (s, c), ok in zip(units, valid)]
        q, k, v, logf = ([t[i] for t in qkvl] for i in range(4))
        cum = each(lambda x: _sel_l(tri_ref[...], x, EXACT_TERMS), logf)
        st = [st_scr[s] for s, _ in units]
        o = each(lambda a, b, x: _mm_nt(a * jnp.exp(b), x), q, cum, st)
        intra = [[] for _ in units]
        for blk in range(C // sub):
            att = each(lambda a, b, d: scores(a, b, d, blk), q, k, cum)
            pv = each(lambda a, x: _mm(a, x[0:(blk + 1) * sub]), att, v)
            for lst, x in zip(intra, pv):
                lst.append(_unstack_heads(x * stacked_mask, N_HEADS))
        o = each(lambda a, lst: a + jnp.concatenate(lst, axis=0), o, intra)
        upd = each(lambda x, b, d: _mm_tn(x, b * jnp.exp(d[C - 1:C] - d)) * bd, v, k, cum)
        for (s, _), x, d, u in zip(units, st, cum, upd):
            st_scr[s] = x * jnp.exp(d[C - 1:C]) + u
        ms = each(lambda x: _sel_r(x * x, bd2_ref[...], SUMSQ_TERMS) * (1.0 / HEAD_DIM), o)
        for (s, c), x, ss, ok in zip(units, o, ms, valid):
            rows = pl.ds(pl.multiple_of(c * C, C), C)
            x = x * lax.rsqrt(ss + NORM_EPS) * nw_ref[...] * _silu(g_ref[s, rows, :])
            y_ref[s, rows, :] = jnp.where(ok, x, 0.0)

    plan.for_unit_groups(C, chunk)

    @pl.when(j == plan.t_steps - 1)
    def _():
        def fin(seqs):
            for s, st in zip(seqs, [_store_state_bd_t(st_scr[s], tilet_ref[...]) for s in seqs]):
                sn_ref[s] = st
        plan.for_seq_groups(fin)


def _hgrn(proj, s0, layer, lb, norm_w, consts, plan):
    C = plan.chunk
    sub = min(16, C)
    selectors, selector_specs = _selectors(consts, C, transposed=True)
    in_specs = [plan.rows_spec(D_GROUP, 1), plan.rows_spec(D_GROUP, 2), plan.rows_spec(D_GROUP, 3),
                plan.rows_spec(D_GROUP, 4), plan.seq_spec(D_GROUP, HEAD_DIM, layer=layer),
                _const_spec((1, D_GROUP)), _const_spec((1, D_GROUP))] + selector_specs
    out_specs = [plan.rows_spec(D_GROUP), plan.seq_spec(D_GROUP, HEAD_DIM)]
    out_shape = [plan.rows_shape(), jax.ShapeDtypeStruct((plan.n_seq, D_GROUP, HEAD_DIM), F32)]
    scratch = [pltpu.VMEM((plan.seq_blk, D_GROUP, D_GROUP), F32)]
    kern = functools.partial(_hgrn_kernel, plan=plan, sub=sub)
    return _seq_call(kern, plan, in_specs, out_specs, out_shape, scratch, "hgrn2")(
        proj, proj, proj, proj, s0, lb, norm_w, *selectors)


def _gdn_kernel(q_ref, k_ref, v_ref, z_ref, sc_ref, s0_ref, cv0_ref, cw_ref, alog_ref, dtb_ref, nw_ref,
                bd_ref, bd2_ref, tile_ref, tilet_ref, tri_ref, eb_ref, ea_ref, lvl0_ref, lvl_ref,
                y_ref, sn_ref, cvn_ref, st_scr, cx_scr, prev_scr, *, plan):
    j = pl.program_id(1)
    C = plan.chunk
    H = N_HEADS
    HC = H * C
    masks = _head_masks()
    bd = bd_ref[...]
    lane = lax.broadcasted_iota(jnp.int32, (1, D_GROUP), 1)
    beta_lanes = lane < H
    a_lanes = (lane >= H) & (lane < 2 * H)
    neg_rate = jnp.where(a_lanes, -jnp.exp(alog_ref[...]), 0.0)
    n_lvl = max(1, int(math.ceil(math.log2(C))))
    valid_last = (plan.length - 1) % C + 1
    assert valid_last >= GD_CONV - 1
    pad0 = SUBLANES - (GD_CONV - 1)

    @pl.when(j == 0)
    def _():
        def init(seqs):
            for s, st in zip(seqs, [_load_state_bd(s0_ref[s], tile_ref[...], bd) for s in seqs]):
                st_scr[s] = st
                prev_scr[s] = jnp.zeros((SUBLANES, 3 * D_GROUP), F32)
                prev_scr[s, pl.ds(pad0, GD_CONV - 1), :] = cv0_ref[s]
        plan.for_seq_groups(init)

    row_i = lax.broadcasted_iota(jnp.int32, (HC, HC), 0)
    col_i = lax.broadcasted_iota(jnp.int32, (HC, HC), 1)
    same_head = _div2(row_i, C) == _div2(col_i, C)
    incl = same_head & (col_i <= row_i)
    strict = same_head & (col_i < row_i)
    eye = (row_i == col_i).astype(F32)

    def head_rows(gc_all):
        first_tile = gc_all[:, 0:LANES]
        if C < LANES:
            first_tile = jnp.concatenate([first_tile, jnp.zeros((LANES - C, LANES), F32)], axis=0)
        tr = first_tile.T
        return jnp.concatenate([tr[H + h:H + h + 1, 0:C] for h in range(H)], axis=1)

    def conv_qkv(s, c):
        rows = pl.ds(pl.multiple_of(c * C, C), C)
        cx = cx_scr.at[s]
        cx[0:SUBLANES, :] = prev_scr[s]
        cx[SUBLANES:SUBLANES + C, 0:D_GROUP] = q_ref[s, rows, :]
        cx[SUBLANES:SUBLANES + C, D_GROUP:2 * D_GROUP] = k_ref[s, rows, :]
        cx[SUBLANES:SUBLANES + C, 2 * D_GROUP:3 * D_GROUP] = v_ref[s, rows, :]
        cw = cw_ref[...]
        conv = cx[pad0:pad0 + C, :] * cw[0:1]
        for tap in range(1, GD_CONV):
            conv = conv + cx[pad0 + tap:pad0 + tap + C, :] * cw[tap:tap + 1]
        prev_scr[s, pl.ds(pad0, GD_CONV - 1), :] = cx[SUBLANES + C - (GD_CONV - 1):SUBLANES + C, :]
        cvn_ref[s] = cx[SUBLANES + valid_last - (GD_CONV - 1):SUBLANES + valid_last, :]
        conv = _silu(conv)
        return conv[:, 0:D_GROUP], conv[:, D_GROUP:2 * D_GROUP], conv[:, 2 * D_GROUP:3 * D_GROUP]

    def gates(s, c, valid):
        sc = sc_ref[s, pl.ds(pl.multiple_of(c * C, C), C), :]
        beta_all = jnp.where(valid & beta_lanes, jax.nn.sigmoid(sc), 0.0)
        g_all = jnp.where(valid, neg_rate * jax.nn.softplus(sc + dtb_ref[...]), 0.0)
        return beta_all, g_all

    def decay_matrix(gc_all, gc_row):
        gc_col = jnp.concatenate([_lane_col(gc_all, H + h) for h in range(H)], axis=0)
        return jnp.exp(jnp.where(incl, gc_col - gc_row, -jnp.inf))

    def chunk(units):
        each = lambda f, *xs: [f(*a) for a in zip(*xs)]
        valid = [plan.valid_rows(j, c, C) for _, c in units]
        qkv = [conv_qkv(s, c) for s, c in units]
        q, k, v = ([t[i] for t in qkv] for i in range(3))
        qss = each(lambda x: _sel_r(x * x, bd2_ref[...], SUMSQ_TERMS), q)
        kss = each(lambda x: _sel_r(x * x, bd2_ref[...], SUMSQ_TERMS), k)
        q = each(lambda x, ss: x * lax.rsqrt(ss + NORM_EPS) * (HEAD_DIM ** -0.5), q, qss)
        k = each(lambda x, ss, ok: jnp.where(ok, x * lax.rsqrt(ss + NORM_EPS), 0.0), k, kss, valid)

        bg = [gates(s, c, ok) for (s, c), ok in zip(units, valid)]
        beta_all, g_all = [t[0] for t in bg], [t[1] for t in bg]
        gc_all = each(lambda g: _sel_l(tri_ref[...], g, EXACT_TERMS), g_all)
        beta_l = each(lambda b: _sel_r(b, eb_ref[...], EXACT_TERMS), beta_all)
        gc_l = each(lambda g: _sel_r(g, ea_ref[...], EXACT_TERMS), gc_all)
        gam_l = each(jnp.exp, gc_l)
        gc_row = each(head_rows, gc_all)
        dec = each(decay_matrix, gc_all, gc_row)
        beta_col = each(lambda b: jnp.concatenate([_lane_col(b, h) for h in range(H)], axis=0), beta_all)

        ks = each(lambda x: _stack_heads(x, masks), k)
        qs = each(lambda x: _stack_heads(x, masks), q)
        kk = each(_mm_nt, ks, ks)
        m = each(lambda d, b, x: jnp.where(strict, d, 0.0) * b * x, dec, beta_col, kk)
        t_inv = each(lambda x: eye - x * lvl0_ref[...], m)
        m_b = each(lambda x: x.astype(BF16), m)
        for lvl in range(1, n_lvl):
            t_b = each(lambda t: t.astype(BF16), t_inv)
            half = each(lambda t, x: _mms(t, x * lvl_ref[lvl - 1]), t_b, m_b)
            t_inv = each(lambda t, hf, tb: t - _mms(hf, tb), t_inv, half, t_b)
        t_b = each(lambda t: t.astype(BF16), t_inv)
        rhs_w = each(lambda b, g, x: _stack_heads(b * g * x, masks), beta_l, gam_l, k)
        rhs_u = each(lambda b, x: _stack_heads(b * x, masks), beta_l, v)
        w = each(lambda t, r: _unstack_heads(_mms(t, r), H), t_b, rhs_w)
        u0 = each(lambda t, r: _unstack_heads(_mms(t, r), H), t_b, rhs_u)
        qk = each(lambda a, b, d: _mm_nt(a, b) * d, qs, ks, dec)

        st = [st_scr[s] for s, _ in units]
        u = each(lambda a, b, x: a - _mm(b, x), u0, w, st)
        o_st = each(lambda x, g, y: _mm(x * g, y), q, gam_l, st)
        o_in = each(lambda a, b: _unstack_heads(_mm(a, _stack_heads(b, masks)), H), qk, u)
        upd = each(lambda x, g, b: _mm_tn(x * jnp.exp(g[C - 1:C] - g), b) * bd, k, gc_l, u)
        for (s, _), g, x, d in zip(units, gc_l, st, upd):
            st_scr[s] = jnp.exp(g[C - 1:C]) * x + d

        o = each(jnp.add, o_st, o_in)
        ms = each(lambda x: _sel_r(x * x, bd2_ref[...], SUMSQ_TERMS) * (1.0 / HEAD_DIM), o)
        for (s, c), x, ss, ok in zip(units, o, ms, valid):
            rows = pl.ds(pl.multiple_of(c * C, C), C)
            x = x * lax.rsqrt(ss + NORM_EPS) * nw_ref[...] * _silu(z_ref[s, rows, :])
            y_ref[s, rows, :] = jnp.where(ok, x, 0.0)

    plan.for_unit_groups(C, chunk)

    @pl.when(j == plan.t_steps - 1)
    def _():
        def fin(seqs):
            for s, st in zip(seqs, [_store_state_bd(st_scr[s], tilet_ref[...]) for s in seqs]):
                sn_ref[s] = st
        plan.for_seq_groups(fin)


def _gdn(proj, s0, cv0, layer, lw, consts, plan):
    C = plan.chunk
    W3 = 3 * D_GROUP
    in_specs = [plan.rows_spec(D_GROUP, 5), plan.rows_spec(D_GROUP, 6), plan.rows_spec(D_GROUP, 7),
                plan.rows_spec(D_GROUP, 8), plan.rows_spec(D_GROUP, 11),
                plan.seq_spec(D_GROUP, HEAD_DIM, layer=layer), plan.seq_spec(GD_CONV - 1, W3, layer=layer),
                _const_spec((GD_CONV, W3)), _const_spec((1, D_GROUP)), _const_spec((1, D_GROUP)),
                _const_spec((1, D_GROUP))] + _selectors(consts, C, transposed=False)[1] + [
                _const_spec((EXACT_TERMS * D_GROUP, D_GROUP)), _const_spec((EXACT_TERMS * D_GROUP, D_GROUP)),
                _const_spec(consts["solve"][C][0].shape), _const_spec(consts["solve"][C][1].shape)]
    out_specs = [plan.rows_spec(D_GROUP), plan.seq_spec(D_GROUP, HEAD_DIM), plan.seq_spec(GD_CONV - 1, W3)]
    out_shape = [plan.rows_shape(),
                 jax.ShapeDtypeStruct((plan.n_seq, D_GROUP, HEAD_DIM), F32),
                 jax.ShapeDtypeStruct((plan.n_seq, GD_CONV - 1, W3), F32)]
    scratch = [pltpu.VMEM((plan.seq_blk, D_GROUP, D_GROUP), F32),
               pltpu.VMEM((plan.seq_blk, SUBLANES + C, W3), F32),
               pltpu.VMEM((plan.seq_blk, SUBLANES, W3), F32)]
    kern = functools.partial(_gdn_kernel, plan=plan)
    return _seq_call(kern, plan, in_specs, out_specs, out_shape, scratch, "gdn")(
        proj, proj, proj, proj, proj, s0, cv0, lw["conv_w"], lw["a_log"], lw["dt_bias"], lw["norm_w"],
        *_selectors(consts, C, transposed=False)[0], consts["eb"], consts["ea"], *consts["solve"][C])


def _swa_kernel(q_ref, kv_ref, ck_ref, cv_ref, bias_ref, sink_ref, y_ref, kn_ref, vn_ref,
                kk_scr, vv_scr, *, plan, has_cache):
    j = pl.program_id(1)
    QB = plan.chunk
    W = SW_WINDOW
    HK = SW_KV_HEADS * HEAD_DIM
    kv_masks = _head_masks(HK)
    valid_last = (plan.length - 1) % QB + 1

    @pl.when(j == 0)
    def _():
        kn_ref[...] = ck_ref[...]
        vn_ref[...] = cv_ref[...]

    def stacked_queries(s, c):
        rows = pl.ds(pl.multiple_of(c * QB, QB), QB)
        kk, vv = kk_scr.at[s], vv_scr.at[s]
        kk[0:W, :] = kn_ref[s]
        vv[0:W, :] = vn_ref[s]
        kv = kv_ref[s, rows, :]
        kk[W:W + QB, :] = kv[:, 0:HK]
        vv[W:W + QB, :] = kv[:, HK:2 * HK]
        q = q_ref[s, rows, :]
        return jnp.concatenate([q[:, 0:HK] * kv_masks[0], q[:, 0:HK] * kv_masks[1],
                                q[:, HK:2 * HK] * kv_masks[0], q[:, HK:2 * HK] * kv_masks[1]], axis=0)

    def probabilities(sc, c):
        sc = sc * (HEAD_DIM ** -0.5) + bias_ref[...]
        if not has_cache:
            t0 = j * plan.t_blk + c * QB
            col = lax.broadcasted_iota(jnp.int32, sc.shape, 1)
            sc = jnp.where(t0 + col < W, -jnp.inf, sc)
        sink = sink_ref[...]
        mx = jnp.maximum(jnp.max(sc, axis=-1, keepdims=True), sink)
        p = jnp.exp(sc - mx)
        denom = jnp.sum(p, axis=-1, keepdims=True) + jnp.exp(sink - mx)
        return p / denom

    def block(units):
        qs = [stacked_queries(s, c) for s, c in units]
        sc = [_mm_nt(x, kk_scr[s]) for x, (s, _) in zip(qs, units)]
        p = [probabilities(x, c) for x, (_, c) in zip(sc, units)]
        pv = [_mm(x, vv_scr[s]) for x, (s, _) in zip(p, units)]
        for x, (s, c) in zip(pv, units):
            oa = x[0:QB] * kv_masks[0] + x[QB:2 * QB] * kv_masks[1]
            ob = x[2 * QB:3 * QB] * kv_masks[0] + x[3 * QB:4 * QB] * kv_masks[1]
            rows = pl.ds(pl.multiple_of(c * QB, QB), QB)
            y_ref[s, rows, :] = jnp.where(plan.valid_rows(j, c, QB), jnp.concatenate([oa, ob], axis=-1), 0.0)
            kn_ref[s] = kk_scr[s, valid_last:valid_last + W, :]
            vn_ref[s] = vv_scr[s, valid_last:valid_last + W, :]

    plan.for_unit_groups(QB, block)


def _swa(proj, cache_k, cache_v, layer, bias, sink_col, plan, has_cache):
    QB = plan.chunk
    HK = SW_KV_HEADS * HEAD_DIM
    NK = SW_WINDOW + QB
    in_specs = [plan.rows_spec(D_GROUP, 9), plan.rows_spec(D_GROUP, 10),
                plan.seq_spec(SW_WINDOW, HK, layer=layer), plan.seq_spec(SW_WINDOW, HK, layer=layer),
                _const_spec((4 * QB, NK)), _const_spec((4 * QB, 1))]
    out_specs = [plan.rows_spec(D_GROUP), plan.seq_spec(SW_WINDOW, HK), plan.seq_spec(SW_WINDOW, HK)]
    out_shape = [plan.rows_shape(),
                 jax.ShapeDtypeStruct((plan.n_seq, SW_WINDOW, HK), F32),
                 jax.ShapeDtypeStruct((plan.n_seq, SW_WINDOW, HK), F32)]
    scratch = [pltpu.VMEM((plan.seq_blk, NK, HK), F32), pltpu.VMEM((plan.seq_blk, NK, HK), F32)]
    kern = functools.partial(_swa_kernel, plan=plan, has_cache=has_cache)
    return _seq_call(kern, plan, in_specs, out_specs, out_shape, scratch, "swa")(
        proj, proj, cache_k, cache_v, bias, sink_col)


_Q_HEAD_ORDER = (0, 2, 1, 3)


def _swa_bias(qb):
    w = SW_WINDOW
    i = np.arange(qb)[:, None]
    jj = np.arange(w + qb)[None, :]
    dist = w + i - jj
    ok = (dist >= 0) & (dist <= w)
    slopes = 2.0 ** (-8.0 * np.arange(1, N_HEADS + 1) / N_HEADS)
    blocks = []
    for g in range(2):
        for kv in range(SW_KV_HEADS):
            head = kv * 2 + g
            blocks.append(np.where(ok, -slopes[head] * dist, -np.inf))
    return jnp.asarray(np.concatenate(blocks, axis=0), F32)


def _constants(chunks):
    lane_head = np.arange(D_GROUP) // HEAD_DIM
    bd = (lane_head[:, None] == lane_head[None, :]).astype(np.float32)
    tile = np.tile(np.eye(HEAD_DIM, dtype=np.float32), (1, N_HEADS))
    eb = np.zeros((D_GROUP, D_GROUP), np.float32)
    ea = np.zeros((D_GROUP, D_GROUP), np.float32)
    for h in range(N_HEADS):
        eb[h, lane_head == h] = 1.0
        ea[N_HEADS + h, lane_head == h] = 1.0
    def solve_levels(c):
        blk = np.arange(N_HEADS * c) >> 1
        out = [blk[:, None] == blk[None, :]]
        while (1 << len(out)) < c:
            out.append(((blk >> 1)[:, None] == (blk >> 1)[None, :]) & (blk[:, None] != blk[None, :]))
            blk = blk >> 1
        return jnp.asarray(out[0].astype(np.float32)), jnp.asarray(np.stack(out[1:]).astype(np.float32), BF16)

    rows = lambda m, n: jnp.asarray(np.tile(m, (n, 1)), BF16)
    cols = lambda m, n: jnp.asarray(np.tile(m, (1, n)), BF16)
    return {"bd": jnp.asarray(bd), "bd2": rows(bd, SUMSQ_TERMS),
            "tile": rows(tile, EXACT_TERMS), "tile_t": rows(tile.T, EXACT_TERMS), "tile_tc": cols(tile.T, EXACT_TERMS),
            "eb": rows(eb, EXACT_TERMS), "ea": rows(ea, EXACT_TERMS),
            "tri": {c: cols(np.tril(np.ones((c, c), np.float32)), EXACT_TERMS) for c in chunks},
            "solve": {c: solve_levels(c) for c in chunks}}


def _s5_consts(lam_re, lam_im, log_dt, b_re, b_im, c_re, c_im, d, w1, w2):
    dt = jnp.exp(log_dt)[:, None]
    mag = jnp.exp(lam_re * dt)
    ang = lam_im * dt
    a_re, a_im = mag * jnp.cos(ang), mag * jnp.sin(ang)
    den = lam_re * lam_re + lam_im * lam_im
    z_re = ((a_re - 1.0) * lam_re + a_im * lam_im) / den
    z_im = (a_im * lam_re - (a_re - 1.0) * lam_im) / den
    bb_re = z_re[..., None] * b_re - z_im[..., None] * b_im
    bb_im = z_re[..., None] * b_im + z_im[..., None] * b_re
    eye = jnp.eye(SSM_GROUPS, dtype=F32)
    to_b = lambda t: jnp.einsum("gpc,gh->gchp", t, eye).reshape(D_GROUP, SSM_LANES)
    to_c = lambda t: jnp.einsum("gcp,gh->gphc", t, eye).reshape(SSM_LANES, D_GROUP)
    bmat = jnp.concatenate([to_b(bb_re), to_b(bb_im)], axis=1).astype(BF16)
    cmat = jnp.concatenate([to_c(c_re), -to_c(c_im)], axis=0).astype(BF16)
    a_row = jnp.concatenate([a_re.reshape(1, SSM_LANES), a_im.reshape(1, SSM_LANES)], axis=1)
    return (bmat, cmat, d.reshape(1, D_GROUP), w1.astype(BF16), w2.astype(BF16),
            jnp.broadcast_to(a_row, (SUBLANES, 2 * SSM_LANES)))


def _prep_w_in(w):
    q_d = w[:, :, 2312:2568].reshape(DEPTH, D_MODEL, N_HEADS, HEAD_DIM)[:, :, np.array(_Q_HEAD_ORDER)]
    scal = jnp.pad(w[:, :, 2304:2312], ((0, 0), (0, 0), (0, D_GROUP - 2 * N_HEADS)))
    return jnp.concatenate([w[:, :, 0:2304], q_d.reshape(DEPTH, D_MODEL, D_GROUP), w[:, :, 2568:2824], scal],
                           axis=2).astype(BF16)


def _prep_w_out(w):
    d_rows = w[:, 3 * D_GROUP:].reshape(DEPTH, N_HEADS, HEAD_DIM, D_MODEL)[:, np.array(_Q_HEAD_ORDER)]
    return jnp.concatenate([w[:, :3 * D_GROUP], d_rows.reshape(DEPTH, D_GROUP, D_MODEL)], axis=1).astype(BF16)


def _lane_vec(vals, first_lane):
    return jnp.zeros((1, D_GROUP), F32).at[0, first_lane:first_lane + vals.shape[0]].set(vals)


def _trunk(x, state, layers, big, consts, plans, tm, has_cache):
    plan, plan5, plan_w = plans["mix"], plans["s5"], plans["swa"]
    bias = _swa_bias(plan_w.chunk)
    outs = {k: [] for k in ("ssm", "hgrn", "gdn", "conv", "swa_k", "swa_v")}
    for l, lw in enumerate(layers):
        sink_col = jnp.repeat(lw["sinks"], plan_w.chunk).reshape(4 * plan_w.chunk, 1)
        gn = lw["gains"]
        x = _ffn(x, gn[0], gn[1], big["wg"], big["wu"], big["wd"], (l, 0), tm)
        proj, u_tm = _inproj(x, gn[2], big["w_in"], (l,), tm, plan.n_seq, plan.l_pad)
        if u_tm is None:
            u_tm = jnp.swapaxes(proj[:, :, 0:D_GROUP], 0, 1)
        sl = l if has_cache else 0
        y_tm, ssm = _s5(u_tm, state["ssm"], sl, lw["s5"], plan5)
        y_b, hg = _hgrn(proj, state["hgrn"], sl, lw["lb"], lw["hgrn_nw"], consts, plan)
        y_c, gd, cv = _gdn(proj, state["gdn"], state["conv"], sl, lw["gdn"], consts, plans["gdn"])
        y_d, ck, cvv = _swa(proj, state["swa_k"], state["swa_v"], sl, bias, sink_col, plan_w, has_cache)
        ys = (y_tm,) + tuple(y.reshape(x.shape[0], D_GROUP) for y in (y_b, y_c, y_d))
        x = _ffn(x, gn[4], gn[5], big["wg"], big["wu"], big["wd"], (l, 1), tm, mix=(ys, big["w_out"], gn[3]))
        for k, val in zip(outs, (ssm, hg, gd, cv, ck, cvv)):
            outs[k].append(val)
    return x, {k: jnp.stack(v, axis=0) for k, v in outs.items()}


def _finish_states(st, n_seq):
    ssm = st["ssm"].reshape(DEPTH, n_seq, 2, SSM_GROUPS, SSM_STATE)
    hg = st["hgrn"].reshape(DEPTH, n_seq, N_HEADS, HEAD_DIM, HEAD_DIM)
    gd = st["gdn"].reshape(DEPTH, n_seq, N_HEADS, HEAD_DIM, HEAD_DIM)
    sk = st["swa_k"].reshape(DEPTH, n_seq, SW_WINDOW, SW_KV_HEADS, HEAD_DIM)
    sv = st["swa_v"].reshape(DEPTH, n_seq, SW_WINDOW, SW_KV_HEADS, HEAD_DIM)
    return ssm[:, :, 0], ssm[:, :, 1], hg, gd, st["conv"], sk, sv


def kernel(x_prompt, x_sample, state_ssm_re, state_ssm_im, state_hgrn, state_gdn, state_gdn_conv,
           cache_swa_k, cache_swa_v, norm_gains, ffn_w_gate, ffn_w_up, ffn_w_down, w_in, w_out,
           ssm_lambda_re, ssm_lambda_im, ssm_log_dt, ssm_b_re, ssm_b_im, ssm_c_re, ssm_c_im, ssm_d,
           ssm_w_glu1, ssm_w_glu2, hgrn_lb_logits, hgrn_norm_w, gdn_conv_w, gdn_a_log, gdn_dt_bias,
           gdn_norm_w, swa_sinks):
    bp, lp, _ = x_prompt.shape
    bs, ls, _ = x_sample.shape
    ls_pad = -(-ls // SUBLANES) * SUBLANES
    plan_s = _SeqPlan(bs, ls, ls_pad, seq_blk=min(bs, 16), t_blk=ls_pad, chunk=ls_pad, unroll=8)
    plans_s = {"mix": plan_s, "gdn": plan_s, "s5": plan_s, "swa": plan_s}
    plans_p = {
        "mix": _SeqPlan(bp, lp, lp, seq_blk=min(bp, 8), t_blk=min(lp, 256), chunk=min(lp, 64), unroll=8),
        "gdn": _SeqPlan(bp, lp, lp, seq_blk=min(bp, 4), t_blk=min(lp, 256), chunk=min(lp, 64)),
        "s5": _SeqPlan(bp, lp, lp, seq_blk=min(bp, SUBLANES), t_blk=min(lp, 128), chunk=min(lp, 64)),
        "swa": _SeqPlan(bp, lp, lp, seq_blk=min(bp, 4), t_blk=min(lp, 256), chunk=min(lp, SW_WINDOW))}
    consts = _constants({plans_p["mix"].chunk, plan_s.chunk})

    gam = jax.nn.softmax(hgrn_lb_logits.astype(F32), axis=0)
    lbs = jnp.cumsum(gam, axis=0) - gam[:1]
    big = {"wg": ffn_w_gate.astype(BF16), "wu": ffn_w_up.astype(BF16), "wd": ffn_w_down.astype(BF16),
           "w_in": _prep_w_in(w_in), "w_out": _prep_w_out(w_out)}
    layers = []
    for l in range(DEPTH):
        sinks = swa_sinks[l].astype(F32)[np.array(_Q_HEAD_ORDER)]
        layers.append({
            "gains": norm_gains[l].astype(F32).reshape(6, 1, D_MODEL),
            "s5": _s5_consts(ssm_lambda_re[l], ssm_lambda_im[l], ssm_log_dt[l], ssm_b_re[l], ssm_b_im[l],
                             ssm_c_re[l], ssm_c_im[l], ssm_d[l], ssm_w_glu1[l], ssm_w_glu2[l]),
            "lb": lbs[l].reshape(1, D_GROUP),
            "hgrn_nw": jnp.tile(hgrn_norm_w[l].astype(F32), N_HEADS).reshape(1, D_GROUP),
            "gdn": {"conv_w": gdn_conv_w[l].astype(F32),
                    "a_log": _lane_vec(gdn_a_log[l].astype(F32), N_HEADS),
                    "dt_bias": _lane_vec(gdn_dt_bias[l].astype(F32), N_HEADS),
                    "norm_w": jnp.tile(gdn_norm_w[l].astype(F32), N_HEADS).reshape(1, D_GROUP)},
            "sinks": sinks,
        })

    def run(x, n_seq, length, plans, state, has_cache):
        l_pad = plans["mix"].l_pad
        if l_pad > length:
            x = jnp.pad(x, ((0, 0), (0, l_pad - length), (0, 0)))
        x2 = x.reshape(n_seq * l_pad, D_MODEL)
        tm = min(512, x2.shape[0])
        y, st = _trunk(x2, state, layers, big, consts, plans, tm, has_cache)
        y = y.reshape(n_seq, l_pad, D_MODEL)[:, :length]
        return (y,) + _finish_states(st, n_seq)

    HK = SW_KV_HEADS * HEAD_DIM
    zeros = lambda *shape: jnp.zeros((1, bp) + shape, F32)
    prompt_state = {"ssm": zeros(2 * SSM_LANES), "hgrn": zeros(D_GROUP, HEAD_DIM),
                    "gdn": zeros(D_GROUP, HEAD_DIM), "conv": zeros(GD_CONV - 1, 3 * D_GROUP),
                    "swa_k": zeros(SW_WINDOW, HK), "swa_v": zeros(SW_WINDOW, HK)}
    sample_state = {
        "ssm": jnp.concatenate([state_ssm_re.reshape(DEPTH, bs, SSM_LANES),
                                state_ssm_im.reshape(DEPTH, bs, SSM_LANES)], axis=-1).astype(F32),
        "hgrn": state_hgrn.astype(F32).reshape(DEPTH, bs, D_GROUP, HEAD_DIM),
        "gdn": state_gdn.astype(F32).reshape(DEPTH, bs, D_GROUP, HEAD_DIM),
        "conv": state_gdn_conv.astype(F32),
        "swa_k": cache_swa_k.astype(F32).reshape(DEPTH, bs, SW_WINDOW, HK),
        "swa_v": cache_swa_v.astype(F32).reshape(DEPTH, bs, SW_WINDOW, HK)}

    yp, sre_p, sim_p, hg_p, gd_p, cv_p, sk_p, sv_p = run(x_prompt, bp, lp, plans_p, prompt_state, False)
    ys, sre_s, sim_s, hg_s, gd_s, cv_s, sk_s, sv_s = run(x_sample, bs, ls, plans_s, sample_state, True)
    return (yp, ys, sre_p, sre_s, sim_p, sim_s, hg_p, hg_s, gd_p, gd_s,
            cv_p, cv_s, sk_p, sk_s, sv_p, sv_s)
```

```python
import functools
import math

import numpy as np
import jax
import jax.numpy as jnp
from jax import lax
from jax.experimental import pallas as pl
from jax.experimental.pallas import tpu as pltpu

D_MODEL = 1024
DEPTH = 4
D_GROUP = 256
N_HEADS = 4
HEAD_DIM = 64
SSM_CH = 16
SSM_GROUPS = 16
SSM_STATE = 64
SSM_LANES = SSM_GROUPS * SSM_STATE
GD_CONV = 4
SW_WINDOW = 128
SW_KV_HEADS = 2
D_FF = 2816
NORM_EPS = 1e-6
PROJ_W = 12 * D_GROUP
SUBLANES = 8
LANES = 128
VMEM_LIMIT = 56 * 1024 * 1024

F32 = jnp.float32
BF16 = jnp.bfloat16
EXACT_TERMS = 3
SUMSQ_TERMS = 2


def _mm(a, b):
    return jnp.dot(a.astype(BF16), b.astype(BF16), preferred_element_type=F32)


def _mm_nt(a, b):
    return lax.dot_general(a.astype(BF16), b.astype(BF16), (((1,), (1,)), ((), ())),
                           preferred_element_type=F32)


def _mm_tn(a, b):
    return lax.dot_general(a.astype(BF16), b.astype(BF16), (((0,), (0,)), ((), ())),
                           preferred_element_type=F32)


def _mms(a, b):
    return _mm(a, b)


def _split_bf16(x, terms):
    parts = []
    for _ in range(terms - 1):
        p = x.astype(BF16)
        parts.append(p)
        x = x - p.astype(F32)
    parts.append(x.astype(BF16))
    return parts


def _sel_r(a, sel_stack, terms):
    return jnp.dot(jnp.concatenate(_split_bf16(a, terms), axis=1), sel_stack, preferred_element_type=F32)


def _sel_l(sel_stack, b, terms):
    return jnp.dot(sel_stack, jnp.concatenate(_split_bf16(b, terms), axis=0), preferred_element_type=F32)


def _sel_nt(sel_stack, b, terms):
    return lax.dot_general(sel_stack, jnp.concatenate(_split_bf16(b, terms), axis=1),
                           (((1,), (1,)), ((), ())), preferred_element_type=F32)


def _rms(x, gain):
    return x * lax.rsqrt(jnp.mean(x * x, axis=-1, keepdims=True) + NORM_EPS) * gain


def _silu(x):
    return x * jax.nn.sigmoid(x)


def _div2(x, d):
    assert d & (d - 1) == 0
    return x >> (d.bit_length() - 1)


def _mod2(x, d):
    assert d & (d - 1) == 0
    return x & (d - 1)


def _head_masks(width=D_GROUP, head_dim=HEAD_DIM):
    lane = lax.broadcasted_iota(jnp.int32, (1, width), 1)
    return [(_div2(lane, head_dim) == h).astype(F32) for h in range(width // head_dim)]


def _stack_heads(x, masks):
    return jnp.concatenate([x * m for m in masks], axis=0)


def _unstack_heads(x, n_heads):
    c = x.shape[0] // n_heads
    out = x[0:c]
    for h in range(1, n_heads):
        out = out + x[h * c:(h + 1) * c]
    return out


def _lane_col(x, lane):
    idx = lax.broadcasted_iota(jnp.int32, x.shape, 1)
    return jnp.sum(jnp.where(idx == lane, x, 0.0), axis=-1, keepdims=True)


def _const_spec(shape):
    return pl.BlockSpec(shape, lambda *_: (0,) * len(shape), pipeline_mode=pl.Buffered(1))


def _layer_spec(shape, lead):
    idx = tuple(lead) + (0,) * len(shape)
    return pl.BlockSpec((None,) * len(lead) + tuple(shape), lambda *_: idx, pipeline_mode=pl.Buffered(1))


def _params(*sem):
    return pltpu.CompilerParams(dimension_semantics=sem, vmem_limit_bytes=VMEM_LIMIT)


def _ffn_kernel(x_ref, gpre_ref, gpost_ref, wg_ref, wu_ref, wd_ref, *rest, mixed):
    x = x_ref[...]
    if mixed:
        ya_ref, yb_ref, yc_ref, yd_ref, wo_ref, gmix_ref, o_ref = rest
        y = jnp.concatenate([ya_ref[...], yb_ref[...], yc_ref[...], yd_ref[...]], axis=-1)
        x = x + _rms(jnp.dot(y.astype(BF16), wo_ref[...], preferred_element_type=F32), gmix_ref[...])
    else:
        o_ref, = rest
    h = _rms(x, gpre_ref[...]).astype(BF16)
    g = jnp.dot(h, wg_ref[...], preferred_element_type=F32)
    u = jnp.dot(h, wu_ref[...], preferred_element_type=F32)
    a = (_silu(g) * u).astype(BF16)
    y = jnp.dot(a, wd_ref[...], preferred_element_type=F32)
    o_ref[...] = x + 0.5 * _rms(y, gpost_ref[...])


def _ffn(x, gpre, gpost, wg, wu, wd, lead, tm, mix=None):
    n = x.shape[0]
    row = pl.BlockSpec((tm, D_MODEL), lambda i: (i, 0))
    in_specs = [row, _const_spec((1, D_MODEL)), _const_spec((1, D_MODEL)),
                _layer_spec((D_MODEL, D_FF), lead), _layer_spec((D_MODEL, D_FF), lead),
                _layer_spec((D_FF, D_MODEL), lead)]
    args = [x, gpre, gpost, wg, wu, wd]
    if mix is not None:
        ys, w_out, gmix = mix
        l_pad, n_seq, _ = ys[0].shape
        grp = pl.BlockSpec((tm, D_GROUP), lambda i: (i, 0))
        ya_spec = _time_major_spec(tm, l_pad)
        if ya_spec is None:
            ya, ya_spec = jnp.swapaxes(ys[0], 0, 1).reshape(n, D_GROUP), grp
        else:
            ya = ys[0].reshape(l_pad, n_seq * D_GROUP)
        in_specs += [ya_spec, grp, grp, grp, _layer_spec((D_MODEL, D_MODEL), lead[:1]), _const_spec((1, D_MODEL))]
        args += [ya, *ys[1:], w_out, gmix]
    return pl.pallas_call(
        functools.partial(_ffn_kernel, mixed=mix is not None),
        grid=(n // tm,),
        in_specs=in_specs,
        out_specs=row,
        out_shape=jax.ShapeDtypeStruct((n, D_MODEL), F32),
        compiler_params=_params("parallel"),
        name="ffn",
    )(*args)


def _time_major_spec(tm, l_pad):
    if l_pad % tm:
        return None
    per_seq = l_pad // tm
    return pl.BlockSpec((tm, D_GROUP), lambda i: (i % per_seq, i // per_seq))


def _inproj_kernel(x_ref, g_ref, w_ref, o_ref, *u_ref):
    h = _rms(x_ref[...], g_ref[...]).astype(BF16)
    proj = jnp.dot(h, w_ref[...], preferred_element_type=F32)
    o_ref[...] = proj
    if u_ref:
        u_ref[0][...] = proj[:, 0:D_GROUP]


def _inproj(x, gain, w, lead, tm, n_seq, l_pad):
    n = x.shape[0]
    out_specs = [pl.BlockSpec((tm, PROJ_W), lambda i: (i, 0))]
    out_shape = [jax.ShapeDtypeStruct((n, PROJ_W), F32)]
    u_spec = _time_major_spec(tm, l_pad)
    if u_spec is not None:
        out_specs.append(u_spec)
        out_shape.append(jax.ShapeDtypeStruct((l_pad, n_seq * D_GROUP), F32))
    outs = pl.pallas_call(
        _inproj_kernel,
        grid=(n // tm,),
        in_specs=[pl.BlockSpec((tm, D_MODEL), lambda i: (i, 0)), _const_spec((1, D_MODEL)),
                  _layer_spec((D_MODEL, PROJ_W), lead)],
        out_specs=out_specs,
        out_shape=out_shape,
        compiler_params=_params("parallel"),
        name="inproj",
    )(x, gain, w)
    proj = outs[0].reshape(n_seq, l_pad, PROJ_W)
    if u_spec is None:
        return proj, None
    return proj, outs[1].reshape(l_pad, n_seq, D_GROUP)


class _SeqPlan:
    def __init__(self, n_seq, length, l_pad, seq_blk, t_blk, chunk, unroll=4):
        assert l_pad % t_blk == 0 and t_blk % chunk == 0 and n_seq % seq_blk == 0
        assert chunk % SUBLANES == 0 and 0 <= l_pad - length < SUBLANES
        assert length % chunk == 0 or l_pad == chunk
        assert seq_blk & (seq_blk - 1) == 0
        self.n_seq, self.length, self.l_pad = n_seq, length, l_pad
        self.seq_blk, self.t_blk, self.chunk = seq_blk, t_blk, chunk
        self.rows = seq_blk * t_blk
        self.t_steps = l_pad // t_blk
        self.grid = (n_seq // seq_blk, self.t_steps)
        self.unroll = min(unroll, seq_blk)

    def rows_spec(self, width, col_block=0):
        return pl.BlockSpec((self.seq_blk, self.t_blk, width), lambda i, j: (i, j, col_block))

    def seq_spec(self, *tail, layer=None):
        zeros = (0,) * len(tail)
        if layer is None:
            return pl.BlockSpec((self.seq_blk,) + tail, lambda i, j: (i,) + zeros)
        return pl.BlockSpec((None, self.seq_blk) + tail, lambda i, j: (layer, i) + zeros)

    def rows_shape(self):
        return jax.ShapeDtypeStruct((self.n_seq, self.l_pad, D_GROUP), F32)

    def for_unit_groups(self, size, body):
        n_units = self.seq_blk * (self.t_blk // size)

        def step(n, carry):
            if self.unroll == self.seq_blk:
                body([(u, n) for u in range(self.unroll)])
            else:
                units = [n * self.unroll + u for u in range(self.unroll)]
                body([(_mod2(unit, self.seq_blk), _div2(unit, self.seq_blk)) for unit in units])
            return carry

        lax.fori_loop(0, n_units // self.unroll, step, 0)

    def for_seq_groups(self, body):
        def step(n, carry):
            body([n * self.unroll + u for u in range(self.unroll)])
            return carry
        lax.fori_loop(0, self.seq_blk // self.unroll, step, 0)

    def for_units(self, size, body):
        def group(units):
            for s, c in units:
                body(s, c)
        self.for_unit_groups(size, group)

    def valid_rows(self, j, c, size):
        t = j * self.t_blk + c * size + lax.broadcasted_iota(jnp.int32, (size, 1), 0)
        return t < self.length


def _seq_call(kernel, plan, in_specs, out_specs, out_shape, scratch, name):
    return pl.pallas_call(
        kernel, grid=plan.grid, in_specs=in_specs, out_specs=out_specs, out_shape=out_shape,
        scratch_shapes=scratch, compiler_params=_params("parallel", "arbitrary"), name=name)


def _s5_kernel(u_ref, h0_ref, bmat_ref, cmat_ref, d_ref, w1_ref, w2_ref, a_ref,
               y_ref, hn_ref, s_scr, carry_scr, *, plan):
    j = pl.program_id(1)
    P = SSM_LANES
    S = plan.seq_blk
    slab = min(SUBLANES, S)

    @pl.when(j == 0)
    def _():
        carry_scr[...] = h0_ref[...]

    u = u_ref[...].reshape(plan.rows, D_GROUP)
    s_scr[...] = _mm(u, bmat_ref[...])
    a_re, a_im = a_ref[0:slab, 0:P], a_ref[0:slab, P:2 * P]

    def step(t, x):
        out = []
        for g in range(S // slab):
            rows = pl.ds(pl.multiple_of(t * S + g * slab, slab), slab)
            x_re, x_im = x[2 * g], x[2 * g + 1]
            n_re = a_re * x_re - a_im * x_im + s_scr[rows, 0:P]
            n_im = a_re * x_im + a_im * x_re + s_scr[rows, P:2 * P]
            s_scr[rows, 0:P] = n_re
            s_scr[rows, P:2 * P] = n_im
            out += [n_re, n_im]
        return tuple(out)

    x0 = []
    for g in range(S // slab):
        x0 += [carry_scr[g * slab:(g + 1) * slab, 0:P], carry_scr[g * slab:(g + 1) * slab, P:2 * P]]
    x = lax.fori_loop(0, plan.t_blk, step, tuple(x0))
    for g in range(S // slab):
        carry_scr[g * slab:(g + 1) * slab, 0:P] = x[2 * g]
        carry_scr[g * slab:(g + 1) * slab, P:2 * P] = x[2 * g + 1]

    y = _mm(s_scr[...], cmat_ref[...]) + d_ref[...] * u
    y = jax.nn.gelu(y)
    y = _mm(y, w1_ref[...]) * jax.nn.sigmoid(_mm(y, w2_ref[...]))
    t = j * plan.t_blk + _div2(lax.broadcasted_iota(jnp.int32, (plan.rows, 1), 0), S)
    y_ref[...] = jnp.where(t < plan.length, y, 0.0).reshape(plan.t_blk, S, D_GROUP)

    @pl.when(j == plan.t_steps - 1)
    def _():
        hn_ref[...] = s_scr[((plan.length - 1) % plan.t_blk) * S:((plan.length - 1) % plan.t_blk + 1) * S, :]


def _s5(u_tm, h0, layer, consts, plan):
    P2 = 2 * SSM_LANES
    tm_spec = pl.BlockSpec((plan.t_blk, plan.seq_blk, D_GROUP), lambda i, j: (j, i, 0))
    in_specs = [tm_spec, plan.seq_spec(P2, layer=layer),
                _const_spec((D_GROUP, P2)), _const_spec((P2, D_GROUP)), _const_spec((1, D_GROUP)),
                _const_spec((D_GROUP, D_GROUP)), _const_spec((D_GROUP, D_GROUP)),
                _const_spec((SUBLANES, P2))]
    out_specs = [tm_spec, plan.seq_spec(P2)]
    out_shape = [jax.ShapeDtypeStruct((plan.l_pad, plan.n_seq, D_GROUP), F32),
                 jax.ShapeDtypeStruct((plan.n_seq, P2), F32)]
    scratch = [pltpu.VMEM((plan.rows, P2), F32), pltpu.VMEM((plan.seq_blk, P2), F32)]
    return _seq_call(functools.partial(_s5_kernel, plan=plan), plan, in_specs, out_specs, out_shape,
                     scratch, "s5")(u_tm, h0, *consts)


def _selectors(consts, chunk, transposed):
    arrays = (consts["bd"], consts["bd2"], consts["tile_tc" if transposed else "tile"], consts["tile_t"],
              consts["tri"][chunk])
    return arrays, [_const_spec(a.shape) for a in arrays]


def _load_state_bd(s0, tile_mat, bd_mask):
    return _sel_r(s0, tile_mat, EXACT_TERMS) * bd_mask


def _store_state_bd(s_bd, tile_mat_t):
    return _sel_r(s_bd, tile_mat_t, EXACT_TERMS)


def _load_state_bd_t(s0, tile_mat_tc, bd_mask):
    return _sel_nt(tile_mat_tc, s0, EXACT_TERMS) * bd_mask


def _store_state_bd_t(st_bd, tile_mat_t):
    parts = jnp.concatenate(_split_bf16(st_bd, EXACT_TERMS), axis=0)
    return lax.dot_general(parts, tile_mat_t, (((0,), (0,)), ((), ())), preferred_element_type=F32)


def _hgrn_kernel(q_ref, f_ref, i_ref, g_ref, s0_ref, lb_ref, nw_ref,
                 bd_ref, bd2_ref, tile_ref, tilet_ref, tri_ref,
                 y_ref, sn_ref, st_scr, *, plan, sub):
    j = pl.program_id(1)
    C = plan.chunk
    masks = _head_masks()
    bd = bd_ref[...]
    lb = lb_ref[...]

    @pl.when(j == 0)
    def _():
        def init(seqs):
            for s, st in zip(seqs, [_load_state_bd_t(s0_ref[s], tile_ref[...], bd) for s in seqs]):
                st_scr[s] = st
        plan.for_seq_groups(init)

    stacked_mask = jnp.concatenate([jnp.broadcast_to(m, (sub, D_GROUP)) for m in masks], axis=0)

    def gated_inputs(s, c, valid):
        rows = pl.ds(pl.multiple_of(c * C, C), C)
        fr = f_ref[s, rows, :]
        f = lb + (1.0 - lb) * jax.nn.sigmoid(fr)
        k = jnp.where(valid, (1.0 - lb) * jax.nn.sigmoid(-fr), 0.0)
        logf = jnp.where(valid, jnp.log(f), 0.0)
        return _silu(q_ref[s, rows, :]), k, i_ref[s, rows, :], logf

    def scores(q, k, cum, blk):
        r0, r1 = blk * sub, (blk + 1) * sub
        c0 = cum[r0 - 1:r0] if blk else jnp.zeros_like(cum[0:1])
        qs = _stack_heads(q[r0:r1] * jnp.exp(cum[r0:r1] - c0), masks)
        att = _mm_nt(qs, k[0:r1] * jnp.exp(c0 - cum[0:r1]))
        t_idx = r0 + _mod2(lax.broadcasted_iota(jnp.int32, att.shape, 0), sub)
        s_idx = lax.broadcasted_iota(jnp.int32, att.shape, 1)
        return jnp.where(s_idx <= t_idx, att, 0.0)

    def chunk(units):
        each = lambda fn, *xs: [fn(*a) for a in zip(*xs)]
        valid = [plan.valid_rows(j, c, C) for _, c in units]
        qkvl = [gated_inputs(s, c, ok) for (s, c), ok in zip(units, valid)]
        q, k, v, logf = ([t[i] for t in qkvl] for i in range(4))
        cum = each(lambda x: _sel_l(tri_ref[...], x, EXACT_TERMS), logf)
        st = [st_scr[s] for s, _ in units]
        o = each(lambda a, b, x: _mm_nt(a * jnp.exp(b), x), q, cum, st)
        intra = [[] for _ in units]
        for blk in range(C // sub):
            att = each(lambda a, b, d: scores(a, b, d, blk), q, k, cum)
            pv = each(lambda a, x: _mm(a, x[0:(blk + 1) * sub]), att, v)
            for lst, x in zip(intra, pv):
                lst.append(_unstack_heads(x * stacked_mask, N_HEADS))
        o = each(lambda a, lst: a + jnp.concatenate(lst, axis=0), o, intra)
        upd = each(lambda x, b, d: _mm_tn(x, b * jnp.exp(d[C - 1:C] - d)) * bd, v, k, cum)
        for (s, _), x, d, u in zip(units, st, cum, upd):
            st_scr[s] = x * jnp.exp(d[C - 1:C]) + u
        ms = each(lambda x: _sel_r(x * x, bd2_ref[...], SUMSQ_TERMS) * (1.0 / HEAD_DIM), o)
        for (s, c), x, ss, ok in zip(units, o, ms, valid):
            rows = pl.ds(pl.multiple_of(c * C, C), C)
            x = x * lax.rsqrt(ss + NORM_EPS) * nw_ref[...] * _silu(g_ref[s, rows, :])
            y_ref[s, rows, :] = jnp.where(ok, x, 0.0)

    plan.for_unit_groups(C, chunk)

    @pl.when(j == plan.t_steps - 1)
    def _():
        def fin(seqs):
            for s, st in zip(seqs, [_store_state_bd_t(st_scr[s], tilet_ref[...]) for s in seqs]):
                sn_ref[s] = st
        plan.for_seq_groups(fin)


def _hgrn(proj, s0, layer, lb, norm_w, consts, plan):
    C = plan.chunk
    sub = min(16, C)
    selectors, selector_specs = _selectors(consts, C, transposed=True)
    in_specs = [plan.rows_spec(D_GROUP, 1), plan.rows_spec(D_GROUP, 2), plan.rows_spec(D_GROUP, 3),
                plan.rows_spec(D_GROUP, 4), plan.seq_spec(D_GROUP, HEAD_DIM, layer=layer),
                _const_spec((1, D_GROUP)), _const_spec((1, D_GROUP))] + selector_specs
    out_specs = [plan.rows_spec(D_GROUP), plan.seq_spec(D_GROUP, HEAD_DIM)]
    out_shape = [plan.rows_shape(), jax.ShapeDtypeStruct((plan.n_seq, D_GROUP, HEAD_DIM), F32)]
    scratch = [pltpu.VMEM((plan.seq_blk, D_GROUP, D_GROUP), F32)]
    kern = functools.partial(_hgrn_kernel, plan=plan, sub=sub)
    return _seq_call(kern, plan, in_specs, out_specs, out_shape, scratch, "hgrn2")(
        proj, proj, proj, proj, s0, lb, norm_w, *selectors)


def _gdn_kernel(q_ref, k_ref, v_ref, z_ref, sc_ref, s0_ref, cv0_ref, cw_ref, alog_ref, dtb_ref, nw_ref,
                bd_ref, bd2_ref, tile_ref, tilet_ref, tri_ref, eb_ref, ea_ref, lvl0_ref, lvl_ref,
                y_ref, sn_ref, cvn_ref, st_scr, cx_scr, prev_scr, *, plan):
    j = pl.program_id(1)
    C = plan.chunk
    H = N_HEADS
    HC = H * C
    masks = _head_masks()
    bd = bd_ref[...]
    lane = lax.broadcasted_iota(jnp.int32, (1, D_GROUP), 1)
    beta_lanes = lane < H
    a_lanes = (lane >= H) & (lane < 2 * H)
    neg_rate = jnp.where(a_lanes, -jnp.exp(alog_ref[...]), 0.0)
    n_lvl = max(1, int(math.ceil(math.log2(C))))
    valid_last = (plan.length - 1) % C + 1
    assert valid_last >= GD_CONV - 1
    pad0 = SUBLANES - (GD_CONV - 1)

    @pl.when(j == 0)
    def _():
        def init(seqs):
            for s, st in zip(seqs, [_load_state_bd(s0_ref[s], tile_ref[...], bd) for s in seqs]):
                st_scr[s] = st
                prev_scr[s] = jnp.zeros((SUBLANES, 3 * D_GROUP), F32)
                prev_scr[s, pl.ds(pad0, GD_CONV - 1), :] = cv0_ref[s]
        plan.for_seq_groups(init)

    row_i = lax.broadcasted_iota(jnp.int32, (HC, HC), 0)
    col_i = lax.broadcasted_iota(jnp.int32, (HC, HC), 1)
    same_head = _div2(row_i, C) == _div2(col_i, C)
    incl = same_head & (col_i <= row_i)
    strict = same_head & (col_i < row_i)
    eye = (row_i == col_i).astype(F32)

    def head_rows(gc_all):
        first_tile = gc_all[:, 0:LANES]
        if C < LANES:
            first_tile = jnp.concatenate([first_tile, jnp.zeros((LANES - C, LANES), F32)], axis=0)
        tr = first_tile.T
        return jnp.concatenate([tr[H + h:H + h + 1, 0:C] for h in range(H)], axis=1)

    def conv_qkv(s, c):
        rows = pl.ds(pl.multiple_of(c * C, C), C)
        cx = cx_scr.at[s]
        cx[0:SUBLANES, :] = prev_scr[s]
        cx[SUBLANES:SUBLANES + C, 0:D_GROUP] = q_ref[s, rows, :]
        cx[SUBLANES:SUBLANES + C, D_GROUP:2 * D_GROUP] = k_ref[s, rows, :]
        cx[SUBLANES:SUBLANES + C, 2 * D_GROUP:3 * D_GROUP] = v_ref[s, rows, :]
        cw = cw_ref[...]
        conv = cx[pad0:pad0 + C, :] * cw[0:1]
        for tap in range(1, GD_CONV):
            conv = conv + cx[pad0 + tap:pad0 + tap + C, :] * cw[tap:tap + 1]
        prev_scr[s, pl.ds(pad0, GD_CONV - 1), :] = cx[SUBLANES + C - (GD_CONV - 1):SUBLANES + C, :]
        cvn_ref[s] = cx[SUBLANES + valid_last - (GD_CONV - 1):SUBLANES + valid_last, :]
        conv = _silu(conv)
        return conv[:, 0:D_GROUP], conv[:, D_GROUP:2 * D_GROUP], conv[:, 2 * D_GROUP:3 * D_GROUP]

    def gates(s, c, valid):
        sc = sc_ref[s, pl.ds(pl.multiple_of(c * C, C), C), :]
        beta_all = jnp.where(valid & beta_lanes, jax.nn.sigmoid(sc), 0.0)
        g_all = jnp.where(valid, neg_rate * jax.nn.softplus(sc + dtb_ref[...]), 0.0)
        return beta_all, g_all

    def decay_matrix(gc_all, gc_row):
        gc_col = jnp.concatenate([_lane_col(gc_all, H + h) for h in range(H)], axis=0)
        return jnp.exp(jnp.where(incl, gc_col - gc_row, -jnp.inf))

    def chunk(units):
        each = lambda f, *xs: [f(*a) for a in zip(*xs)]
        valid = [plan.valid_rows(j, c, C) for _, c in units]
        qkv = [conv_qkv(s, c) for s, c in units]
        q, k, v = ([t[i] for t in qkv] for i in range(3))
        qss = each(lambda x: _sel_r(x * x, bd2_ref[...], SUMSQ_TERMS), q)
        kss = each(lambda x: _sel_r(x * x, bd2_ref[...], SUMSQ_TERMS), k)
        q = each(lambda x, ss: x * lax.rsqrt(ss + NORM_EPS) * (HEAD_DIM ** -0.5), q, qss)
        k = each(lambda x, ss, ok: jnp.where(ok, x * lax.rsqrt(ss + NORM_EPS), 0.0), k, kss, valid)

        bg = [gates(s, c, ok) for (s, c), ok in zip(units, valid)]
        beta_all, g_all = [t[0] for t in bg], [t[1] for t in bg]
        gc_all = each(lambda g: _sel_l(tri_ref[...], g, EXACT_TERMS), g_all)
        beta_l = each(lambda b: _sel_r(b, eb_ref[...], EXACT_TERMS), beta_all)
        gc_l = each(lambda g: _sel_r(g, ea_ref[...], EXACT_TERMS), gc_all)
        gam_l = each(jnp.exp, gc_l)
        gc_row = each(head_rows, gc_all)
        dec = each(decay_matrix, gc_all, gc_row)
        beta_col = each(lambda b: jnp.concatenate([_lane_col(b, h) for h in range(H)], axis=0), beta_all)

        ks = each(lambda x: _stack_heads(x, masks), k)
        qs = each(lambda x: _stack_heads(x, masks), q)
        kk = each(_mm_nt, ks, ks)
        m = each(lambda d, b, x: jnp.where(strict, d, 0.0) * b * x, dec, beta_col, kk)
        t_inv = each(lambda x: eye - x * lvl0_ref[...], m)
        m_b = each(lambda x: x.astype(BF16), m)
        for lvl in range(1, n_lvl):
            t_b = each(lambda t: t.astype(BF16), t_inv)
            half = each(lambda t, x: _mms(t, x * lvl_ref[lvl - 1]), t_b, m_b)
            t_inv = each(lambda t, hf, tb: t - _mms(hf, tb), t_inv, half, t_b)
        t_b = each(lambda t: t.astype(BF16), t_inv)
        rhs_w = each(lambda b, g, x: _stack_heads(b * g * x, masks), beta_l, gam_l, k)
        rhs_u = each(lambda b, x: _stack_heads(b * x, masks), beta_l, v)
        w = each(lambda t, r: _unstack_heads(_mms(t, r), H), t_b, rhs_w)
        u0 = each(lambda t, r: _unstack_heads(_mms(t, r), H), t_b, rhs_u)
        qk = each(lambda a, b, d: _mm_nt(a, b) * d, qs, ks, dec)

        st = [st_scr[s] for s, _ in units]
        u = each(lambda a, b, x: a - _mm(b, x), u0, w, st)
        o_st = each(lambda x, g, y: _mm(x * g, y), q, gam_l, st)
        o_in = each(lambda a, b: _unstack_heads(_mm(a, _stack_heads(b, masks)), H), qk, u)
        upd = each(lambda x, g, b: _mm_tn(x * jnp.exp(g[C - 1:C] - g), b) * bd, k, gc_l, u)
        for (s, _), g, x, d in zip(units, gc_l, st, upd):
            st_scr[s] = jnp.exp(g[C - 1:C]) * x + d

        o = each(jnp.add, o_st, o_in)
        ms = each(lambda x: _sel_r(x * x, bd2_ref[...], SUMSQ_TERMS) * (1.0 / HEAD_DIM), o)
        for (s, c), x, ss, ok in zip(units, o, ms, valid):
            rows = pl.ds(pl.multiple_of(c * C, C), C)
            x = x * lax.rsqrt(ss + NORM_EPS) * nw_ref[...] * _silu(z_ref[s, rows, :])
            y_ref[s, rows, :] = jnp.where(ok, x, 0.0)

    plan.for_unit_groups(C, chunk)

    @pl.when(j == plan.t_steps - 1)
    def _():
        def fin(seqs):
            for s, st in zip(seqs, [_store_state_bd(st_scr[s], tilet_ref[...]) for s in seqs]):
                sn_ref[s] = st
        plan.for_seq_groups(fin)


def _gdn(proj, s0, cv0, layer, lw, consts, plan):
    C = plan.chunk
    W3 = 3 * D_GROUP
    in_specs = [plan.rows_spec(D_GROUP, 5), plan.rows_spec(D_GROUP, 6), plan.rows_spec(D_GROUP, 7),
                plan.rows_spec(D_GROUP, 8), plan.rows_spec(D_GROUP, 11),
                plan.seq_spec(D_GROUP, HEAD_DIM, layer=layer), plan.seq_spec(GD_CONV - 1, W3, layer=layer),
                _const_spec((GD_CONV, W3)), _const_spec((1, D_GROUP)), _const_spec((1, D_GROUP)),
                _const_spec((1, D_GROUP))] + _selectors(consts, C, transposed=False)[1] + [
                _const_spec((EXACT_TERMS * D_GROUP, D_GROUP)), _const_spec((EXACT_TERMS * D_GROUP, D_GROUP)),
                _const_spec(consts["solve"][C][0].shape), _const_spec(consts["solve"][C][1].shape)]
    out_specs = [plan.rows_spec(D_GROUP), plan.seq_spec(D_GROUP, HEAD_DIM), plan.seq_spec(GD_CONV - 1, W3)]
    out_shape = [plan.rows_shape(),
                 jax.ShapeDtypeStruct((plan.n_seq, D_GROUP, HEAD_DIM), F32),
                 jax.ShapeDtypeStruct((plan.n_seq, GD_CONV - 1, W3), F32)]
    scratch = [pltpu.VMEM((plan.seq_blk, D_GROUP, D_GROUP), F32),
               pltpu.VMEM((plan.seq_blk, SUBLANES + C, W3), F32),
               pltpu.VMEM((plan.seq_blk, SUBLANES, W3), F32)]
    kern = functools.partial(_gdn_kernel, plan=plan)
    return _seq_call(kern, plan, in_specs, out_specs, out_shape, scratch, "gdn")(
        proj, proj, proj, proj, proj, s0, cv0, lw["conv_w"], lw["a_log"], lw["dt_bias"], lw["norm_w"],
        *_selectors(consts, C, transposed=False)[0], consts["eb"], consts["ea"], *consts["solve"][C])


def _swa_kernel(q_ref, kv_ref, ck_ref, cv_ref, bias_ref, sink_ref, y_ref, kn_ref, vn_ref,
                kk_scr, vv_scr, *, plan, has_cache):
    j = pl.program_id(1)
    QB = plan.chunk
    W = SW_WINDOW
    HK = SW_KV_HEADS * HEAD_DIM
    kv_masks = _head_masks(HK)
    valid_last = (plan.length - 1) % QB + 1

    @pl.when(j == 0)
    def _():
        kn_ref[...] = ck_ref[...]
        vn_ref[...] = cv_ref[...]

    def stacked_queries(s, c):
        rows = pl.ds(pl.multiple_of(c * QB, QB), QB)
        kk, vv = kk_scr.at[s], vv_scr.at[s]
        kk[0:W, :] = kn_ref[s]
        vv[0:W, :] = vn_ref[s]
        kv = kv_ref[s, rows, :]
        kk[W:W + QB, :] = kv[:, 0:HK]
        vv[W:W + QB, :] = kv[:, HK:2 * HK]
        q = q_ref[s, rows, :]
        return jnp.concatenate([q[:, 0:HK] * kv_masks[0], q[:, 0:HK] * kv_masks[1],
                                q[:, HK:2 * HK] * kv_masks[0], q[:, HK:2 * HK] * kv_masks[1]], axis=0)

    def probabilities(sc, c):
        sc = sc * (HEAD_DIM ** -0.5) + bias_ref[...]
        if not has_cache:
            t0 = j * plan.t_blk + c * QB
            col = lax.broadcasted_iota(jnp.int32, sc.shape, 1)
            sc = jnp.where(t0 + col < W, -jnp.inf, sc)
        sink = sink_ref[...]
        mx = jnp.maximum(jnp.max(sc, axis=-1, keepdims=True), sink)
        return jnp.exp(sc - mx), jnp.exp(sink - mx)

    ones_cols = jnp.ones((W + QB, HK), BF16)

    def attend(p, sink_p, s):
        acc = _mm(p, jnp.concatenate([vv_scr[s].astype(BF16), ones_cols], axis=1))
        return acc[:, 0:HK] / (acc[:, HK:2 * HK] + sink_p)

    def block(units):
        qs = [stacked_queries(s, c) for s, c in units]
        sc = [_mm_nt(x, kk_scr[s]) for x, (s, _) in zip(qs, units)]
        p = [probabilities(x, c) for x, (_, c) in zip(sc, units)]
        pv = [attend(x, sp, s) for (x, sp), (s, _) in zip(p, units)]
        for x, (s, c) in zip(pv, units):
            oa = x[0:QB] * kv_masks[0] + x[QB:2 * QB] * kv_masks[1]
            ob = x[2 * QB:3 * QB] * kv_masks[0] + x[3 * QB:4 * QB] * kv_masks[1]
            rows = pl.ds(pl.multiple_of(c * QB, QB), QB)
            y_ref[s, rows, :] = jnp.where(plan.valid_rows(j, c, QB), jnp.concatenate([oa, ob], axis=-1), 0.0)
            kn_ref[s] = kk_scr[s, valid_last:valid_last + W, :]
            vn_ref[s] = vv_scr[s, valid_last:valid_last + W, :]

    plan.for_unit_groups(QB, block)


def _swa(proj, cache_k, cache_v, layer, bias, sink_col, plan, has_cache):
    QB = plan.chunk
    HK = SW_KV_HEADS * HEAD_DIM
    NK = SW_WINDOW + QB
    in_specs = [plan.rows_spec(D_GROUP, 9), plan.rows_spec(D_GROUP, 10),
                plan.seq_spec(SW_WINDOW, HK, layer=layer), plan.seq_spec(SW_WINDOW, HK, layer=layer),
                _const_spec((4 * QB, NK)), _const_spec((4 * QB, 1))]
    out_specs = [plan.rows_spec(D_GROUP), plan.seq_spec(SW_WINDOW, HK), plan.seq_spec(SW_WINDOW, HK)]
    out_shape = [plan.rows_shape(),
                 jax.ShapeDtypeStruct((plan.n_seq, SW_WINDOW, HK), F32),
                 jax.ShapeDtypeStruct((plan.n_seq, SW_WINDOW, HK), F32)]
    scratch = [pltpu.VMEM((plan.seq_blk, NK, HK), F32), pltpu.VMEM((plan.seq_blk, NK, HK), F32)]
    kern = functools.partial(_swa_kernel, plan=plan, has_cache=has_cache)
    return _seq_call(kern, plan, in_specs, out_specs, out_shape, scratch, "swa")(
        proj, proj, cache_k, cache_v, bias, sink_col)


_Q_HEAD_ORDER = (0, 2, 1, 3)


def _swa_bias(qb):
    w = SW_WINDOW
    i = np.arange(qb)[:, None]
    jj = np.arange(w + qb)[None, :]
    dist = w + i - jj
    ok = (dist >= 0) & (dist <= w)
    slopes = 2.0 ** (-8.0 * np.arange(1, N_HEADS + 1) / N_HEADS)
    blocks = []
    for g in range(2):
        for kv in range(SW_KV_HEADS):
            head = kv * 2 + g
            blocks.append(np.where(ok, -slopes[head] * dist, -np.inf))
    return jnp.asarray(np.concatenate(blocks, axis=0), F32)


def _constants(chunks):
    lane_head = np.arange(D_GROUP) // HEAD_DIM
    bd = (lane_head[:, None] == lane_head[None, :]).astype(np.float32)
    tile = np.tile(np.eye(HEAD_DIM, dtype=np.float32), (1, N_HEADS))
    eb = np.zeros((D_GROUP, D_GROUP), np.float32)
    ea = np.zeros((D_GROUP, D_GROUP), np.float32)
    for h in range(N_HEADS):
        eb[h, lane_head == h] = 1.0
        ea[N_HEADS + h, lane_head == h] = 1.0
    def solve_levels(c):
        blk = np.arange(N_HEADS * c) >> 1
        out = [blk[:, None] == blk[None, :]]
        while (1 << len(out)) < c:
            out.append(((blk >> 1)[:, None] == (blk >> 1)[None, :]) & (blk[:, None] != blk[None, :]))
            blk = blk >> 1
        return jnp.asarray(out[0].astype(np.float32)), jnp.asarray(np.stack(out[1:]).astype(np.float32), BF16)

    rows = lambda m, n: jnp.asarray(np.tile(m, (n, 1)), BF16)
    cols = lambda m, n: jnp.asarray(np.tile(m, (1, n)), BF16)
    return {"bd": jnp.asarray(bd), "bd2": rows(bd, SUMSQ_TERMS),
            "tile": rows(tile, EXACT_TERMS), "tile_t": rows(tile.T, EXACT_TERMS), "tile_tc": cols(tile.T, EXACT_TERMS),
            "eb": rows(eb, EXACT_TERMS), "ea": rows(ea, EXACT_TERMS),
            "tri": {c: cols(np.tril(np.ones((c, c), np.float32)), EXACT_TERMS) for c in chunks},
            "solve": {c: solve_levels(c) for c in chunks}}


def _s5_consts(lam_re, lam_im, log_dt, b_re, b_im, c_re, c_im, d, w1, w2):
    dt = jnp.exp(log_dt)[:, None]
    mag = jnp.exp(lam_re * dt)
    ang = lam_im * dt
    a_re, a_im = mag * jnp.cos(ang), mag * jnp.sin(ang)
    den = lam_re * lam_re + lam_im * lam_im
    z_re = ((a_re - 1.0) * lam_re + a_im * lam_im) / den
    z_im = (a_im * lam_re - (a_re - 1.0) * lam_im) / den
    bb_re = z_re[..., None] * b_re - z_im[..., None] * b_im
    bb_im = z_re[..., None] * b_im + z_im[..., None] * b_re
    eye = jnp.eye(SSM_GROUPS, dtype=F32)
    to_b = lambda t: jnp.einsum("gpc,gh->gchp", t, eye).reshape(D_GROUP, SSM_LANES)
    to_c = lambda t: jnp.einsum("gcp,gh->gphc", t, eye).reshape(SSM_LANES, D_GROUP)
    bmat = jnp.concatenate([to_b(bb_re), to_b(bb_im)], axis=1).astype(BF16)
    cmat = jnp.concatenate([to_c(c_re), -to_c(c_im)], axis=0).astype(BF16)
    a_row = jnp.concatenate([a_re.reshape(1, SSM_LANES), a_im.reshape(1, SSM_LANES)], axis=1)
    return (bmat, cmat, d.reshape(1, D_GROUP), w1.astype(BF16), w2.astype(BF16),
            jnp.broadcast_to(a_row, (SUBLANES, 2 * SSM_LANES)))


def _prep_w_in(w):
    q_d = w[:, :, 2312:2568].reshape(DEPTH, D_MODEL, N_HEADS, HEAD_DIM)[:, :, np.array(_Q_HEAD_ORDER)]
    scal = jnp.pad(w[:, :, 2304:2312], ((0, 0), (0, 0), (0, D_GROUP - 2 * N_HEADS)))
    return jnp.concatenate([w[:, :, 0:2304], q_d.reshape(DEPTH, D_MODEL, D_GROUP), w[:, :, 2568:2824], scal],
                           axis=2).astype(BF16)


def _prep_w_out(w):
    d_rows = w[:, 3 * D_GROUP:].reshape(DEPTH, N_HEADS, HEAD_DIM, D_MODEL)[:, np.array(_Q_HEAD_ORDER)]
    return jnp.concatenate([w[:, :3 * D_GROUP], d_rows.reshape(DEPTH, D_GROUP, D_MODEL)], axis=1).astype(BF16)


def _lane_vec(vals, first_lane):
    return jnp.zeros((1, D_GROUP), F32).at[0, first_lane:first_lane + vals.shape[0]].set(vals)


def _trunk(x, state, layers, big, consts, plans, tm, has_cache):
    plan, plan5, plan_w = plans["mix"], plans["s5"], plans["swa"]
    bias = _swa_bias(plan_w.chunk)
    outs = {k: [] for k in ("ssm", "hgrn", "gdn", "conv", "swa_k", "swa_v")}
    for l, lw in enumerate(layers):
        sink_col = jnp.repeat(lw["sinks"], plan_w.chunk).reshape(4 * plan_w.chunk, 1)
        gn = lw["gains"]
        x = _ffn(x, gn[0], gn[1], big["wg"], big["wu"], big["wd"], (l, 0), tm)
        proj, u_tm = _inproj(x, gn[2], big["w_in"], (l,), tm, plan.n_seq, plan.l_pad)
        if u_tm is None:
            u_tm = jnp.swapaxes(proj[:, :, 0:D_GROUP], 0, 1)
        sl = l if has_cache else 0
        y_tm, ssm = _s5(u_tm, state["ssm"], sl, lw["s5"], plan5)
        y_b, hg = _hgrn(proj, state["hgrn"], sl, lw["lb"], lw["hgrn_nw"], consts, plan)
        y_c, gd, cv = _gdn(proj, state["gdn"], state["conv"], sl, lw["gdn"], consts, plans["gdn"])
        y_d, ck, cvv = _swa(proj, state["swa_k"], state["swa_v"], sl, bias, sink_col, plan_w, has_cache)
        ys = (y_tm,) + tuple(y.reshape(x.shape[0], D_GROUP) for y in (y_b, y_c, y_d))
        x = _ffn(x, gn[4], gn[5], big["wg"], big["wu"], big["wd"], (l, 1), tm, mix=(ys, big["w_out"], gn[3]))
        for k, val in zip(outs, (ssm, hg, gd, cv, ck, cvv)):
            outs[k].append(val)
    return x, {k: jnp.stack(v, axis=0) for k, v in outs.items()}


def _finish_states(st, n_seq):
    ssm = st["ssm"].reshape(DEPTH, n_seq, 2, SSM_GROUPS, SSM_STATE)
    hg = st["hgrn"].reshape(DEPTH, n_seq, N_HEADS, HEAD_DIM, HEAD_DIM)
    gd = st["gdn"].reshape(DEPTH, n_seq, N_HEADS, HEAD_DIM, HEAD_DIM)
    sk = st["swa_k"].reshape(DEPTH, n_seq, SW_WINDOW, SW_KV_HEADS, HEAD_DIM)
    sv = st["swa_v"].reshape(DEPTH, n_seq, SW_WINDOW, SW_KV_HEADS, HEAD_DIM)
    return ssm[:, :, 0], ssm[:, :, 1], hg, gd, st["conv"], sk, sv


def kernel(x_prompt, x_sample, state_ssm_re, state_ssm_im, state_hgrn, state_gdn, state_gdn_conv,
           cache_swa_k, cache_swa_v, norm_gains, ffn_w_gate, ffn_w_up, ffn_w_down, w_in, w_out,
           ssm_lambda_re, ssm_lambda_im, ssm_log_dt, ssm_b_re, ssm_b_im, ssm_c_re, ssm_c_im, ssm_d,
           ssm_w_glu1, ssm_w_glu2, hgrn_lb_logits, hgrn_norm_w, gdn_conv_w, gdn_a_log, gdn_dt_bias,
           gdn_norm_w, swa_sinks):
    bp, lp, _ = x_prompt.shape
    bs, ls, _ = x_sample.shape
    ls_pad = -(-ls // SUBLANES) * SUBLANES
    plan_s = _SeqPlan(bs, ls, ls_pad, seq_blk=min(bs, 16), t_blk=ls_pad, chunk=ls_pad, unroll=16)
    plans_s = {"mix": plan_s, "gdn": plan_s, "s5": plan_s, "swa": plan_s}
    plans_p = {
        "mix": _SeqPlan(bp, lp, lp, seq_blk=min(bp, 8), t_blk=min(lp, 256), chunk=min(lp, 64), unroll=8),
        "gdn": _SeqPlan(bp, lp, lp, seq_blk=min(bp, 4), t_blk=min(lp, 256), chunk=min(lp, 64)),
        "s5": _SeqPlan(bp, lp, lp, seq_blk=min(bp, SUBLANES), t_blk=min(lp, 128), chunk=min(lp, 64)),
        "swa": _SeqPlan(bp, lp, lp, seq_blk=min(bp, 4), t_blk=min(lp, 256), chunk=min(lp, SW_WINDOW))}
    consts = _constants({plans_p["mix"].chunk, plan_s.chunk})

    gam = jax.nn.softmax(hgrn_lb_logits.astype(F32), axis=0)
    lbs = jnp.cumsum(gam, axis=0) - gam[:1]
    big = {"wg": ffn_w_gate.astype(BF16), "wu": ffn_w_up.astype(BF16), "wd": ffn_w_down.astype(BF16),
           "w_in": _prep_w_in(w_in), "w_out": _prep_w_out(w_out)}
    layers = []
    for l in range(DEPTH):
        sinks = swa_sinks[l].astype(F32)[np.array(_Q_HEAD_ORDER)]
        layers.append({
            "gains": norm_gains[l].astype(F32).reshape(6, 1, D_MODEL),
            "s5": _s5_consts(ssm_lambda_re[l], ssm_lambda_im[l], ssm_log_dt[l], ssm_b_re[l], ssm_b_im[l],
                             ssm_c_re[l], ssm_c_im[l], ssm_d[l], ssm_w_glu1[l], ssm_w_glu2[l]),
            "lb": lbs[l].reshape(1, D_GROUP),
            "hgrn_nw": jnp.tile(hgrn_norm_w[l].astype(F32), N_HEADS).reshape(1, D_GROUP),
            "gdn": {"conv_w": gdn_conv_w[l].astype(F32),
                    "a_log": _lane_vec(gdn_a_log[l].astype(F32), N_HEADS),
                    "dt_bias": _lane_vec(gdn_dt_bias[l].astype(F32), N_HEADS),
                    "norm_w": jnp.tile(gdn_norm_w[l].astype(F32), N_HEADS).reshape(1, D_GROUP)},
            "sinks": sinks,
        })

    def run(x, n_seq, length, plans, state, has_cache):
        l_pad = plans["mix"].l_pad
        if l_pad > length:
            x = jnp.pad(x, ((0, 0), (0, l_pad - length), (0, 0)))
        x2 = x.reshape(n_seq * l_pad, D_MODEL)
        tm = min(512, x2.shape[0])
        y, st = _trunk(x2, state, layers, big, consts, plans, tm, has_cache)
        y = y.reshape(n_seq, l_pad, D_MODEL)[:, :length]
        return (y,) + _finish_states(st, n_seq)

    HK = SW_KV_HEADS * HEAD_DIM
    zeros = lambda *shape: jnp.zeros((1, bp) + shape, F32)
    prompt_state = {"ssm": zeros(2 * SSM_LANES), "hgrn": zeros(D_GROUP, HEAD_DIM),
                    "gdn": zeros(D_GROUP, HEAD_DIM), "conv": zeros(GD_CONV - 1, 3 * D_GROUP),
                    "swa_k": zeros(SW_WINDOW, HK), "swa_v": zeros(SW_WINDOW, HK)}
    sample_state = {
        "ssm": jnp.concatenate([state_ssm_re.reshape(DEPTH, bs, SSM_LANES),
                                state_ssm_im.reshape(DEPTH, bs, SSM_LANES)], axis=-1).astype(F32),
        "hgrn": state_hgrn.astype(F32).reshape(DEPTH, bs, D_GROUP, HEAD_DIM),
        "gdn": state_gdn.astype(F32).reshape(DEPTH, bs, D_GROUP, HEAD_DIM),
        "conv": state_gdn_conv.astype(F32),
        "swa_k": cache_swa_k.astype(F32).reshape(DEPTH, bs, SW_WINDOW, HK),
        "swa_v": cache_swa_v.astype(F32).reshape(DEPTH, bs, SW_WINDOW, HK)}

    yp, sre_p, sim_p, hg_p, gd_p, cv_p, sk_p, sv_p = run(x_prompt, bp, lp, plans_p, prompt_state, False)
    ys, sre_s, sim_s, hg_s, gd_s, cv_s, sk_s, sv_s = run(x_sample, bs, ls, plans_s, sample_state, True)
    return (yp, ys, sre_p, sre_s, sim_p, sim_s, hg_p, hg_s, gd_p, gd_s,
            cv_p, cv_s, sk_p, sk_s, sv_p, sv_s)
```

```python
import functools
import math

import numpy as np
import jax
import jax.numpy as jnp
from jax import lax
from jax.experimental import pallas as pl
from jax.experimental.pallas import tpu as pltpu

D_MODEL = 1024
DEPTH = 4
D_GROUP = 256
N_HEADS = 4
HEAD_DIM = 64
SSM_CH = 16
SSM_GROUPS = 16
SSM_STATE = 64
SSM_LANES = SSM_GROUPS * SSM_STATE
GD_CONV = 4
SW_WINDOW = 128
SW_KV_HEADS = 2
D_FF = 2816
NORM_EPS = 1e-6
PROJ_W = 12 * D_GROUP
SUBLANES = 8
LANES = 128
VMEM_LIMIT = 56 * 1024 * 1024

F32 = jnp.float32
BF16 = jnp.bfloat16
EXACT_TERMS = 3
SUMSQ_TERMS = 2


def _mm(a, b):
    return jnp.dot(a.astype(BF16), b.astype(BF16), preferred_element_type=F32)


def _mm_nt(a, b):
    return lax.dot_general(a.astype(BF16), b.astype(BF16), (((1,), (1,)), ((), ())),
                           preferred_element_type=F32)


def _mm_tn(a, b):
    return lax.dot_general(a.astype(BF16), b.astype(BF16), (((0,), (0,)), ((), ())),
                           preferred_element_type=F32)


def _mms(a, b):
    return _mm(a, b)


def _split_bf16(x, terms):
    parts = []
    for _ in range(terms - 1):
        p = x.astype(BF16)
        parts.append(p)
        x = x - p.astype(F32)
    parts.append(x.astype(BF16))
    return parts


def _sel_r(a, sel_stack, terms):
    return jnp.dot(jnp.concatenate(_split_bf16(a, terms), axis=1), sel_stack, preferred_element_type=F32)


def _sel_l(sel_stack, b, terms):
    return jnp.dot(sel_stack, jnp.concatenate(_split_bf16(b, terms), axis=0), preferred_element_type=F32)


def _sel_nt(sel_stack, b, terms):
    return lax.dot_general(sel_stack, jnp.concatenate(_split_bf16(b, terms), axis=1),
                           (((1,), (1,)), ((), ())), preferred_element_type=F32)


def _rms(x, gain):
    return x * lax.rsqrt(jnp.mean(x * x, axis=-1, keepdims=True) + NORM_EPS) * gain


def _silu(x):
    return x * jax.nn.sigmoid(x)


def _div2(x, d):
    assert d & (d - 1) == 0
    return x >> (d.bit_length() - 1)


def _mod2(x, d):
    assert d & (d - 1) == 0
    return x & (d - 1)


def _head_masks(width=D_GROUP, head_dim=HEAD_DIM):
    lane = lax.broadcasted_iota(jnp.int32, (1, width), 1)
    return [(_div2(lane, head_dim) == h).astype(F32) for h in range(width // head_dim)]


def _stack_heads(x, masks):
    return jnp.concatenate([x * m for m in masks], axis=0)


def _unstack_heads(x, n_heads):
    c = x.shape[0] // n_heads
    out = x[0:c]
    for h in range(1, n_heads):
        out = out + x[h * c:(h + 1) * c]
    return out


def _lane_col(x, lane):
    idx = lax.broadcasted_iota(jnp.int32, x.shape, 1)
    return jnp.sum(jnp.where(idx == lane, x, 0.0), axis=-1, keepdims=True)


def _const_spec(shape):
    return pl.BlockSpec(shape, lambda *_: (0,) * len(shape), pipeline_mode=pl.Buffered(1))


def _layer_spec(shape, lead):
    idx = tuple(lead) + (0,) * len(shape)
    return pl.BlockSpec((None,) * len(lead) + tuple(shape), lambda *_: idx, pipeline_mode=pl.Buffered(1))


def _params(*sem):
    return pltpu.CompilerParams(dimension_semantics=sem, vmem_limit_bytes=VMEM_LIMIT)


def _ffn_kernel(x_ref, gpre_ref, gpost_ref, wg_ref, wu_ref, wd_ref, *rest, mixed):
    x = x_ref[...]
    if mixed:
        ya_ref, yb_ref, yc_ref, yd_ref, wo_ref, gmix_ref, o_ref = rest
        y = jnp.concatenate([ya_ref[...], yb_ref[...], yc_ref[...], yd_ref[...]], axis=-1)
        x = x + _rms(jnp.dot(y.astype(BF16), wo_ref[...], preferred_element_type=F32), gmix_ref[...])
    else:
        o_ref, = rest
    h = _rms(x, gpre_ref[...]).astype(BF16)
    g = jnp.dot(h, wg_ref[...], preferred_element_type=F32)
    u = jnp.dot(h, wu_ref[...], preferred_element_type=F32)
    a = (_silu(g) * u).astype(BF16)
    y = jnp.dot(a, wd_ref[...], preferred_element_type=F32)
    o_ref[...] = x + 0.5 * _rms(y, gpost_ref[...])


def _ffn(x, gpre, gpost, wg, wu, wd, lead, tm, mix=None):
    n = x.shape[0]
    row = pl.BlockSpec((tm, D_MODEL), lambda i: (i, 0))
    in_specs = [row, _const_spec((1, D_MODEL)), _const_spec((1, D_MODEL)),
                _layer_spec((D_MODEL, D_FF), lead), _layer_spec((D_MODEL, D_FF), lead),
                _layer_spec((D_FF, D_MODEL), lead)]
    args = [x, gpre, gpost, wg, wu, wd]
    if mix is not None:
        ys, w_out, gmix = mix
        l_pad, n_seq, _ = ys[0].shape
        grp = pl.BlockSpec((tm, D_GROUP), lambda i: (i, 0))
        ya_spec = _time_major_spec(tm, l_pad)
        if ya_spec is None:
            ya, ya_spec = jnp.swapaxes(ys[0], 0, 1).reshape(n, D_GROUP), grp
        else:
            ya = ys[0].reshape(l_pad, n_seq * D_GROUP)
        in_specs += [ya_spec, grp, grp, grp, _layer_spec((D_MODEL, D_MODEL), lead[:1]), _const_spec((1, D_MODEL))]
        args += [ya, *ys[1:], w_out, gmix]
    return pl.pallas_call(
        functools.partial(_ffn_kernel, mixed=mix is not None),
        grid=(n // tm,),
        in_specs=in_specs,
        out_specs=row,
        out_shape=jax.ShapeDtypeStruct((n, D_MODEL), F32),
        compiler_params=_params("parallel"),
        name="ffn",
    )(*args)


def _time_major_spec(tm, l_pad):
    if l_pad % tm:
        return None
    per_seq = l_pad // tm
    return pl.BlockSpec((tm, D_GROUP), lambda i: (i % per_seq, i // per_seq))


def _inproj_kernel(x_ref, g_ref, w_ref, o_ref, *u_ref):
    h = _rms(x_ref[...], g_ref[...]).astype(BF16)
    proj = jnp.dot(h, w_ref[...], preferred_element_type=F32)
    o_ref[...] = proj
    if u_ref:
        u_ref[0][...] = proj[:, 0:D_GROUP]


def _inproj(x, gain, w, lead, tm, n_seq, l_pad):
    n = x.shape[0]
    out_specs = [pl.BlockSpec((tm, PROJ_W), lambda i: (i, 0))]
    out_shape = [jax.ShapeDtypeStruct((n, PROJ_W), F32)]
    u_spec = _time_major_spec(tm, l_pad)
    if u_spec is not None:
        out_specs.append(u_spec)
        out_shape.append(jax.ShapeDtypeStruct((l_pad, n_seq * D_GROUP), F32))
    outs = pl.pallas_call(
        _inproj_kernel,
        grid=(n // tm,),
        in_specs=[pl.BlockSpec((tm, D_MODEL), lambda i: (i, 0)), _const_spec((1, D_MODEL)),
                  _layer_spec((D_MODEL, PROJ_W), lead)],
        out_specs=out_specs,
        out_shape=out_shape,
        compiler_params=_params("parallel"),
        name="inproj",
    )(x, gain, w)
    proj = outs[0].reshape(n_seq, l_pad, PROJ_W)
    if u_spec is None:
        return proj, None
    return proj, outs[1].reshape(l_pad, n_seq, D_GROUP)


class _SeqPlan:
    def __init__(self, n_seq, length, l_pad, seq_blk, t_blk, chunk, unroll=4):
        assert l_pad % t_blk == 0 and t_blk % chunk == 0 and n_seq % seq_blk == 0
        assert chunk % SUBLANES == 0 and 0 <= l_pad - length < SUBLANES
        assert length % chunk == 0 or l_pad == chunk
        assert seq_blk & (seq_blk - 1) == 0
        self.n_seq, self.length, self.l_pad = n_seq, length, l_pad
        self.seq_blk, self.t_blk, self.chunk = seq_blk, t_blk, chunk
        self.rows = seq_blk * t_blk
        self.t_steps = l_pad // t_blk
        self.grid = (n_seq // seq_blk, self.t_steps)
        self.unroll = min(unroll, seq_blk)

    def rows_spec(self, width, col_block=0):
        return pl.BlockSpec((self.seq_blk, self.t_blk, width), lambda i, j: (i, j, col_block))

    def seq_spec(self, *tail, layer=None):
        zeros = (0,) * len(tail)
        if layer is None:
            return pl.BlockSpec((self.seq_blk,) + tail, lambda i, j: (i,) + zeros)
        return pl.BlockSpec((None, self.seq_blk) + tail, lambda i, j: (layer, i) + zeros)

    def rows_shape(self):
        return jax.ShapeDtypeStruct((self.n_seq, self.l_pad, D_GROUP), F32)

    def for_unit_groups(self, size, body):
        n_units = self.seq_blk * (self.t_blk // size)

        def step(n, carry):
            if self.unroll == self.seq_blk:
                body([(u, n) for u in range(self.unroll)])
            else:
                units = [n * self.unroll + u for u in range(self.unroll)]
                body([(_mod2(unit, self.seq_blk), _div2(unit, self.seq_blk)) for unit in units])
            return carry

        lax.fori_loop(0, n_units // self.unroll, step, 0)

    def for_seq_groups(self, body):
        def step(n, carry):
            body([n * self.unroll + u for u in range(self.unroll)])
            return carry
        lax.fori_loop(0, self.seq_blk // self.unroll, step, 0)

    def for_units(self, size, body):
        def group(units):
            for s, c in units:
                body(s, c)
        self.for_unit_groups(size, group)

    def valid_rows(self, j, c, size):
        t = j * self.t_blk + c * size + lax.broadcasted_iota(jnp.int32, (size, 1), 0)
        return t < self.length


def _seq_call(kernel, plan, in_specs, out_specs, out_shape, scratch, name):
    return pl.pallas_call(
        kernel, grid=plan.grid, in_specs=in_specs, out_specs=out_specs, out_shape=out_shape,
        scratch_shapes=scratch, compiler_params=_params("parallel", "arbitrary"), name=name)


def _s5_kernel(u_ref, h0_ref, bmat_ref, cmat_ref, d_ref, w1_ref, w2_ref, a_ref,
               y_ref, hn_ref, s_scr, carry_scr, *, plan):
    j = pl.program_id(1)
    P = SSM_LANES
    S = plan.seq_blk
    slab = min(SUBLANES, S)

    @pl.when(j == 0)
    def _():
        carry_scr[...] = h0_ref[...]

    u = u_ref[...].reshape(plan.rows, D_GROUP)
    s_scr[...] = _mm(u, bmat_ref[...])
    a_re, a_im = a_ref[0:slab, 0:P], a_ref[0:slab, P:2 * P]

    def step(t, x):
        out = []
        for g in range(S // slab):
            rows = pl.ds(pl.multiple_of(t * S + g * slab, slab), slab)
            x_re, x_im = x[2 * g], x[2 * g + 1]
            n_re = a_re * x_re - a_im * x_im + s_scr[rows, 0:P]
            n_im = a_re * x_im + a_im * x_re + s_scr[rows, P:2 * P]
            s_scr[rows, 0:P] = n_re
            s_scr[rows, P:2 * P] = n_im
            out += [n_re, n_im]
        return tuple(out)

    x0 = []
    for g in range(S // slab):
        x0 += [carry_scr[g * slab:(g + 1) * slab, 0:P], carry_scr[g * slab:(g + 1) * slab, P:2 * P]]
    x = lax.fori_loop(0, plan.t_blk, step, tuple(x0))
    for g in range(S // slab):
        carry_scr[g * slab:(g + 1) * slab, 0:P] = x[2 * g]
        carry_scr[g * slab:(g + 1) * slab, P:2 * P] = x[2 * g + 1]

    y = _mm(s_scr[...], cmat_ref[...]) + d_ref[...] * u
    y = jax.nn.gelu(y)
    y = _mm(y, w1_ref[...]) * jax.nn.sigmoid(_mm(y, w2_ref[...]))
    t = j * plan.t_blk + _div2(lax.broadcasted_iota(jnp.int32, (plan.rows, 1), 0), S)
    y_ref[...] = jnp.where(t < plan.length, y, 0.0).reshape(plan.t_blk, S, D_GROUP)

    @pl.when(j == plan.t_steps - 1)
    def _():
        hn_ref[...] = s_scr[((plan.length - 1) % plan.t_blk) * S:((plan.length - 1) % plan.t_blk + 1) * S, :]


def _s5(u_tm, h0, layer, consts, plan):
    P2 = 2 * SSM_LANES
    tm_spec = pl.BlockSpec((plan.t_blk, plan.seq_blk, D_GROUP), lambda i, j: (j, i, 0))
    in_specs = [tm_spec, plan.seq_spec(P2, layer=layer),
                _const_spec((D_GROUP, P2)), _const_spec((P2, D_GROUP)), _const_spec((1, D_GROUP)),
                _const_spec((D_GROUP, D_GROUP)), _const_spec((D_GROUP, D_GROUP)),
                _const_spec((SUBLANES, P2))]
    out_specs = [tm_spec, plan.seq_spec(P2)]
    out_shape = [jax.ShapeDtypeStruct((plan.l_pad, plan.n_seq, D_GROUP), F32),
                 jax.ShapeDtypeStruct((plan.n_seq, P2), F32)]
    scratch = [pltpu.VMEM((plan.rows, P2), F32), pltpu.VMEM((plan.seq_blk, P2), F32)]
    return _seq_call(functools.partial(_s5_kernel, plan=plan), plan, in_specs, out_specs, out_shape,
                     scratch, "s5")(u_tm, h0, *consts)


def _selectors(consts, chunk, transposed):
    arrays = (consts["bd"], consts["bd2"], consts["tile_tc" if transposed else "tile"], consts["tile_t"],
              consts["tri"][chunk])
    return arrays, [_const_spec(a.shape) for a in arrays]


def _load_state_bd(s0, tile_mat, bd_mask):
    return _sel_r(s0, tile_mat, EXACT_TERMS) * bd_mask


def _store_state_bd(s_bd, tile_mat_t):
    return _sel_r(s_bd, tile_mat_t, EXACT_TERMS)


def _load_state_bd_t(s0, tile_mat_tc, bd_mask):
    return _sel_nt(tile_mat_tc, s0, EXACT_TERMS) * bd_mask


def _store_state_bd_t(st_bd, tile_mat_t):
    parts = jnp.concatenate(_split_bf16(st_bd, EXACT_TERMS), axis=0)
    return lax.dot_general(parts, tile_mat_t, (((0,), (0,)), ((), ())), preferred_element_type=F32)


def _hgrn_kernel(q_ref, f_ref, i_ref, g_ref, s0_ref, lb_ref, nw_ref,
                 bd_ref, bd2_ref, tile_ref, tilet_ref, tri_ref,
                 y_ref, sn_ref, st_scr, *, plan, sub):
    j = pl.program_id(1)
    C = plan.chunk
    masks = _head_masks()
    bd = bd_ref[...]
    lb = lb_ref[...]

    @pl.when(j == 0)
    def _():
        def init(seqs):
            for s, st in zip(seqs, [_load_state_bd_t(s0_ref[s], tile_ref[...], bd) for s in seqs]):
                st_scr[s] = st
        plan.for_seq_groups(init)

    stacked_mask = jnp.concatenate([jnp.broadcast_to(m, (sub, D_GROUP)) for m in masks], axis=0)

    def gated_inputs(s, c, valid):
        rows = pl.ds(pl.multiple_of(c * C, C), C)
        fr = f_ref[s, rows, :]
        f = lb + (1.0 - lb) * jax.nn.sigmoid(fr)
        k = jnp.where(valid, (1.0 - lb) * jax.nn.sigmoid(-fr), 0.0)
        logf = jnp.where(valid, jnp.log(f), 0.0)
        return _silu(q_ref[s, rows, :]), k, i_ref[s, rows, :], logf

    def scores(q, k, cum, blk):
        r0, r1 = blk * sub, (blk + 1) * sub
        c0 = cum[r0 - 1:r0] if blk else jnp.zeros_like(cum[0:1])
        qs = _stack_heads(q[r0:r1] * jnp.exp(cum[r0:r1] - c0), masks)
        att = _mm_nt(qs, k[0:r1] * jnp.exp(c0 - cum[0:r1]))
        t_idx = r0 + _mod2(lax.broadcasted_iota(jnp.int32, att.shape, 0), sub)
        s_idx = lax.broadcasted_iota(jnp.int32, att.shape, 1)
        return jnp.where(s_idx <= t_idx, att, 0.0)

    def chunk(units):
        each = lambda fn, *xs: [fn(*a) for a in zip(*xs)]
        valid = [plan.valid_rows(j, c, C) for _, c in units]
        qkvl = [gated_inputs(s, c, ok) for (s, c), ok in zip(units, valid)]
        q, k, v, logf = ([t[i] for t in qkvl] for i in range(4))
        cum = each(lambda x: _sel_l(tri_ref[...], x, EXACT_TERMS), logf)
        st = [st_scr[s] for s, _ in units]
        o = each(lambda a, b, x: _mm_nt(a * jnp.exp(b), x), q, cum, st)
        intra = [[] for _ in units]
        for blk in range(C // sub):
            att = each(lambda a, b, d: scores(a, b, d, blk), q, k, cum)
            pv = each(lambda a, x: _mm(a, x[0:(blk + 1) * sub]), att, v)
            for lst, x in zip(intra, pv):
                lst.append(_unstack_heads(x * stacked_mask, N_HEADS))
        o = each(lambda a, lst: a + jnp.concatenate(lst, axis=0), o, intra)
        upd = each(lambda x, b, d: _mm_tn(x, b * jnp.exp(d[C - 1:C] - d)) * bd, v, k, cum)
        for (s, _), x, d, u in zip(units, st, cum, upd):
            st_scr[s] = x * jnp.exp(d[C - 1:C]) + u
        ms = each(lambda x: _sel_r(x * x, bd2_ref[...], SUMSQ_TERMS) * (1.0 / HEAD_DIM), o)
        for (s, c), x, ss, ok in zip(units, o, ms, valid):
            rows = pl.ds(pl.multiple_of(c * C, C), C)
            x = x * lax.rsqrt(ss + NORM_EPS) * nw_ref[...] * _silu(g_ref[s, rows, :])
            y_ref[s, rows, :] = jnp.where(ok, x, 0.0)

    plan.for_unit_groups(C, chunk)

    @pl.when(j == plan.t_steps - 1)
    def _():
        def fin(seqs):
            for s, st in zip(seqs, [_store_state_bd_t(st_scr[s], tilet_ref[...]) for s in seqs]):
                sn_ref[s] = st
        plan.for_seq_groups(fin)


def _hgrn(proj, s0, layer, lb, norm_w, consts, plan):
    C = plan.chunk
    sub = min(16, C)
    selectors, selector_specs = _selectors(consts, C, transposed=True)
    in_specs = [plan.rows_spec(D_GROUP, 1), plan.rows_spec(D_GROUP, 2), plan.rows_spec(D_GROUP, 3),
                plan.rows_spec(D_GROUP, 4), plan.seq_spec(D_GROUP, HEAD_DIM, layer=layer),
                _const_spec((1, D_GROUP)), _const_spec((1, D_GROUP))] + selector_specs
    out_specs = [plan.rows_spec(D_GROUP), plan.seq_spec(D_GROUP, HEAD_DIM)]
    out_shape = [plan.rows_shape(), jax.ShapeDtypeStruct((plan.n_seq, D_GROUP, HEAD_DIM), F32)]
    scratch = [pltpu.VMEM((plan.seq_blk, D_GROUP, D_GROUP), F32)]
    kern = functools.partial(_hgrn_kernel, plan=plan, sub=sub)
    return _seq_call(kern, plan, in_specs, out_specs, out_shape, scratch, "hgrn2")(
        proj, proj, proj, proj, s0, lb, norm_w, *selectors)


def _gdn_kernel(q_ref, k_ref, v_ref, z_ref, sc_ref, s0_ref, cv0_ref, cw_ref, alog_ref, dtb_ref, nw_ref,
                bd_ref, bd2_ref, tile_ref, tilet_ref, tri_ref, eb_ref, ea_ref, lvl0_ref, lvl_ref,
                y_ref, sn_ref, cvn_ref, st_scr, cx_scr, prev_scr, *, plan):
    j = pl.program_id(1)
    C = plan.chunk
    H = N_HEADS
    HC = H * C
    masks = _head_masks()
    bd = bd_ref[...]
    lane = lax.broadcasted_iota(jnp.int32, (1, D_GROUP), 1)
    beta_lanes = lane < H
    a_lanes = (lane >= H) & (lane < 2 * H)
    neg_rate = jnp.where(a_lanes, -jnp.exp(alog_ref[...]), 0.0)
    n_lvl = max(1, int(math.ceil(math.log2(C))))
    valid_last = (plan.length - 1) % C + 1
    assert valid_last >= GD_CONV - 1
    pad0 = SUBLANES - (GD_CONV - 1)

    @pl.when(j == 0)
    def _():
        def init(seqs):
            for s, st in zip(seqs, [_load_state_bd(s0_ref[s], tile_ref[...], bd) for s in seqs]):
                st_scr[s] = st
                prev_scr[s] = jnp.zeros((SUBLANES, 3 * D_GROUP), F32)
                prev_scr[s, pl.ds(pad0, GD_CONV - 1), :] = cv0_ref[s]
        plan.for_seq_groups(init)

    row_i = lax.broadcasted_iota(jnp.int32, (HC, HC), 0)
    col_i = lax.broadcasted_iota(jnp.int32, (HC, HC), 1)
    same_head = _div2(row_i, C) == _div2(col_i, C)
    incl = same_head & (col_i <= row_i)
    strict = same_head & (col_i < row_i)
    eye = (row_i == col_i).astype(F32)

    def head_rows(gc_all):
        first_tile = gc_all[:, 0:LANES]
        if C < LANES:
            first_tile = jnp.concatenate([first_tile, jnp.zeros((LANES - C, LANES), F32)], axis=0)
        tr = first_tile.T
        return jnp.concatenate([tr[H + h:H + h + 1, 0:C] for h in range(H)], axis=1)

    def conv_qkv(s, c):
        rows = pl.ds(pl.multiple_of(c * C, C), C)
        cx = cx_scr.at[s]
        cx[0:SUBLANES, :] = prev_scr[s]
        cx[SUBLANES:SUBLANES + C, 0:D_GROUP] = q_ref[s, rows, :]
        cx[SUBLANES:SUBLANES + C, D_GROUP:2 * D_GROUP] = k_ref[s, rows, :]
        cx[SUBLANES:SUBLANES + C, 2 * D_GROUP:3 * D_GROUP] = v_ref[s, rows, :]
        cw = cw_ref[...]
        conv = cx[pad0:pad0 + C, :] * cw[0:1]
        for tap in range(1, GD_CONV):
            conv = conv + cx[pad0 + tap:pad0 + tap + C, :] * cw[tap:tap + 1]
        prev_scr[s, pl.ds(pad0, GD_CONV - 1), :] = cx[SUBLANES + C - (GD_CONV - 1):SUBLANES + C, :]
        cvn_ref[s] = cx[SUBLANES + valid_last - (GD_CONV - 1):SUBLANES + valid_last, :]
        conv = _silu(conv)
        return conv[:, 0:D_GROUP], conv[:, D_GROUP:2 * D_GROUP], conv[:, 2 * D_GROUP:3 * D_GROUP]

    def gates(s, c, valid):
        sc = sc_ref[s, pl.ds(pl.multiple_of(c * C, C), C), :]
        beta_all = jnp.where(valid & beta_lanes, jax.nn.sigmoid(sc), 0.0)
        g_all = jnp.where(valid, neg_rate * jax.nn.softplus(sc + dtb_ref[...]), 0.0)
        return beta_all, g_all

    def decay_matrix(gc_all, gc_row):
        gc_col = jnp.concatenate([_lane_col(gc_all, H + h) for h in range(H)], axis=0)
        return jnp.exp(jnp.where(incl, gc_col - gc_row, -jnp.inf))

    def chunk(units):
        each = lambda f, *xs: [f(*a) for a in zip(*xs)]
        valid = [plan.valid_rows(j, c, C) for _, c in units]
        qkv = [conv_qkv(s, c) for s, c in units]
        q, k, v = ([t[i] for t in qkv] for i in range(3))
        qss = each(lambda x: _sel_r(x * x, bd2_ref[...], SUMSQ_TERMS), q)
        kss = each(lambda x: _sel_r(x * x, bd2_ref[...], SUMSQ_TERMS), k)
        q = each(lambda x, ss: x * lax.rsqrt(ss + NORM_EPS) * (HEAD_DIM ** -0.5), q, qss)
        k = each(lambda x, ss, ok: jnp.where(ok, x * lax.rsqrt(ss + NORM_EPS), 0.0), k, kss, valid)

        bg = [gates(s, c, ok) for (s, c), ok in zip(units, valid)]
        beta_all, g_all = [t[0] for t in bg], [t[1] for t in bg]
        gc_all = each(lambda g: _sel_l(tri_ref[...], g, EXACT_TERMS), g_all)
        beta_l = each(lambda b: _sel_r(b, eb_ref[...], EXACT_TERMS), beta_all)
        gc_l = each(lambda g: _sel_r(g, ea_ref[...], EXACT_TERMS), gc_all)
        gam_l = each(jnp.exp, gc_l)
        gc_row = each(head_rows, gc_all)
        dec = each(decay_matrix, gc_all, gc_row)
        beta_col = each(lambda b: jnp.concatenate([_lane_col(b, h) for h in range(H)], axis=0), beta_all)

        ks = each(lambda x: _stack_heads(x, masks), k)
        qs = each(lambda x: _stack_heads(x, masks), q)
        kk = each(_mm_nt, ks, ks)
        m = each(lambda d, b, x: jnp.where(strict, d, 0.0) * b * x, dec, beta_col, kk)
        t_inv = each(lambda x: eye - x * lvl0_ref[...], m)
        m_b = each(lambda x: x.astype(BF16), m)
        for lvl in range(1, n_lvl):
            t_b = each(lambda t: t.astype(BF16), t_inv)
            half = each(lambda t, x: _mms(t, x * lvl_ref[lvl - 1]), t_b, m_b)
            t_inv = each(lambda t, hf, tb: t - _mms(hf, tb), t_inv, half, t_b)
        t_b = each(lambda t: t.astype(BF16), t_inv)
        rhs_w = each(lambda b, g, x: _stack_heads(b * g * x, masks), beta_l, gam_l, k)
        rhs_u = each(lambda b, x: _stack_heads(b * x, masks), beta_l, v)
        w = each(lambda t, r: _unstack_heads(_mms(t, r), H), t_b, rhs_w)
        u0 = each(lambda t, r: _unstack_heads(_mms(t, r), H), t_b, rhs_u)
        qk = each(lambda a, b, d: _mm_nt(a, b) * d, qs, ks, dec)

        st = [st_scr[s] for s, _ in units]
        u = each(lambda a, b, x: a - _mm(b, x), u0, w, st)
        o_st = each(lambda x, g, y: _mm(x * g, y), q, gam_l, st)
        o_in = each(lambda a, b: _unstack_heads(_mm(a, _stack_heads(b, masks)), H), qk, u)
        upd = each(lambda x, g, b: _mm_tn(x * jnp.exp(g[C - 1:C] - g), b) * bd, k, gc_l, u)
        for (s, _), g, x, d in zip(units, gc_l, st, upd):
            st_scr[s] = jnp.exp(g[C - 1:C]) * x + d

        o = each(jnp.add, o_st, o_in)
        ms = each(lambda x: _sel_r(x * x, bd2_ref[...], SUMSQ_TERMS) * (1.0 / HEAD_DIM), o)
        for (s, c), x, ss, ok in zip(units, o, ms, valid):
            rows = pl.ds(pl.multiple_of(c * C, C), C)
            x = x * lax.rsqrt(ss + NORM_EPS) * nw_ref[...] * _silu(z_ref[s, rows, :])
            y_ref[s, rows, :] = jnp.where(ok, x, 0.0)

    plan.for_unit_groups(C, chunk)

    @pl.when(j == plan.t_steps - 1)
    def _():
        def fin(seqs):
            for s, st in zip(seqs, [_store_state_bd(st_scr[s], tilet_ref[...]) for s in seqs]):
                sn_ref[s] = st
        plan.for_seq_groups(fin)


def _gdn(proj, s0, cv0, layer, lw, consts, plan):
    C = plan.chunk
    W3 = 3 * D_GROUP
    in_specs = [plan.rows_spec(D_GROUP, 5), plan.rows_spec(D_GROUP, 6), plan.rows_spec(D_GROUP, 7),
                plan.rows_spec(D_GROUP, 8), plan.rows_spec(D_GROUP, 11),
                plan.seq_spec(D_GROUP, HEAD_DIM, layer=layer), plan.seq_spec(GD_CONV - 1, W3, layer=layer),
                _const_spec((GD_CONV, W3)), _const_spec((1, D_GROUP)), _const_spec((1, D_GROUP)),
                _const_spec((1, D_GROUP))] + _selectors(consts, C, transposed=False)[1] + [
                _const_spec((EXACT_TERMS * D_GROUP, D_GROUP)), _const_spec((EXACT_TERMS * D_GROUP, D_GROUP)),
                _const_spec(consts["solve"][C][0].shape), _const_spec(consts["solve"][C][1].shape)]
    out_specs = [plan.rows_spec(D_GROUP), plan.seq_spec(D_GROUP, HEAD_DIM), plan.seq_spec(GD_CONV - 1, W3)]
    out_shape = [plan.rows_shape(),
                 jax.ShapeDtypeStruct((plan.n_seq, D_GROUP, HEAD_DIM), F32),
                 jax.ShapeDtypeStruct((plan.n_seq, GD_CONV - 1, W3), F32)]
    scratch = [pltpu.VMEM((plan.seq_blk, D_GROUP, D_GROUP), F32),
               pltpu.VMEM((plan.seq_blk, SUBLANES + C, W3), F32),
               pltpu.VMEM((plan.seq_blk, SUBLANES, W3), F32)]
    kern = functools.partial(_gdn_kernel, plan=plan)
    return _seq_call(kern, plan, in_specs, out_specs, out_shape, scratch, "gdn")(
        proj, proj, proj, proj, proj, s0, cv0, lw["conv_w"], lw["a_log"], lw["dt_bias"], lw["norm_w"],
        *_selectors(consts, C, transposed=False)[0], consts["eb"], consts["ea"], *consts["solve"][C])


def _swa_kernel(q_ref, kv_ref, ck_ref, cv_ref, bias_ref, sink_ref, y_ref, kn_ref, vn_ref,
                kk_scr, vv_scr, *, plan, has_cache):
    j = pl.program_id(1)
    QB = plan.chunk
    W = SW_WINDOW
    HK = SW_KV_HEADS * HEAD_DIM
    kv_masks = _head_masks(HK)
    valid_last = (plan.length - 1) % QB + 1

    @pl.when(j == 0)
    def _():
        kn_ref[...] = ck_ref[...]
        vn_ref[...] = cv_ref[...]

    def stacked_queries(s, c):
        rows = pl.ds(pl.multiple_of(c * QB, QB), QB)
        kk, vv = kk_scr.at[s], vv_scr.at[s]
        kk[0:W, :] = kn_ref[s]
        vv[0:W, :] = vn_ref[s]
        kv = kv_ref[s, rows, :]
        kk[W:W + QB, :] = kv[:, 0:HK]
        vv[W:W + QB, :] = kv[:, HK:2 * HK]
        q = q_ref[s, rows, :]
        return jnp.concatenate([q[:, 0:HK] * kv_masks[0], q[:, 0:HK] * kv_masks[1],
                                q[:, HK:2 * HK] * kv_masks[0], q[:, HK:2 * HK] * kv_masks[1]], axis=0)

    def probabilities(sc, c):
        sc = sc * (HEAD_DIM ** -0.5) + bias_ref[...]
        if not has_cache:
            t0 = j * plan.t_blk + c * QB
            col = lax.broadcasted_iota(jnp.int32, sc.shape, 1)
            sc = jnp.where(t0 + col < W, -jnp.inf, sc)
        sink = sink_ref[...]
        mx = jnp.maximum(jnp.max(sc, axis=-1, keepdims=True), sink)
        return jnp.exp(sc - mx), jnp.exp(sink - mx)

    ones_cols = jnp.ones((W + QB, HK), BF16)

    def attend(p, sink_p, s):
        acc = _mm(p, jnp.concatenate([vv_scr[s].astype(BF16), ones_cols], axis=1))
        return acc[:, 0:HK] / (acc[:, HK:2 * HK] + sink_p)

    def block(units):
        qs = [stacked_queries(s, c) for s, c in units]
        sc = [_mm_nt(x, kk_scr[s]) for x, (s, _) in zip(qs, units)]
        p = [probabilities(x, c) for x, (_, c) in zip(sc, units)]
        pv = [attend(x, sp, s) for (x, sp), (s, _) in zip(p, units)]
        for x, (s, c) in zip(pv, units):
            oa = x[0:QB] * kv_masks[0] + x[QB:2 * QB] * kv_masks[1]
            ob = x[2 * QB:3 * QB] * kv_masks[0] + x[3 * QB:4 * QB] * kv_masks[1]
            rows = pl.ds(pl.multiple_of(c * QB, QB), QB)
            y_ref[s, rows, :] = jnp.where(plan.valid_rows(j, c, QB), jnp.concatenate([oa, ob], axis=-1), 0.0)
            kn_ref[s] = kk_scr[s, valid_last:valid_last + W, :]
            vn_ref[s] = vv_scr[s, valid_last:valid_last + W, :]

    plan.for_unit_groups(QB, block)


def _swa(proj, cache_k, cache_v, layer, bias, sink_col, plan, has_cache):
    QB = plan.chunk
    HK = SW_KV_HEADS * HEAD_DIM
    NK = SW_WINDOW + QB
    in_specs = [plan.rows_spec(D_GROUP, 9), plan.rows_spec(D_GROUP, 10),
                plan.seq_spec(SW_WINDOW, HK, layer=layer), plan.seq_spec(SW_WINDOW, HK, layer=layer),
                _const_spec((4 * QB, NK)), _const_spec((4 * QB, 1))]
    out_specs = [plan.rows_spec(D_GROUP), plan.seq_spec(SW_WINDOW, HK), plan.seq_spec(SW_WINDOW, HK)]
    out_shape = [plan.rows_shape(),
                 jax.ShapeDtypeStruct((plan.n_seq, SW_WINDOW, HK), F32),
                 jax.ShapeDtypeStruct((plan.n_seq, SW_WINDOW, HK), F32)]
    scratch = [pltpu.VMEM((plan.seq_blk, NK, HK), F32), pltpu.VMEM((plan.seq_blk, NK, HK), F32)]
    kern = functools.partial(_swa_kernel, plan=plan, has_cache=has_cache)
    return _seq_call(kern, plan, in_specs, out_specs, out_shape, scratch, "swa")(
        proj, proj, cache_k, cache_v, bias, sink_col)


_Q_HEAD_ORDER = (0, 2, 1, 3)


def _swa_bias(qb):
    w = SW_WINDOW
    i = np.arange(qb)[:, None]
    jj = np.arange(w + qb)[None, :]
    dist = w + i - jj
    ok = (dist >= 0) & (dist <= w)
    slopes = 2.0 ** (-8.0 * np.arange(1, N_HEADS + 1) / N_HEADS)
    blocks = []
    for g in range(2):
        for kv in range(SW_KV_HEADS):
            head = kv * 2 + g
            blocks.append(np.where(ok, -slopes[head] * dist, -np.inf))
    return jnp.asarray(np.concatenate(blocks, axis=0), F32)


def _constants(chunks):
    lane_head = np.arange(D_GROUP) // HEAD_DIM
    bd = (lane_head[:, None] == lane_head[None, :]).astype(np.float32)
    tile = np.tile(np.eye(HEAD_DIM, dtype=np.float32), (1, N_HEADS))
    eb = np.zeros((D_GROUP, D_GROUP), np.float32)
    ea = np.zeros((D_GROUP, D_GROUP), np.float32)
    for h in range(N_HEADS):
        eb[h, lane_head == h] = 1.0
        ea[N_HEADS + h, lane_head == h] = 1.0
    def solve_levels(c):
        blk = np.arange(N_HEADS * c) >> 1
        out = [blk[:, None] == blk[None, :]]
        while (1 << len(out)) < c:
            out.append(((blk >> 1)[:, None] == (blk >> 1)[None, :]) & (blk[:, None] != blk[None, :]))
            blk = blk >> 1
        return jnp.asarray(out[0].astype(np.float32)), jnp.asarray(np.stack(out[1:]).astype(np.float32), BF16)

    rows = lambda m, n: jnp.asarray(np.tile(m, (n, 1)), BF16)
    cols = lambda m, n: jnp.asarray(np.tile(m, (1, n)), BF16)
    return {"bd": jnp.asarray(bd), "bd2": rows(bd, SUMSQ_TERMS),
            "tile": rows(tile, EXACT_TERMS), "tile_t": rows(tile.T, EXACT_TERMS), "tile_tc": cols(tile.T, EXACT_TERMS),
            "eb": rows(eb, EXACT_TERMS), "ea": rows(ea, EXACT_TERMS),
            "tri": {c: cols(np.tril(np.ones((c, c), np.float32)), EXACT_TERMS) for c in chunks},
            "solve": {c: solve_levels(c) for c in chunks}}


def _s5_consts(lam_re, lam_im, log_dt, b_re, b_im, c_re, c_im, d, w1, w2):
    dt = jnp.exp(log_dt)[:, None]
    mag = jnp.exp(lam_re * dt)
    ang = lam_im * dt
    a_re, a_im = mag * jnp.cos(ang), mag * jnp.sin(ang)
    den = lam_re * lam_re + lam_im * lam_im
    z_re = ((a_re - 1.0) * lam_re + a_im * lam_im) / den
    z_im = (a_im * lam_re - (a_re - 1.0) * lam_im) / den
    bb_re = z_re[..., None] * b_re - z_im[..., None] * b_im
    bb_im = z_re[..., None] * b_im + z_im[..., None] * b_re
    eye = jnp.eye(SSM_GROUPS, dtype=F32)
    to_b = lambda t: jnp.einsum("gpc,gh->gchp", t, eye).reshape(D_GROUP, SSM_LANES)
    to_c = lambda t: jnp.einsum("gcp,gh->gphc", t, eye).reshape(SSM_LANES, D_GROUP)
    bmat = jnp.concatenate([to_b(bb_re), to_b(bb_im)], axis=1).astype(BF16)
    cmat = jnp.concatenate([to_c(c_re), -to_c(c_im)], axis=0).astype(BF16)
    a_row = jnp.concatenate([a_re.reshape(1, SSM_LANES), a_im.reshape(1, SSM_LANES)], axis=1)
    return (bmat, cmat, d.reshape(1, D_GROUP), w1.astype(BF16), w2.astype(BF16),
            jnp.broadcast_to(a_row, (SUBLANES, 2 * SSM_LANES)))


def _prep_w_in(w):
    q_d = w[:, :, 2312:2568].reshape(DEPTH, D_MODEL, N_HEADS, HEAD_DIM)[:, :, np.array(_Q_HEAD_ORDER)]
    scal = jnp.pad(w[:, :, 2304:2312], ((0, 0), (0, 0), (0, D_GROUP - 2 * N_HEADS)))
    return jnp.concatenate([w[:, :, 0:2304], q_d.reshape(DEPTH, D_MODEL, D_GROUP), w[:, :, 2568:2824], scal],
                           axis=2).astype(BF16)


def _prep_w_out(w):
    d_rows = w[:, 3 * D_GROUP:].reshape(DEPTH, N_HEADS, HEAD_DIM, D_MODEL)[:, np.array(_Q_HEAD_ORDER)]
    return jnp.concatenate([w[:, :3 * D_GROUP], d_rows.reshape(DEPTH, D_GROUP, D_MODEL)], axis=1).astype(BF16)


def _lane_vec(vals, first_lane):
    return jnp.zeros((1, D_GROUP), F32).at[0, first_lane:first_lane + vals.shape[0]].set(vals)


def _trunk(x, state, layers, big, consts, plans, tm, has_cache):
    plan, plan5, plan_w = plans["mix"], plans["s5"], plans["swa"]
    bias = _swa_bias(plan_w.chunk)
    outs = {k: [] for k in ("ssm", "hgrn", "gdn", "conv", "swa_k", "swa_v")}
    for l, lw in enumerate(layers):
        sink_col = jnp.repeat(lw["sinks"], plan_w.chunk).reshape(4 * plan_w.chunk, 1)
        gn = lw["gains"]
        x = _ffn(x, gn[0], gn[1], big["wg"], big["wu"], big["wd"], (l, 0), tm)
        proj, u_tm = _inproj(x, gn[2], big["w_in"], (l,), tm, plan.n_seq, plan.l_pad)
        if u_tm is None:
            u_tm = jnp.swapaxes(proj[:, :, 0:D_GROUP], 0, 1)
        sl = l if has_cache else 0
        y_tm, ssm = _s5(u_tm, state["ssm"], sl, lw["s5"], plan5)
        y_b, hg = _hgrn(proj, state["hgrn"], sl, lw["lb"], lw["hgrn_nw"], consts, plan)
        y_c, gd, cv = _gdn(proj, state["gdn"], state["conv"], sl, lw["gdn"], consts, plans["gdn"])
        y_d, ck, cvv = _swa(proj, state["swa_k"], state["swa_v"], sl, bias, sink_col, plan_w, has_cache)
        ys = (y_tm,) + tuple(y.reshape(x.shape[0], D_GROUP) for y in (y_b, y_c, y_d))
        x = _ffn(x, gn[4], gn[5], big["wg"], big["wu"], big["wd"], (l, 1), tm, mix=(ys, big["w_out"], gn[3]))
        for k, val in zip(outs, (ssm, hg, gd, cv, ck, cvv)):
            outs[k].append(val)
    return x, {k: jnp.stack(v, axis=0) for k, v in outs.items()}


def _finish_states(st, n_seq):
    ssm = st["ssm"].reshape(DEPTH, n_seq, 2, SSM_GROUPS, SSM_STATE)
    hg = st["hgrn"].reshape(DEPTH, n_seq, N_HEADS, HEAD_DIM, HEAD_DIM)
    gd = st["gdn"].reshape(DEPTH, n_seq, N_HEADS, HEAD_DIM, HEAD_DIM)
    sk = st["swa_k"].reshape(DEPTH, n_seq, SW_WINDOW, SW_KV_HEADS, HEAD_DIM)
    sv = st["swa_v"].reshape(DEPTH, n_seq, SW_WINDOW, SW_KV_HEADS, HEAD_DIM)
    return ssm[:, :, 0], ssm[:, :, 1], hg, gd, st["conv"], sk, sv


def kernel(x_prompt, x_sample, state_ssm_re, state_ssm_im, state_hgrn, state_gdn, state_gdn_conv,
           cache_swa_k, cache_swa_v, norm_gains, ffn_w_gate, ffn_w_up, ffn_w_down, w_in, w_out,
           ssm_lambda_re, ssm_lambda_im, ssm_log_dt, ssm_b_re, ssm_b_im, ssm_c_re, ssm_c_im, ssm_d,
           ssm_w_glu1, ssm_w_glu2, hgrn_lb_logits, hgrn_norm_w, gdn_conv_w, gdn_a_log, gdn_dt_bias,
           gdn_norm_w, swa_sinks):
    bp, lp, _ = x_prompt.shape
    bs, ls, _ = x_sample.shape
    ls_pad = -(-ls // SUBLANES) * SUBLANES
    plan_s = _SeqPlan(bs, ls, ls_pad, seq_blk=min(bs, 16), t_blk=ls_pad, chunk=ls_pad, unroll=16)
    plans_s = {"mix": plan_s, "gdn": plan_s, "s5": plan_s, "swa": plan_s}
    plans_p = {
        "mix": _SeqPlan(bp, lp, lp, seq_blk=min(bp, 8), t_blk=min(lp, 256), chunk=min(lp, 64), unroll=8),
        "gdn": _SeqPlan(bp, lp, lp, seq_blk=min(bp, 4), t_blk=min(lp, 256), chunk=min(lp, 64)),
        "s5": _SeqPlan(bp, lp, lp, seq_blk=min(bp, SUBLANES), t_blk=min(lp, 128), chunk=min(lp, 64)),
        "swa": _SeqPlan(bp, lp, lp, seq_blk=min(bp, 8), t_blk=min(lp, 256), chunk=min(lp, SW_WINDOW), unroll=8)}
    consts = _constants({plans_p["mix"].chunk, plan_s.chunk})

    gam = jax.nn.softmax(hgrn_lb_logits.astype(F32), axis=0)
    lbs = jnp.cumsum(gam, axis=0) - gam[:1]
    big = {"wg": ffn_w_gate.astype(BF16), "wu": ffn_w_up.astype(BF16), "wd": ffn_w_down.astype(BF16),
           "w_in": _prep_w_in(w_in), "w_out": _prep_w_out(w_out)}
    layers = []
    for l in range(DEPTH):
        sinks = swa_sinks[l].astype(F32)[np.array(_Q_HEAD_ORDER)]
        layers.append({
            "gains": norm_gains[l].astype(F32).reshape(6, 1, D_MODEL),
            "s5": _s5_consts(ssm_lambda_re[l], ssm_lambda_im[l], ssm_log_dt[l], ssm_b_re[l], ssm_b_im[l],
                             ssm_c_re[l], ssm_c_im[l], ssm_d[l], ssm_w_glu1[l], ssm_w_glu2[l]),
            "lb": lbs[l].reshape(1, D_GROUP),
            "hgrn_nw": jnp.tile(hgrn_norm_w[l].astype(F32), N_HEADS).reshape(1, D_GROUP),
            "gdn": {"conv_w": gdn_conv_w[l].astype(F32),
                    "a_log": _lane_vec(gdn_a_log[l].astype(F32), N_HEADS),
                    "dt_bias": _lane_vec(gdn_dt_bias[l].astype(F32), N_HEADS),
                    "norm_w": jnp.tile(gdn_norm_w[l].astype(F32), N_HEADS).reshape(1, D_GROUP)},
            "sinks": sinks,
        })

    def run(x, n_seq, length, plans, state, has_cache):
        l_pad = plans["mix"].l_pad
        if l_pad > length:
            x = jnp.pad(x, ((0, 0), (0, l_pad - length), (0, 0)))
        x2 = x.reshape(n_seq * l_pad, D_MODEL)
        tm = min(512, x2.shape[0])
        y, st = _trunk(x2, state, layers, big, consts, plans, tm, has_cache)
        y = y.reshape(n_seq, l_pad, D_MODEL)[:, :length]
        return (y,) + _finish_states(st, n_seq)

    HK = SW_KV_HEADS * HEAD_DIM
    zeros = lambda *shape: jnp.zeros((1, bp) + shape, F32)
    prompt_state = {"ssm": zeros(2 * SSM_LANES), "hgrn": zeros(D_GROUP, HEAD_DIM),
                    "gdn": zeros(D_GROUP, HEAD_DIM), "conv": zeros(GD_CONV - 1, 3 * D_GROUP),
                    "swa_k": zeros(SW_WINDOW, HK), "swa_v": zeros(SW_WINDOW, HK)}
    sample_state = {
        "ssm": jnp.concatenate([state_ssm_re.reshape(DEPTH, bs, SSM_LANES),
                                state_ssm_im.reshape(DEPTH, bs, SSM_LANES)], axis=-1).astype(F32),
        "hgrn": state_hgrn.astype(F32).reshape(DEPTH, bs, D_GROUP, HEAD_DIM),
        "gdn": state_gdn.astype(F32).reshape(DEPTH, bs, D_GROUP, HEAD_DIM),
        "conv": state_gdn_conv.astype(F32),
        "swa_k": cache_swa_k.astype(F32).reshape(DEPTH, bs, SW_WINDOW, HK),
        "swa_v": cache_swa_v.astype(F32).reshape(DEPTH, bs, SW_WINDOW, HK)}

    yp, sre_p, sim_p, hg_p, gd_p, cv_p, sk_p, sv_p = run(x_prompt, bp, lp, plans_p, prompt_state, False)
    ys, sre_s, sim_s, hg_s, gd_s, cv_s, sk_s, sv_s = run(x_sample, bs, ls, plans_s, sample_state, True)
    return (yp, ys, sre_p, sre_s, sim_p, sim_s, hg_p, hg_s, gd_p, gd_s,
            cv_p, cv_s, sk_p, sk_s, sv_p, sv_s)
```
